```python
import math
import jax, jax.numpy as jnp
from jax import lax
import numpy as np

D_MODEL = 2048
BATCH = 1
SEQ = 16384
DEPTH = 2
DEC_BATCH = 8
DEC_SEQ = 32
PAST_LEN = 1024

CHUNK = 64
N_META = 16
EPS = 1e-6
NEG_INF = -1e30
CONV_WIDTH = 3
A_WIDTH = 512
MLA_HEADS = 8
Q_LORA = 512
KV_LORA = 512
NOPE_DIM = 128
ROPE_DIM = 64
V_DIM = 128
ROPE_THETA = 10000.0
Q_BLOCK = 128
ATTN_SCALE = 1.0 / math.sqrt(NOPE_DIM + ROPE_DIM)
SSM_GROUP = 16
SSM_GROUPS = 32
SSM_WIDTH = SSM_GROUP * SSM_GROUPS
SSM_STATE = 64
N_BRANCH = 3
IN_COLS = 3 * A_WIDTH + Q_LORA + KV_LORA + ROPE_DIM + SSM_WIDTH + N_BRANCH * D_MODEL
PEER_HEADS = 8
PEER_KEYS = 128
PEER_EXPERTS = PEER_KEYS * PEER_KEYS
PEER_QDIM = 256
PEER_TOPK = 16
TOKEN_BLOCK = 128

kernel_name = "hybrid_streaming_conv_mla_s5_peer_step"


def rmsnorm(x, g):
    xf = x.astype(jnp.float32)
    y = xf * lax.rsqrt(jnp.mean(xf * xf, axis=-1, keepdims=True) + EPS)
    return (y * g.astype(jnp.float32)).astype(x.dtype)


def rope(x, pos):
    half = ROPE_DIM // 2
    inv = 1.0 / (ROPE_THETA ** (jnp.arange(half, dtype=jnp.float32) / half))
    ang = pos[:, None] * inv[None, :]
    shape = (1, pos.shape[0]) + (1,) * (x.ndim - 3) + (half,)
    cos = jnp.cos(ang).reshape(shape)
    sin = jnp.sin(ang).reshape(shape)
    xf = x.astype(jnp.float32)
    x1, x2 = xf[..., :half], xf[..., half:]
    return jnp.concatenate([x1 * cos - x2 * sin, x2 * cos + x1 * sin], axis=-1).astype(x.dtype)


def chunk_index(idx):
    return jnp.where(idx < N_META, 0, 1 + (idx - N_META) // CHUNK)


def short_conv(a_b, a_c, a_h, conv_state, w_conv, b_conv, w_out):
    z = a_c * a_h
    L = z.shape[1]
    zp = jnp.concatenate([conv_state.astype(z.dtype), z], axis=1)
    y = b_conv
    for k in range(CONV_WIDTH):
        y = y + w_conv[k] * zp[:, k:k + L]
    return (a_b * y) @ w_out, zp[:, -(CONV_WIDTH - 1):]


def mla_prompt(q_nope, q_pe, lat, kpe, w_ukv):
    B, L, H, _ = q_nope.shape
    kv = (lat @ w_ukv).reshape(B, L, H, NOPE_DIM + V_DIM)
    k_nope, v = kv[..., :NOPE_DIM], kv[..., NOPE_DIM:]
    nb = -(-L // Q_BLOCK)
    pad = nb * Q_BLOCK - L
    qn = jnp.pad(q_nope, ((0, 0), (0, pad), (0, 0), (0, 0))).reshape(B, nb, Q_BLOCK, H, NOPE_DIM).swapaxes(0, 1)
    qp = jnp.pad(q_pe, ((0, 0), (0, pad), (0, 0), (0, 0))).reshape(B, nb, Q_BLOCK, H, ROPE_DIM).swapaxes(0, 1)
    key_chunk = chunk_index(jnp.arange(L))

    def block(args):
        qn_b, qp_b, start = args
        q_chunk = chunk_index(start + jnp.arange(Q_BLOCK))
        s = (jnp.einsum('bqhd,bkhd->bhqk', qn_b, k_nope)
             + jnp.einsum('bqhr,bkr->bhqk', qp_b, kpe)).astype(jnp.float32) * ATTN_SCALE
        mask = key_chunk[None, :] <= q_chunk[:, None]
        s = jnp.where(mask, s, NEG_INF)
        p = jax.nn.softmax(s, axis=-1).astype(v.dtype)
        return jnp.einsum('bhqk,bkhd->bqhd', p, v)

    o = lax.map(block, (qn, qp, jnp.arange(nb) * Q_BLOCK))
    return o.swapaxes(0, 1).reshape(B, nb * Q_BLOCK, H * V_DIM)[:, :L]


def mla_cached(q_nope, q_pe, lat_all, kpe_all, w_ukv):
    B, S, H, _ = q_nope.shape
    Lk = lat_all.shape[1]
    kv = (lat_all @ w_ukv).reshape(B, Lk, H, NOPE_DIM + V_DIM)
    k_nope, v = kv[..., :NOPE_DIM], kv[..., NOPE_DIM:]
    s = (jnp.einsum('bqhd,bkhd->bhqk', q_nope, k_nope)
         + jnp.einsum('bqhr,bkr->bhqk', q_pe, kpe_all)).astype(jnp.float32) * ATTN_SCALE
    p = jax.nn.softmax(s, axis=-1).astype(v.dtype)
    return jnp.einsum('bhqk,bkhd->bqhd', p, v).reshape(B, S, H * V_DIM)


def s5(u, h0_re, h0_im, lp):
    B, L, _ = u.shape
    f32 = jnp.float32
    uf = u.astype(f32)
    ug = uf.reshape(B, L, SSM_GROUPS, SSM_GROUP)
    ar = lp['ssm_a_re'].astype(f32)
    ai = lp['ssm_a_im'].astype(f32)
    dt = jnp.exp(lp['ssm_log_dt'].astype(f32))[:, None]
    mag = jnp.exp(dt * ar)
    abar_re = mag * jnp.cos(dt * ai)
    abar_im = mag * jnp.sin(dt * ai)
    nr, ni = abar_re - 1.0, abar_im
    den = ar * ar + ai * ai
    coef_re = (nr * ar + ni * ai) / den
    coef_im = (ni * ar - nr * ai) / den
    b_re = lp['ssm_b_re'].astype(f32)
    b_im = lp['ssm_b_im'].astype(f32)
    bbar_re = coef_re[..., None] * b_re - coef_im[..., None] * b_im
    bbar_im = coef_re[..., None] * b_im + coef_im[..., None] * b_re
    bu_re = jnp.einsum('gpc,blgc->blgp', bbar_re, ug)
    bu_im = jnp.einsum('gpc,blgc->blgp', bbar_im, ug)
    h0r = h0_re.astype(f32)
    h0i = h0_im.astype(f32)
    bu_re = bu_re.at[:, 0].add(abar_re * h0r - abar_im * h0i)
    bu_im = bu_im.at[:, 0].add(abar_re * h0i + abar_im * h0r)
    a_re = jnp.broadcast_to(abar_re, bu_re.shape)
    a_im = jnp.broadcast_to(abar_im, bu_im.shape)

    def combine(e1, e2):
        a1r, a1i, b1r, b1i = e1
        a2r, a2i, b2r, b2i = e2
        return (a1r * a2r - a1i * a2i,
                a1r * a2i + a1i * a2r,
                a2r * b1r - a2i * b1i + b2r,
                a2r * b1i + a2i * b1r + b2i)

    _, _, hr, hi = lax.associative_scan(combine, (a_re, a_im, bu_re, bu_im), axis=1)
    y = (jnp.einsum('gcp,blgp->blgc', lp['ssm_c_re'].astype(f32), hr)
         - jnp.einsum('gcp,blgp->blgc', lp['ssm_c_im'].astype(f32), hi)).reshape(B, L, SSM_WIDTH)
    y = jax.nn.gelu(y + lp['ssm_d'].astype(f32) * uf).astype(u.dtype)
    ga, gb = jnp.split(y @ lp['w_glu'], 2, axis=-1)
    return ga * jax.nn.sigmoid(gb), hr[:, -1], hi[:, -1]


def mixer(xn, pos, conv_state, h_re, h_im, past, lp):
    B, L, _ = xn.shape
    sizes = (A_WIDTH, A_WIDTH, A_WIDTH, Q_LORA, KV_LORA, ROPE_DIM, SSM_WIDTH)
    cuts = np.cumsum(sizes).tolist()
    a_b, a_c, a_h, c_q, c_kv, k_pe, u, gate_pre = jnp.split(xn @ lp['w_in'], cuts, axis=-1)
    gates = jax.nn.sigmoid(gate_pre + lp['b_gate']).reshape(B, L, N_BRANCH, D_MODEL)
    y_a, new_conv = short_conv(a_b, a_c, a_h, conv_state, lp['conv_w'], lp['conv_b'], lp['w_a_out'])
    q = (rmsnorm(c_q, lp['q_norm']) @ lp['w_uq']).reshape(B, L, MLA_HEADS, NOPE_DIM + ROPE_DIM)
    q_nope = q[..., :NOPE_DIM]
    q_pe = rope(q[..., NOPE_DIM:], pos)
    lat = rmsnorm(c_kv, lp['kv_norm'])
    kpe = rope(k_pe, pos)
    if past is None:
        o = mla_prompt(q_nope, q_pe, lat, kpe, lp['w_ukv'])
    else:
        past_lat, past_kpe = past
        lat_all = jnp.concatenate([past_lat.astype(lat.dtype), lat], axis=1)
        kpe_all = jnp.concatenate([past_kpe.astype(kpe.dtype), kpe], axis=1)
        o = mla_cached(q_nope, q_pe, lat_all, kpe_all, lp['w_ukv'])
    y_b = o @ lp['w_b_out']
    y_c, new_re, new_im = s5(u, h_re, h_im, lp)
    merged = gates[:, :, 0] * y_a + gates[:, :, 1] * y_b + gates[:, :, 2] * y_c.astype(y_a.dtype)
    return merged @ lp['w_o'], new_conv, lat, kpe, new_re, new_im


def peer(xn, w_q, k1, k2, e_u, e_v):
    B, L, D = xn.shape
    t = xn.reshape(B * L, D)
    T = t.shape[0]
    nb = -(-T // TOKEN_BLOCK)
    t = jnp.pad(t, ((0, nb * TOKEN_BLOCK - T), (0, 0))).reshape(nb, TOKEN_BLOCK, D)
    half = PEER_QDIM // 2

    def block(tb):
        q = (tb @ w_q).reshape(TOKEN_BLOCK, PEER_HEADS, PEER_QDIM)
        s1 = jnp.einsum('thd,kd->thk', q[..., :half], k1).astype(jnp.float32)
        s2 = jnp.einsum('thd,kd->thk', q[..., half:], k2).astype(jnp.float32)
        v1, i1 = lax.top_k(s1, PEER_TOPK)
        v2, i2 = lax.top_k(s2, PEER_TOPK)
        cand = (v1[..., :, None] + v2[..., None, :]).reshape(TOKEN_BLOCK, PEER_HEADS, PEER_TOPK * PEER_TOPK)
        vals, flat = lax.top_k(cand, PEER_TOPK)
        idx = (jnp.take_along_axis(i1, flat // PEER_TOPK, axis=-1) * PEER_KEYS
               + jnp.take_along_axis(i2, flat % PEER_TOPK, axis=-1))
        g = jax.nn.softmax(vals, axis=-1)
        u_sel = e_u[idx]
        v_sel = e_v[idx]
        h = jax.nn.gelu(jnp.einsum('thkd,td->thk', u_sel, tb).astype(jnp.float32))
        w = (g * h).astype(tb.dtype)
        return jnp.einsum('thk,thkd->td', w, v_sel)

    out = lax.map(block, t).reshape(nb * TOKEN_BLOCK, D)[:T]
    return out.reshape(B, L, D)


def setup_inputs(seed: int = 0) -> dict:
    key = jax.random.key(seed)
    ks = jax.random.split(key, 40)
    f32 = jnp.float32
    nrm = lambda k, shape, s: jax.random.normal(k, shape, f32) * s
    gain = lambda k, shape: 1.0 + 0.01 * jax.random.normal(k, shape, f32)
    a_im0 = jnp.pi * jnp.arange(SSM_STATE, dtype=f32)
    return {
        'x_prompt': nrm(ks[0], (BATCH, SEQ, D_MODEL), 1.0),
        'x_sample': nrm(ks[1], (DEC_BATCH, DEC_SEQ, D_MODEL), 1.0),
        'cache_ckv': nrm(ks[2], (DEPTH, DEC_BATCH, PAST_LEN, KV_LORA), 1.0),
        'cache_kpe': nrm(ks[3], (DEPTH, DEC_BATCH, PAST_LEN, ROPE_DIM), 1.0),
        'state_conv': nrm(ks[4], (DEPTH, DEC_BATCH, CONV_WIDTH - 1, A_WIDTH), 1.0),
        'state_ssm_re': nrm(ks[5], (DEPTH, DEC_BATCH, SSM_GROUPS, SSM_STATE), 1.0),
        'state_ssm_im': nrm(ks[6], (DEPTH, DEC_BATCH, SSM_GROUPS, SSM_STATE), 1.0),
        'meta_tokens': nrm(ks[7], (N_META, D_MODEL), 1.0),
        'norm_mix': gain(ks[8], (DEPTH, D_MODEL)),
        'norm_ffn': gain(ks[9], (DEPTH, D_MODEL)),
        'w_in': nrm(ks[10], (DEPTH, D_MODEL, IN_COLS), D_MODEL ** -0.5),
        'b_gate': nrm(ks[11], (DEPTH, N_BRANCH * D_MODEL), 0.01),
        'conv_w': nrm(ks[12], (DEPTH, CONV_WIDTH, A_WIDTH), CONV_WIDTH ** -0.5),
        'conv_b': nrm(ks[13], (DEPTH, A_WIDTH), 0.01),
        'w_a_out': nrm(ks[14], (DEPTH, A_WIDTH, D_MODEL), A_WIDTH ** -0.5),
        'q_norm': gain(ks[15], (DEPTH, Q_LORA)),
        'w_uq': nrm(ks[16], (DEPTH, Q_LORA, MLA_HEADS * (NOPE_DIM + ROPE_DIM)), Q_LORA ** -0.5),
        'kv_norm': gain(ks[17], (DEPTH, KV_LORA)),
        'w_ukv': nrm(ks[18], (DEPTH, KV_LORA, MLA_HEADS * (NOPE_DIM + V_DIM)), KV_LORA ** -0.5),
        'w_b_out': nrm(ks[19], (DEPTH, MLA_HEADS * V_DIM, D_MODEL), (MLA_HEADS * V_DIM) ** -0.5),
        'ssm_a_re': -0.5 + nrm(ks[20], (DEPTH, SSM_GROUPS, SSM_STATE), 0.01),
        'ssm_a_im': a_im0 + nrm(ks[21], (DEPTH, SSM_GROUPS, SSM_STATE), 0.01),
        'ssm_b_re': nrm(ks[22], (DEPTH, SSM_GROUPS, SSM_STATE, SSM_GROUP), (2 * SSM_GROUP) ** -0.5),
        'ssm_b_im': nrm(ks[23], (DEPTH, SSM_GROUPS, SSM_STATE, SSM_GROUP), (2 * SSM_GROUP) ** -0.5),
        'ssm_c_re': nrm(ks[24], (DEPTH, SSM_GROUPS, SSM_GROUP, SSM_STATE), (2 * SSM_STATE) ** -0.5),
        'ssm_c_im': nrm(ks[25], (DEPTH, SSM_GROUPS, SSM_GROUP, SSM_STATE), (2 * SSM_STATE) ** -0.5),
        'ssm_log_dt': jax.random.uniform(ks[26], (DEPTH, SSM_GROUPS), f32, math.log(0.001), math.log(0.1)),
        'ssm_d': nrm(ks[27], (DEPTH, SSM_WIDTH), 1.0),
        'w_glu': nrm(ks[28], (DEPTH, SSM_WIDTH, 2 * D_MODEL), SSM_WIDTH ** -0.5),
        'w_o': nrm(ks[29], (DEPTH, D_MODEL, D_MODEL), D_MODEL ** -0.5),
        'peer_wq': nrm(ks[30], (DEPTH, D_MODEL, PEER_HEADS * PEER_QDIM), D_MODEL ** -0.5),
        'peer_k1': nrm(ks[31], (DEPTH, PEER_KEYS, PEER_QDIM // 2), (PEER_QDIM // 2) ** -0.5),
        'peer_k2': nrm(ks[32], (DEPTH, PEER_KEYS, PEER_QDIM // 2), (PEER_QDIM // 2) ** -0.5),
        'peer_u': nrm(ks[33], (DEPTH, PEER_EXPERTS, D_MODEL), D_MODEL ** -0.5),
        'peer_v': nrm(ks[34], (DEPTH, PEER_EXPERTS, D_MODEL), PEER_HEADS ** -0.5),
        'norm_final': gain(ks[35], (D_MODEL,)),
    }


def reference(x_prompt, x_sample, cache_ckv, cache_kpe, state_conv, state_ssm_re, state_ssm_im,
              meta_tokens, norm_mix, norm_ffn, w_in, b_gate, conv_w, conv_b, w_a_out,
              q_norm, w_uq, kv_norm, w_ukv, w_b_out,
              ssm_a_re, ssm_a_im, ssm_b_re, ssm_b_im, ssm_c_re, ssm_c_im, ssm_log_dt, ssm_d,
              w_glu, w_o, peer_wq, peer_k1, peer_k2, peer_u, peer_v, norm_final):
    B = x_prompt.shape[0]
    meta = jnp.broadcast_to(meta_tokens.astype(x_prompt.dtype)[None], (B, N_META, D_MODEL))
    xp = jnp.concatenate([meta, x_prompt], axis=1)
    xs = x_sample
    Lp = xp.shape[1]
    pos_p = jnp.arange(Lp, dtype=jnp.float32)
    pos_s = PAST_LEN + jnp.arange(xs.shape[1], dtype=jnp.float32)
    p_conv0 = jnp.zeros((B, CONV_WIDTH - 1, A_WIDTH), xp.dtype)
    p_h0 = jnp.zeros((B, SSM_GROUPS, SSM_STATE), jnp.float32)

    p_ckv, p_kpe, p_conv, p_re, p_im = [], [], [], [], []
    s_ckv, s_kpe, s_conv, s_re, s_im = [], [], [], [], []
    for l in range(DEPTH):
        lp = {
            'w_in': w_in[l], 'b_gate': b_gate[l], 'conv_w': conv_w[l], 'conv_b': conv_b[l],
            'w_a_out': w_a_out[l], 'q_norm': q_norm[l], 'w_uq': w_uq[l], 'kv_norm': kv_norm[l],
            'w_ukv': w_ukv[l], 'w_b_out': w_b_out[l], 'ssm_a_re': ssm_a_re[l], 'ssm_a_im': ssm_a_im[l],
            'ssm_b_re': ssm_b_re[l], 'ssm_b_im': ssm_b_im[l], 'ssm_c_re': ssm_c_re[l],
            'ssm_c_im': ssm_c_im[l], 'ssm_log_dt': ssm_log_dt[l], 'ssm_d': ssm_d[l],
            'w_glu': w_glu[l], 'w_o': w_o[l],
        }
        m, cv, lat, kpe, hr, hi = mixer(rmsnorm(xp, norm_mix[l]), pos_p, p_conv0, p_h0, p_h0, None, lp)
        xp = xp + m
        xp = xp + peer(rmsnorm(xp, norm_ffn[l]), peer_wq[l], peer_k1[l], peer_k2[l], peer_u[l], peer_v[l])
        p_ckv.append(lat); p_kpe.append(kpe); p_conv.append(cv); p_re.append(hr); p_im.append(hi)
        m, cv, lat, kpe, hr, hi = mixer(rmsnorm(xs, norm_mix[l]), pos_s, state_conv[l],
                                        state_ssm_re[l], state_ssm_im[l],
                                        (cache_ckv[l], cache_kpe[l]), lp)
        xs = xs + m
        xs = xs + peer(rmsnorm(xs, norm_ffn[l]), peer_wq[l], peer_k1[l], peer_k2[l], peer_u[l], peer_v[l])
        s_ckv.append(lat); s_kpe.append(kpe); s_conv.append(cv); s_re.append(hr); s_im.append(hi)

    y_prompt = rmsnorm(xp, norm_final)[:, N_META:]
    y_sample = rmsnorm(xs, norm_final)
    return (y_prompt, y_sample,
            jnp.stack(p_ckv), jnp.stack(p_kpe), jnp.stack(p_conv), jnp.stack(p_re), jnp.stack(p_im),
            jnp.stack(s_ckv), jnp.stack(s_kpe), jnp.stack(s_conv), jnp.stack(s_re), jnp.stack(s_im))
```

```python
import functools
import math

import jax
import jax.numpy as jnp
from jax import lax
from jax.experimental import pallas as pl
from jax.experimental.pallas import tpu as pltpu

F32 = jnp.float32
BF16 = jnp.bfloat16

D_MODEL = 2048
CHUNK = 64
N_META = 16
EPS = 1e-6
NEG_INF = -1e30
A_WIDTH = 512
CONV_WIDTH = 3
MLA_HEADS = 8
Q_LORA = 512
KV_LORA = 512
NOPE_DIM = 128
ROPE_DIM = 64
V_DIM = 128
QK_DIM = NOPE_DIM + ROPE_DIM
ROPE_THETA = 10000.0
ATTN_SCALE = 1.0 / math.sqrt(NOPE_DIM + ROPE_DIM)
SSM_GROUP = 16
SSM_GROUPS = 32
SSM_WIDTH = SSM_GROUP * SSM_GROUPS
SSM_STATE = 64
SSM_CH = SSM_GROUPS * SSM_STATE
PEER_HEADS = 8
PEER_KEYS = 128
PEER_QDIM = 256
PEER_TOPK = 16

FRONT = CHUNK
ROW_ALIGN = 512
VMEM_LIMIT = 60 * 1024 * 1024

C_AB, C_AC, C_AH, C_Q, C_KV, C_U, C_KPE, N_SMALL = 0, 512, 1024, 1536, 2048, 2560, 3072, 3200


def _cparams(*sem):
    return pltpu.CompilerParams(dimension_semantics=sem, vmem_limit_bytes=VMEM_LIMIT)


def _rms(x, g):
    return x * lax.rsqrt(jnp.mean(x * x, axis=-1, keepdims=True) + EPS) * g


def _sigmoid(x):
    return 1.0 / (1.0 + jnp.exp(-x))


def _gelu(x):
    return 0.5 * x * (1.0 + jnp.tanh(math.sqrt(2.0 / math.pi) * (x + 0.044715 * (x * x * x))))


def _dot(a, b):
    return jnp.dot(a, b, preferred_element_type=F32)


def _dot_nt(a, b):
    return lax.dot_general(a, b, (((1,), (1,)), ((), ())), preferred_element_type=F32)


def _in_proj_kernel(x_ref, g_ref, w_ref, o_ref):
    xn = _rms(x_ref[...], g_ref[...]).astype(BF16)
    o_ref[...] = _dot(xn, w_ref[...])


def _in_proj(x, g, w_small):
    R = x.shape[0]
    tb = 256
    return pl.pallas_call(
        _in_proj_kernel,
        grid=(R // tb,),
        in_specs=[pl.BlockSpec((tb, D_MODEL), lambda i: (i, 0)),
                  pl.BlockSpec((1, D_MODEL), lambda i: (0, 0)),
                  pl.BlockSpec((D_MODEL, N_SMALL), lambda i: (0, 0))],
        out_specs=pl.BlockSpec((tb, N_SMALL), lambda i: (i, 0)),
        out_shape=jax.ShapeDtypeStruct((R, N_SMALL), F32),
        compiler_params=_cparams("parallel"),
        name="in_proj",
    )(x, g, w_small)


def _conv_kernel(p_ref, w_ref, b_ref, st_ref, za_ref, pc_ref, sc_ref, zbuf, *, p_blk, p_off, n_blk, tb, dec_seq):
    i = pl.program_id(0)
    ab = p_ref[:, C_AB:C_AB + A_WIDTH]
    z = p_ref[:, C_AC:C_AC + A_WIDTH] * p_ref[:, C_AH:C_AH + A_WIDTH]
    w0, w1, w2 = w_ref[0:1, :], w_ref[1:2, :], w_ref[2:3, :]
    b = b_ref[...]

    @pl.when(i == 0)
    def _():
        zbuf[0:8, :] = jnp.zeros((8, A_WIDTH), F32)

    @pl.when(i < n_blk - 1)
    def _():
        zbuf[8:8 + tb, :] = z
        y = b + w0 * zbuf[6:6 + tb, :] + w1 * zbuf[7:7 + tb, :] + w2 * zbuf[8:8 + tb, :]
        za_ref[...] = (ab * y).astype(BF16)
        zbuf[0:8, :] = zbuf[tb:tb + 8, :]

    @pl.when(i == p_blk)
    def _():
        pc_ref[...] = z[p_off - 1:p_off + 1, :]

    @pl.when(i == n_blk - 1)
    def _():
        for s in range(tb // dec_seq):
            r0 = s * dec_seq
            zbuf[6:8, :] = st_ref[2 * s:2 * s + 2, :]
            zbuf[8:8 + dec_seq, :] = z[r0:r0 + dec_seq, :]
            y = (b + w0 * zbuf[6:6 + dec_seq, :] + w1 * zbuf[7:7 + dec_seq, :]
                 + w2 * zbuf[8:8 + dec_seq, :])
            za_ref[r0:r0 + dec_seq, :] = (ab[r0:r0 + dec_seq, :] * y).astype(BF16)
            sc_ref[2 * s:2 * s + 2, :] = z[r0 + dec_seq - 2:r0 + dec_seq, :]


def _conv(proj, conv_w, conv_b, state, lay):
    R = proj.shape[0]
    tb = lay["ns"]
    n_blk = R // tb
    last = lay["p_end"] - 1
    kern = functools.partial(_conv_kernel, p_blk=last // tb, p_off=last % tb, n_blk=n_blk, tb=tb,
                             dec_seq=lay["dec_seq"])
    assert last % tb >= 1
    nseq = tb // lay["dec_seq"]
    return pl.pallas_call(
        kern,
        grid=(n_blk,),
        in_specs=[pl.BlockSpec((tb, 3 * A_WIDTH), lambda i: (i, 0)),
                  pl.BlockSpec((CONV_WIDTH, A_WIDTH), lambda i: (0, 0)),
                  pl.BlockSpec((1, A_WIDTH), lambda i: (0, 0)),
                  pl.BlockSpec((2 * nseq, A_WIDTH), lambda i: (0, 0))],
        out_specs=[pl.BlockSpec((tb, A_WIDTH), lambda i: (i, 0)),
                   pl.BlockSpec((2, A_WIDTH), lambda i: (0, 0)),
                   pl.BlockSpec((2 * nseq, A_WIDTH), lambda i: (0, 0))],
        out_shape=[jax.ShapeDtypeStruct((R, A_WIDTH), BF16),
                   jax.ShapeDtypeStruct((2, A_WIDTH), F32),
                   jax.ShapeDtypeStruct((2 * nseq, A_WIDTH), F32)],
        scratch_shapes=[pltpu.VMEM((tb + 8, A_WIDTH), F32)],
        compiler_params=_cparams("arbitrary"),
        name="short_conv",
    )(proj, conv_w, conv_b, state)


def _qkv_kernel(cq_ref, ckv_ref, kp_ref, cs_ref, qn_ref, wq_ref, kn_ref, wkv_ref,
                q_ref, k_ref, v_ref, lat_ref, kpe_ref):
    cs = cs_ref[...]
    qc = _rms(cq_ref[...], qn_ref[...]).astype(BF16)
    q = _dot(qc, wq_ref[...])
    lat = _rms(ckv_ref[...], kn_ref[...])
    lat_ref[...] = lat
    kv = _dot(lat.astype(BF16), wkv_ref[...])
    t = kp_ref[...] * cs
    kpe = (t + pltpu.roll(t, ROPE_DIM, 1))[:, 0:ROPE_DIM]
    kpe_ref[...] = kpe
    kpe_b = kpe.astype(BF16)
    for h in range(MLA_HEADS):
        c = h * 256
        q_ref[h, :, 0:NOPE_DIM] = (q[:, c:c + NOPE_DIM] * ATTN_SCALE).astype(BF16)
        t = q[:, c + NOPE_DIM:c + 256] * cs
        qpe = (t + pltpu.roll(t, ROPE_DIM, 1))[:, 0:ROPE_DIM]
        q_ref[h, :, NOPE_DIM:QK_DIM] = (qpe * ATTN_SCALE).astype(BF16)
        k_ref[h, :, 0:NOPE_DIM] = kv[:, c:c + NOPE_DIM].astype(BF16)
        k_ref[h, :, NOPE_DIM:QK_DIM] = kpe_b
        v_ref[h, :, :] = kv[:, c + NOPE_DIM:c + 256].astype(BF16)


def _qkv(proj, cs, q_norm, wq_aug, kv_norm, w_ukv):
    R = proj.shape[0]
    tb = 256
    H = MLA_HEADS
    return pl.pallas_call(
        _qkv_kernel,
        grid=(R // tb,),
        in_specs=[pl.BlockSpec((tb, Q_LORA), lambda i: (i, C_Q // Q_LORA)),
                  pl.BlockSpec((tb, KV_LORA), lambda i: (i, C_KV // KV_LORA)),
                  pl.BlockSpec((tb, 2 * ROPE_DIM), lambda i: (i, C_KPE // (2 * ROPE_DIM))),
                  pl.BlockSpec((tb, 2 * ROPE_DIM), lambda i: (i, 0)),
                  pl.BlockSpec((1, Q_LORA), lambda i: (0, 0)),
                  pl.BlockSpec((Q_LORA, H * 256), lambda i: (0, 0)),
                  pl.BlockSpec((1, KV_LORA), lambda i: (0, 0)),
                  pl.BlockSpec((KV_LORA, H * 256), lambda i: (0, 0))],
        out_specs=[pl.BlockSpec((H, tb, QK_DIM), lambda i: (0, i, 0)),
                   pl.BlockSpec((H, tb, QK_DIM), lambda i: (0, i, 0)),
                   pl.BlockSpec((H, tb, V_DIM), lambda i: (0, i, 0)),
                   pl.BlockSpec((tb, KV_LORA), lambda i: (i, 0)),
                   pl.BlockSpec((tb, ROPE_DIM), lambda i: (i, 0))],
        out_shape=[jax.ShapeDtypeStruct((H, R, QK_DIM), BF16),
                   jax.ShapeDtypeStruct((H, R, QK_DIM), BF16),
                   jax.ShapeDtypeStruct((H, R, V_DIM), BF16),
                   jax.ShapeDtypeStruct((R, KV_LORA), F32),
                   jax.ShapeDtypeStruct((R, ROPE_DIM), F32)],
        compiler_params=_cparams("parallel"),
        name="qkv_rope",
    )(proj, proj, proj, cs, q_norm, wq_aug, kv_norm, w_ukv)


def _flash_kernel(q_ref, k_ref, v_ref, o_ref, m_ref, l_ref, acc_ref, *, bq, bk):
    i = pl.program_id(1)
    j = pl.program_id(2)

    @pl.when(j == 0)
    def _():
        m_ref[...] = jnp.full(m_ref.shape, NEG_INF, F32)
        l_ref[...] = jnp.zeros(l_ref.shape, F32)
        acc_ref[...] = jnp.zeros(acc_ref.shape, F32)

    def step(masked):
        s = _dot_nt(q_ref[...], k_ref[...])
        if masked:
            qrow = i * bq + lax.broadcasted_iota(jnp.int32, (bq, bk), 0)
            krow = j * bk + lax.broadcasted_iota(jnp.int32, (bq, bk), 1)
            ok = (krow // CHUNK <= qrow // CHUNK) & (krow >= FRONT - N_META)
            s = jnp.where(ok, s, NEG_INF)
        m_prev = m_ref[...]
        m_new = jnp.maximum(m_prev, jnp.max(s, axis=1, keepdims=True))
        alpha = jnp.exp(m_prev - m_new)
        p = jnp.exp(s - m_new)
        l_ref[...] = alpha * l_ref[...] + jnp.sum(p, axis=1, keepdims=True)
        acc_ref[...] = alpha * acc_ref[...] + _dot(p.astype(BF16), v_ref[...])
        m_ref[...] = m_new

    needs_mask = (j == i) | (j == 0)

    @pl.when(needs_mask)
    def _():
        step(True)

    @pl.when(jnp.logical_not(needs_mask) & (j < i))
    def _():
        step(False)

    @pl.when(j == i)
    def _():
        row = i * bq + lax.broadcasted_iota(jnp.int32, acc_ref.shape, 0)
        o = jnp.where(row >= FRONT - N_META, acc_ref[...] / l_ref[...], 0.0)
        o_ref[...] = o.astype(o_ref.dtype)


def _flash(q, k, v):
    H, R, _ = q.shape
    bq = bk = 512
    n = R // bq
    kern = functools.partial(_flash_kernel, bq=bq, bk=bk)
    return pl.pallas_call(
        kern,
        grid=(H, n, n),
        in_specs=[pl.BlockSpec((None, bq, QK_DIM), lambda h, i, j: (h, i, 0)),
                  pl.BlockSpec((None, bk, QK_DIM), lambda h, i, j: (h, jnp.minimum(j, i), 0)),
                  pl.BlockSpec((None, bk, V_DIM), lambda h, i, j: (h, jnp.minimum(j, i), 0))],
        out_specs=pl.BlockSpec((bq, V_DIM), lambda h, i, j: (i, h)),
        out_shape=jax.ShapeDtypeStruct((R, H * V_DIM), BF16),
        scratch_shapes=[pltpu.VMEM((bq, 1), F32), pltpu.VMEM((bq, 1), F32), pltpu.VMEM((bq, V_DIM), F32)],
        compiler_params=_cparams("parallel", "parallel", "arbitrary"),
        name="prompt_attention",
    )(q, k, v)


def _cached_attn_kernel(q_ref, lat_ref, kpe_ref, w_ref, o_ref):
    kv = _dot(lat_ref[...], w_ref[...]).astype(BF16)
    kpe = kpe_ref[...]
    for h in range(MLA_HEADS):
        c = h * 256
        qh = q_ref[h]
        s = _dot_nt(qh[:, 0:NOPE_DIM], kv[:, c:c + NOPE_DIM]) + _dot_nt(qh[:, NOPE_DIM:QK_DIM], kpe)
        m = jnp.max(s, axis=1, keepdims=True)
        p = jnp.exp(s - m)
        p = p / jnp.sum(p, axis=1, keepdims=True)
        o_ref[:, h * V_DIM:(h + 1) * V_DIM] = _dot(p.astype(BF16), kv[:, c + NOPE_DIM:c + 256]).astype(o_ref.dtype)


def _cached_attn(q, lat_all, kpe_all, w_ukv, lay):
    H = MLA_HEADS
    B, Lk, _ = lat_all.shape
    S = lay["dec_seq"]
    blk0 = lay["s0"] // S
    return pl.pallas_call(
        _cached_attn_kernel,
        grid=(B,),
        in_specs=[pl.BlockSpec((H, S, QK_DIM), lambda b: (0, blk0 + b, 0)),
                  pl.BlockSpec((None, Lk, KV_LORA), lambda b: (b, 0, 0)),
                  pl.BlockSpec((None, Lk, ROPE_DIM), lambda b: (b, 0, 0)),
                  pl.BlockSpec((KV_LORA, H * 256), lambda b: (0, 0))],
        out_specs=pl.BlockSpec((S, H * V_DIM), lambda b: (b, 0)),
        out_shape=jax.ShapeDtypeStruct((B * S, H * V_DIM), BF16),
        compiler_params=_cparams("parallel"),
        name="sample_attention",
    )(q, lat_all, kpe_all, w_ukv)


def _s5_kernel(u_ref, bm_ref, a_ref, cm_ref, d_ref, h0_ref, y_ref, ps_ref, ss_ref, hre, him, carry,
               *, p_blk, p_off, n_blk, tb, dec_seq):
    i = pl.program_id(0)
    u = u_ref[...]
    hb = _dot(u.astype(BF16), bm_ref[...])
    hre[...] = hb[:, 0:SSM_CH]
    him[...] = hb[:, SSM_CH:2 * SSM_CH]
    ar = a_ref[0:1, :]
    ai = a_ref[1:2, :]

    def scan(start, n, hr, hi):
        def body(t, c):
            hr, hi = c
            r = start + t
            nr = ar * hr - ai * hi + hre[pl.ds(r, 1), :]
            ni = ar * hi + ai * hr + him[pl.ds(r, 1), :]
            hre[pl.ds(r, 1), :] = nr
            him[pl.ds(r, 1), :] = ni
            return nr, ni
        return lax.fori_loop(0, n, body, (hr, hi), unroll=2)

    @pl.when(i == 0)
    def _():
        carry[...] = jnp.zeros(carry.shape, F32)

    @pl.when(i < n_blk - 1)
    def _():
        hr, hi = scan(0, tb, carry[0:1, :], carry[1:2, :])
        carry[0:1, :] = hr
        carry[1:2, :] = hi

    @pl.when(i == p_blk)
    def _():
        ps_ref[0:1, :] = hre[p_off:p_off + 1, :]
        ps_ref[1:2, :] = him[p_off:p_off + 1, :]

    @pl.when(i == n_blk - 1)
    def _():
        for s in range(tb // dec_seq):
            hr, hi = scan(s * dec_seq, dec_seq, h0_ref[0, s:s + 1, :], h0_ref[1, s:s + 1, :])
            ss_ref[0, s:s + 1, :] = hr
            ss_ref[1, s:s + 1, :] = hi

    y = _dot(hre[...].astype(BF16), cm_ref[0]) + _dot(him[...].astype(BF16), cm_ref[1])
    y_ref[...] = _gelu(y + d_ref[...] * u).astype(BF16)


def _s5(proj, bmat, abar, cmat, d, h0, lay):
    R = proj.shape[0]
    tb = lay["ns"]
    n_blk = R // tb
    last = lay["p_end"] - 1
    nseq = tb // lay["dec_seq"]
    kern = functools.partial(_s5_kernel, p_blk=last // tb, p_off=last % tb, n_blk=n_blk, tb=tb,
                             dec_seq=lay["dec_seq"])
    return pl.pallas_call(
        kern,
        grid=(n_blk,),
        in_specs=[pl.BlockSpec((tb, SSM_WIDTH), lambda i: (i, C_U // SSM_WIDTH)),
                  pl.BlockSpec((SSM_WIDTH, 2 * SSM_CH), lambda i: (0, 0)),
                  pl.BlockSpec((2, SSM_CH), lambda i: (0, 0)),
                  pl.BlockSpec((2, SSM_CH, SSM_WIDTH), lambda i: (0, 0, 0)),
                  pl.BlockSpec((1, SSM_WIDTH), lambda i: (0, 0)),
                  pl.BlockSpec((2, nseq, SSM_CH), lambda i: (0, 0, 0))],
        out_specs=[pl.BlockSpec((tb, SSM_WIDTH), lambda i: (i, 0)),
                   pl.BlockSpec((2, SSM_CH), lambda i: (0, 0)),
                   pl.BlockSpec((2, nseq, SSM_CH), lambda i: (0, 0, 0))],
        out_shape=[jax.ShapeDtypeStruct((R, SSM_WIDTH), BF16),
                   jax.ShapeDtypeStruct((2, SSM_CH), F32),
                   jax.ShapeDtypeStruct((2, nseq, SSM_CH), F32)],
        scratch_shapes=[pltpu.VMEM((tb, SSM_CH), F32), pltpu.VMEM((tb, SSM_CH), F32),
                        pltpu.VMEM((2, SSM_CH), F32)],
        compiler_params=_cparams("arbitrary"),
        name="s5_scan",
    )(proj, bmat, abar, cmat, d, h0)


MERGE_TN = 256
MERGE_NC = D_MODEL // MERGE_TN


def _merge_kernel(x_ref, g_ref, za_ref, o_ref, yc_ref, wg0_ref, wg1_ref, wg2_ref, b0_ref, b1_ref, b2_ref,
                  wa_ref, wb_ref, wga_ref, wgb_ref, wo_ref, out_ref, xn_s, mg_s):
    c = pl.program_id(1)

    @pl.when(c == 0)
    def _():
        xn_s[...] = _rms(x_ref[...], g_ref[...]).astype(BF16)

    xn = xn_s[...]
    g0 = _sigmoid(_dot(xn, wg0_ref[...]) + b0_ref[...])
    g1 = _sigmoid(_dot(xn, wg1_ref[...]) + b1_ref[...])
    g2 = _sigmoid(_dot(xn, wg2_ref[...]) + b2_ref[...])
    yc = yc_ref[...]
    y_a = _dot(za_ref[...], wa_ref[...])
    y_b = _dot(o_ref[...], wb_ref[...])
    y_c = _dot(yc, wga_ref[...]) * _sigmoid(_dot(yc, wgb_ref[...]))
    mg_s[c] = (g0 * y_a + g1 * y_b + g2 * y_c).astype(BF16)

    @pl.when(c == MERGE_NC - 1)
    def _():
        acc = x_ref[...]
        for cc in range(MERGE_NC):
            acc = acc + _dot(mg_s[cc], wo_ref[cc * MERGE_TN:(cc + 1) * MERGE_TN, :])
        out_ref[...] = acc


def _merge(x, g, za, o, yc, w_gate, b_gate, w_a, w_b, w_glu, w_o):
    R = x.shape[0]
    tb = 512
    tn, nc = MERGE_TN, MERGE_NC
    row = lambda w: pl.BlockSpec((tb, w), lambda i, c: (i, 0))
    col = lambda k, off: pl.BlockSpec((k, tn), lambda i, c: (0, off + c))
    return pl.pallas_call(
        _merge_kernel,
        grid=(R // tb, nc),
        in_specs=[row(D_MODEL), pl.BlockSpec((1, D_MODEL), lambda i, c: (0, 0)),
                  row(A_WIDTH), row(MLA_HEADS * V_DIM), row(SSM_WIDTH),
                  col(D_MODEL, 0), col(D_MODEL, nc), col(D_MODEL, 2 * nc),
                  col(1, 0), col(1, nc), col(1, 2 * nc),
                  col(A_WIDTH, 0), col(MLA_HEADS * V_DIM, 0), col(SSM_WIDTH, 0), col(SSM_WIDTH, nc),
                  pl.BlockSpec((D_MODEL, D_MODEL), lambda i, c: (0, 0))],
        out_specs=row(D_MODEL),
        out_shape=jax.ShapeDtypeStruct((R, D_MODEL), F32),
        scratch_shapes=[pltpu.VMEM((tb, D_MODEL), BF16), pltpu.VMEM((nc, tb, tn), BF16)],
        compiler_params=_cparams("parallel", "arbitrary"),
        name="branch_merge",
    )(x, g, za, o, yc, w_gate, w_gate, w_gate, b_gate, b_gate, b_gate, w_a, w_b, w_glu, w_glu, w_o)


def _top_desc(cur, n_out, out_ref):
    n = cur.shape[0]
    iota = lax.broadcasted_iota(jnp.int32, cur.shape, 0).astype(F32)
    for r in range(n_out):
        m = jnp.max(cur, axis=0, keepdims=True)
        out_ref[r:r + 1, :] = m
        if r + 1 < n_out:
            first = jnp.min(jnp.where(cur == m, iota, float(n)), axis=0, keepdims=True)
            cur = jnp.where(iota == first, -jnp.inf, cur)


def _route_kernel(x_ref, g_ref, wq_ref, k1_ref, k2_ref, xn_ref, s1_ref, s2_ref, e1_ref, e2_ref, tau_ref,
                  v1_s, v2_s, cand_s, val_s):
    K = PEER_TOPK
    xn = _rms(x_ref[...], g_ref[...]).astype(BF16)
    xn_ref[...] = xn
    q = _dot(xn, wq_ref[...]).astype(BF16)
    half = PEER_QDIM // 2
    for h in range(PEER_HEADS):
        c = h * PEER_QDIM
        s1_ref[h] = _dot_nt(k1_ref[...], q[:, c:c + half])
        s2_ref[h] = _dot_nt(k2_ref[...], q[:, c + half:c + PEER_QDIM])

    def head(h, carry):
        s1 = s1_ref[h]
        s2 = s2_ref[h]
        _top_desc(s1, K, v1_s)
        _top_desc(s2, K, v2_s)
        v2 = v2_s[...]
        for a in range(K):
            cand_s[a * K:(a + 1) * K, :] = v1_s[a:a + 1, :] + v2
        _top_desc(cand_s[...], K, val_s)
        vals = val_s[...]
        z = jnp.sum(jnp.exp(vals - vals[0:1, :]), axis=0, keepdims=True)
        e1_ref[h] = jnp.exp(s1 - v1_s[0:1, :]) / z
        e2_ref[h] = jnp.exp(s2 - v2_s[0:1, :])
        tau_ref[h] = vals[K - 1:K, :]
        return carry

    lax.fori_loop(0, PEER_HEADS, head, 0)


def _route(x, g, wq, k1, k2):
    R = x.shape[0]
    tb = 128
    H, NK, K = PEER_HEADS, PEER_KEYS, PEER_TOPK
    tmap = pl.BlockSpec((H, NK, tb), lambda i: (0, 0, i))
    tshape = jax.ShapeDtypeStruct((H, NK, R), F32)
    return pl.pallas_call(
        _route_kernel,
        grid=(R // tb,),
        in_specs=[pl.BlockSpec((tb, D_MODEL), lambda i: (i, 0)),
                  pl.BlockSpec((1, D_MODEL), lambda i: (0, 0)),
                  pl.BlockSpec((D_MODEL, H * PEER_QDIM), lambda i: (0, 0)),
                  pl.BlockSpec((NK, PEER_QDIM // 2), lambda i: (0, 0)),
                  pl.BlockSpec((NK, PEER_QDIM // 2), lambda i: (0, 0))],
        out_specs=[pl.BlockSpec((tb, D_MODEL), lambda i: (i, 0)), tmap, tmap, tmap, tmap,
                   pl.BlockSpec((H, 1, tb), lambda i: (0, 0, i))],
        out_shape=[jax.ShapeDtypeStruct((R, D_MODEL), BF16), tshape, tshape, tshape, tshape,
                   jax.ShapeDtypeStruct((H, 1, R), F32)],
        scratch_shapes=[pltpu.VMEM((K, tb), F32), pltpu.VMEM((K, tb), F32),
                        pltpu.VMEM((K * K, tb), F32), pltpu.VMEM((K, tb), F32)],
        compiler_params=_cparams("parallel"),
        name="peer_route",
    )(x, g, wq, k1, k2)


PEER_SUB = 8
PEER_EB = PEER_SUB * PEER_KEYS
PEER_TC = 128


def _expert_kernel(x_ref, xn_ref, s1_ref, s2_ref, e1_ref, e2_ref, tau_ref, eu_ref, evt_ref, out_ref, acc_ref,
                   act_s, p_s, *, tb):
    e = pl.program_id(1)
    ne = pl.num_programs(1)
    NK = PEER_KEYS

    @pl.when(e == 0)
    def _():
        acc_ref[...] = jnp.zeros(acc_ref.shape, F32)

    act_s[...] = _gelu(_dot_nt(eu_ref[...], xn_ref[...]))
    for sub in range(PEER_SUB):
        rows = slice(sub * NK, (sub + 1) * NK)
        for tc in range(tb // PEER_TC):
            lanes = slice(tc * PEER_TC, (tc + 1) * PEER_TC)
            w = jnp.zeros((NK, PEER_TC), F32)
            for h in range(PEER_HEADS):
                s = s1_ref[h, sub:sub + 1, lanes] + s2_ref[h, :, lanes]
                sel = s >= tau_ref[h, :, lanes]
                w = w + jnp.where(sel, e1_ref[h, sub:sub + 1, lanes] * e2_ref[h, :, lanes], 0.0)
            p_s[rows, lanes] = (w * act_s[rows, lanes]).astype(BF16)
    acc_ref[...] += _dot(evt_ref[...], p_s[...])

    @pl.when(e == ne - 1)
    def _():
        out_ref[...] = x_ref[...] + acc_ref[...].T


def _experts(x, xn, s1, s2, e1, e2, tau, e_u, e_vt):
    R = x.shape[0]
    tb = 512
    H, NK = PEER_HEADS, PEER_KEYS
    ne = e_u.shape[0] // PEER_EB
    tmap = pl.BlockSpec((H, NK, tb), lambda i, e: (0, 0, i))
    kmap = pl.BlockSpec((H, PEER_SUB, tb), lambda i, e: (0, e, i))
    kern = functools.partial(_expert_kernel, tb=tb)
    return pl.pallas_call(
        kern,
        grid=(R // tb, ne),
        in_specs=[pl.BlockSpec((tb, D_MODEL), lambda i, e: (i, 0)),
                  pl.BlockSpec((tb, D_MODEL), lambda i, e: (i, 0)),
                  kmap, tmap, kmap, tmap,
                  pl.BlockSpec((H, 1, tb), lambda i, e: (0, 0, i)),
                  pl.BlockSpec((PEER_EB, D_MODEL), lambda i, e: (e, 0)),
                  pl.BlockSpec((D_MODEL, PEER_EB), lambda i, e: (0, e))],
        out_specs=pl.BlockSpec((tb, D_MODEL), lambda i, e: (i, 0)),
        out_shape=jax.ShapeDtypeStruct((R, D_MODEL), F32),
        scratch_shapes=[pltpu.VMEM((D_MODEL, tb), F32), pltpu.VMEM((PEER_EB, tb), F32),
                        pltpu.VMEM((PEER_EB, tb), BF16)],
        compiler_params=_cparams("parallel", "arbitrary"),
        name="peer_experts",
    )(x, xn, s1, s2, e1, e2, tau, e_u, e_vt)


def _final_norm_kernel(x_ref, g_ref, o_ref):
    o_ref[...] = _rms(x_ref[...], g_ref[...])


def _final_norm(x, g):
    R = x.shape[0]
    tb = 512
    return pl.pallas_call(
        _final_norm_kernel,
        grid=(R // tb,),
        in_specs=[pl.BlockSpec((tb, D_MODEL), lambda i: (i, 0)), pl.BlockSpec((1, D_MODEL), lambda i: (0, 0))],
        out_specs=pl.BlockSpec((tb, D_MODEL), lambda i: (i, 0)),
        out_shape=jax.ShapeDtypeStruct((R, D_MODEL), F32),
        compiler_params=_cparams("parallel"),
        name="final_norm",
    )(x, g)


def _s5_params(a_re, a_im, b_re, b_im, c_re, c_im, log_dt):
    dt = jnp.exp(log_dt)[:, None]
    mag = jnp.exp(dt * a_re)
    abar_re = mag * jnp.cos(dt * a_im)
    abar_im = mag * jnp.sin(dt * a_im)
    nr, ni = abar_re - 1.0, abar_im
    den = a_re * a_re + a_im * a_im
    coef_re = (nr * a_re + ni * a_im) / den
    coef_im = (ni * a_re - nr * a_im) / den
    bbar_re = coef_re[..., None] * b_re - coef_im[..., None] * b_im
    bbar_im = coef_re[..., None] * b_im + coef_im[..., None] * b_re
    eye = jnp.eye(SSM_GROUPS, dtype=F32)
    expand_b = lambda b: jnp.einsum('gpc,gh->gchp', b, eye).reshape(SSM_WIDTH, SSM_CH)
    expand_c = lambda c: jnp.einsum('gcp,gh->gphc', c, eye).reshape(SSM_CH, SSM_WIDTH)
    bmat = jnp.concatenate([expand_b(bbar_re), expand_b(bbar_im)], axis=1).astype(BF16)
    cmat = jnp.stack([expand_c(c_re), -expand_c(c_im)]).astype(BF16)
    abar = jnp.stack([abar_re.reshape(SSM_CH), abar_im.reshape(SSM_CH)])
    return bmat, abar, cmat


def _swap_halves(w):
    half = ROPE_DIM // 2
    return jnp.concatenate([w[..., half:], w[..., :half]], axis=-1)


def _rope_table(pos):
    half = ROPE_DIM // 2
    inv = 1.0 / (ROPE_THETA ** (jnp.arange(half, dtype=F32) / half))
    ang = pos[:, None] * inv[None, :]
    cos, sin = jnp.cos(ang), jnp.sin(ang)
    return jnp.concatenate([cos, cos, -sin, sin], axis=1)


def kernel(x_prompt, x_sample, cache_ckv, cache_kpe, state_conv, state_ssm_re, state_ssm_im, meta_tokens, norm_mix, norm_ffn, w_in, b_gate, conv_w, conv_b, w_a_out, q_norm, w_uq, kv_norm, w_ukv, w_b_out, ssm_a_re, ssm_a_im, ssm_b_re, ssm_b_im, ssm_c_re, ssm_c_im, ssm_log_dt, ssm_d, w_glu, w_o, peer_wq, peer_k1, peer_k2, peer_u, peer_v, norm_final):
    B, seq, _ = x_prompt.shape
    assert B == 1
    nb, dec_seq, _ = x_sample.shape
    ns = nb * dec_seq
    past = cache_ckv.shape[2]
    depth = w_in.shape[0]
    p_end = FRONT + seq
    R = -(-(p_end + ns) // ROW_ALIGN) * ROW_ALIGN
    s0 = R - ns
    lay = dict(ns=ns, dec_seq=dec_seq, p_end=p_end, s0=s0)

    x = jnp.concatenate([
        jnp.zeros((FRONT - N_META, D_MODEL), F32), meta_tokens, x_prompt[0],
        jnp.zeros((s0 - p_end, D_MODEL), F32), x_sample.reshape(ns, D_MODEL)], axis=0)
    pos = jnp.concatenate([
        jnp.arange(s0, dtype=F32) - (FRONT - N_META),
        jnp.tile(past + jnp.arange(dec_seq, dtype=F32), nb)])
    cs = _rope_table(pos)

    outs = [[] for _ in range(10)]
    for l in range(depth):
        wl = w_in[l]
        kp = wl[:, 2560:2624]
        w_small = jnp.concatenate([wl[:, 0:2560], wl[:, 2624:3136], kp, _swap_halves(kp)], axis=1).astype(BF16)
        w_gate = wl[:, 3136:].astype(BF16)
        wq3 = w_uq[l].reshape(Q_LORA, MLA_HEADS, QK_DIM)
        wq_aug = jnp.concatenate([wq3, _swap_halves(wq3[..., NOPE_DIM:])], axis=-1).reshape(Q_LORA, MLA_HEADS * 256)
        wq_aug = wq_aug.astype(BF16)
        wkv = w_ukv[l].astype(BF16)
        bmat, abar, cmat = _s5_params(ssm_a_re[l], ssm_a_im[l], ssm_b_re[l], ssm_b_im[l], ssm_c_re[l],
                                      ssm_c_im[l], ssm_log_dt[l])
        h0 = jnp.stack([state_ssm_re[l].reshape(nb, SSM_CH), state_ssm_im[l].reshape(nb, SSM_CH)])

        proj = _in_proj(x, norm_mix[l][None], w_small)
        za, p_conv, s_conv = _conv(proj, conv_w[l], conv_b[l][None], state_conv[l].reshape(2 * nb, A_WIDTH), lay)
        q, k, v, lat, kpe = _qkv(proj, cs, q_norm[l][None], wq_aug, kv_norm[l][None], wkv)
        o = _flash(q, k, v)
        lat_all = jnp.concatenate([cache_ckv[l], lat[s0:].reshape(nb, dec_seq, KV_LORA)], axis=1).astype(BF16)
        kpe_all = jnp.concatenate([cache_kpe[l], kpe[s0:].reshape(nb, dec_seq, ROPE_DIM)], axis=1).astype(BF16)
        o_s = _cached_attn(q, lat_all, kpe_all, wkv, lay)
        o = lax.dynamic_update_slice(o, o_s, (s0, 0))
        yc, p_state, s_state = _s5(proj, bmat, abar, cmat, ssm_d[l][None], h0, lay)
        x = _merge(x, norm_mix[l][None], za, o, yc, w_gate, b_gate[l][None], w_a_out[l].astype(BF16),
                   w_b_out[l].astype(BF16), w_glu[l].astype(BF16), w_o[l].astype(BF16))
        xn2, s1, s2, e1, e2, tau = _route(x, norm_ffn[l][None], peer_wq[l].astype(BF16),
                                          peer_k1[l].astype(BF16), peer_k2[l].astype(BF16))
        x = _experts(x, xn2, s1, s2, e1, e2, tau, peer_u[l].astype(BF16), peer_v[l].astype(BF16).T)

        lo = FRONT - N_META
        outs[0].append(lat[lo:p_end][None])
        outs[1].append(kpe[lo:p_end][None])
        outs[2].append(p_conv[None])
        outs[3].append(p_state[0].reshape(1, SSM_GROUPS, SSM_STATE))
        outs[4].append(p_state[1].reshape(1, SSM_GROUPS, SSM_STATE))
        outs[5].append(lat[s0:].reshape(nb, dec_seq, KV_LORA))
        outs[6].append(kpe[s0:].reshape(nb, dec_seq, ROPE_DIM))
        outs[7].append(s_conv.reshape(nb, CONV_WIDTH - 1, A_WIDTH))
        outs[8].append(s_state[0].reshape(nb, SSM_GROUPS, SSM_STATE))
        outs[9].append(s_state[1].reshape(nb, SSM_GROUPS, SSM_STATE))

    y = _final_norm(x, norm_final[None])
    y_prompt = y[FRONT:p_end][None]
    y_sample = y[s0:].reshape(nb, dec_seq, D_MODEL)
    return (y_prompt, y_sample) + tuple(jnp.stack(o) for o in outs)
```

```python
import functools
import math

import jax
import jax.numpy as jnp
from jax import lax
from jax.experimental import pallas as pl
from jax.experimental.pallas import tpu as pltpu

F32 = jnp.float32
BF16 = jnp.bfloat16

D_MODEL = 2048
CHUNK = 64
N_META = 16
EPS = 1e-6
NEG_INF = -1e30
A_WIDTH = 512
CONV_WIDTH = 3
MLA_HEADS = 8
Q_LORA = 512
KV_LORA = 512
NOPE_DIM = 128
ROPE_DIM = 64
V_DIM = 128
QK_DIM = NOPE_DIM + ROPE_DIM
ROPE_THETA = 10000.0
ATTN_SCALE = 1.0 / math.sqrt(NOPE_DIM + ROPE_DIM)
SSM_GROUP = 16
SSM_GROUPS = 32
SSM_WIDTH = SSM_GROUP * SSM_GROUPS
SSM_STATE = 64
SSM_CH = SSM_GROUPS * SSM_STATE
PEER_HEADS = 8
PEER_KEYS = 128
PEER_QDIM = 256
PEER_TOPK = 16

FRONT = CHUNK
ROW_ALIGN = 512
VMEM_LIMIT = 60 * 1024 * 1024

C_AB, C_AC, C_AH, C_Q, C_KV, C_U, C_KPE, N_SMALL = 0, 512, 1024, 1536, 2048, 2560, 3072, 3200


def _cparams(*sem):
    return pltpu.CompilerParams(dimension_semantics=sem, vmem_limit_bytes=VMEM_LIMIT)


def _rms(x, g):
    return x * lax.rsqrt(jnp.mean(x * x, axis=-1, keepdims=True) + EPS) * g


def _sigmoid(x):
    return 1.0 / (1.0 + jnp.exp(-x))


def _gelu(x):
    return 0.5 * x * (1.0 + jnp.tanh(math.sqrt(2.0 / math.pi) * (x + 0.044715 * (x * x * x))))


def _dot(a, b):
    return jnp.dot(a, b, preferred_element_type=F32)


def _dot_nt(a, b):
    return lax.dot_general(a, b, (((1,), (1,)), ((), ())), preferred_element_type=F32)


def _in_proj_kernel(x_ref, g_ref, w_ref, o_ref):
    xn = _rms(x_ref[...], g_ref[...]).astype(BF16)
    o_ref[...] = _dot(xn, w_ref[...])


def _in_proj(x, g, w_small):
    R = x.shape[0]
    tb = 256
    return pl.pallas_call(
        _in_proj_kernel,
        grid=(R // tb,),
        in_specs=[pl.BlockSpec((tb, D_MODEL), lambda i: (i, 0)),
                  pl.BlockSpec((1, D_MODEL), lambda i: (0, 0)),
                  pl.BlockSpec((D_MODEL, N_SMALL), lambda i: (0, 0))],
        out_specs=pl.BlockSpec((tb, N_SMALL), lambda i: (i, 0)),
        out_shape=jax.ShapeDtypeStruct((R, N_SMALL), F32),
        compiler_params=_cparams("parallel"),
        name="in_proj",
    )(x, g, w_small)


def _conv_kernel(p_ref, w_ref, b_ref, st_ref, za_ref, pc_ref, sc_ref, zbuf, *, p_blk, p_off, n_blk, tb, dec_seq):
    i = pl.program_id(0)
    ab = p_ref[:, C_AB:C_AB + A_WIDTH]
    z = p_ref[:, C_AC:C_AC + A_WIDTH] * p_ref[:, C_AH:C_AH + A_WIDTH]
    w0, w1, w2 = w_ref[0:1, :], w_ref[1:2, :], w_ref[2:3, :]
    b = b_ref[...]

    @pl.when(i == 0)
    def _():
        zbuf[0:8, :] = jnp.zeros((8, A_WIDTH), F32)

    @pl.when(i < n_blk - 1)
    def _():
        zbuf[8:8 + tb, :] = z
        y = b + w0 * zbuf[6:6 + tb, :] + w1 * zbuf[7:7 + tb, :] + w2 * zbuf[8:8 + tb, :]
        za_ref[...] = (ab * y).astype(BF16)
        zbuf[0:8, :] = zbuf[tb:tb + 8, :]

    @pl.when(i == p_blk)
    def _():
        pc_ref[...] = z[p_off - 1:p_off + 1, :]

    @pl.when(i == n_blk - 1)
    def _():
        for s in range(tb // dec_seq):
            r0 = s * dec_seq
            zbuf[6:8, :] = st_ref[2 * s:2 * s + 2, :]
            zbuf[8:8 + dec_seq, :] = z[r0:r0 + dec_seq, :]
            y = (b + w0 * zbuf[6:6 + dec_seq, :] + w1 * zbuf[7:7 + dec_seq, :]
                 + w2 * zbuf[8:8 + dec_seq, :])
            za_ref[r0:r0 + dec_seq, :] = (ab[r0:r0 + dec_seq, :] * y).astype(BF16)
            sc_ref[2 * s:2 * s + 2, :] = z[r0 + dec_seq - 2:r0 + dec_seq, :]


def _conv(proj, conv_w, conv_b, state, lay):
    R = proj.shape[0]
    tb = lay["ns"]
    n_blk = R // tb
    last = lay["p_end"] - 1
    kern = functools.partial(_conv_kernel, p_blk=last // tb, p_off=last % tb, n_blk=n_blk, tb=tb,
                             dec_seq=lay["dec_seq"])
    assert last % tb >= 1
    nseq = tb // lay["dec_seq"]
    return pl.pallas_call(
        kern,
        grid=(n_blk,),
        in_specs=[pl.BlockSpec((tb, 3 * A_WIDTH), lambda i: (i, 0)),
                  pl.BlockSpec((CONV_WIDTH, A_WIDTH), lambda i: (0, 0)),
                  pl.BlockSpec((1, A_WIDTH), lambda i: (0, 0)),
                  pl.BlockSpec((2 * nseq, A_WIDTH), lambda i: (0, 0))],
        out_specs=[pl.BlockSpec((tb, A_WIDTH), lambda i: (i, 0)),
                   pl.BlockSpec((2, A_WIDTH), lambda i: (0, 0)),
                   pl.BlockSpec((2 * nseq, A_WIDTH), lambda i: (0, 0))],
        out_shape=[jax.ShapeDtypeStruct((R, A_WIDTH), BF16),
                   jax.ShapeDtypeStruct((2, A_WIDTH), F32),
                   jax.ShapeDtypeStruct((2 * nseq, A_WIDTH), F32)],
        scratch_shapes=[pltpu.VMEM((tb + 8, A_WIDTH), F32)],
        compiler_params=_cparams("arbitrary"),
        name="short_conv",
    )(proj, conv_w, conv_b, state)


QK_PAD = 256
Q_SCALE = ATTN_SCALE * math.log2(math.e)


V_ROWS = V_DIM + 16


def _qkv_kernel(cq_ref, ckv_ref, kp_ref, cs_ref, qn_ref, wq_ref, kn_ref, wk_ref, wvt_ref,
                q_ref, k_ref, vt_ref, lat_ref, kpe_ref, *, tb):
    cs = cs_ref[...]
    qc = _rms(cq_ref[...], qn_ref[...]).astype(BF16)
    q = _dot(qc, wq_ref[...])
    lat = _rms(ckv_ref[...], kn_ref[...])
    lat_ref[...] = lat
    lat_b = lat.astype(BF16)
    kn = _dot(lat_b, wk_ref[...])
    ones_row = jnp.where(lax.broadcasted_iota(jnp.int32, (V_ROWS - V_DIM, tb), 0) == 0, 1.0, 0.0).astype(BF16)
    lane = lax.broadcasted_iota(jnp.int32, (tb, 2 * ROPE_DIM), 1)
    row = pl.program_id(0) * tb + lax.broadcasted_iota(jnp.int32, (tb, 2 * ROPE_DIM), 0)
    q_tail = jnp.where(lane == ROPE_DIM, 1.0, 0.0)
    k_tail = jnp.where((lane == ROPE_DIM) & (row < FRONT - N_META), NEG_INF, 0.0)
    t = kp_ref[...] * cs
    kpe = t + pltpu.roll(t, ROPE_DIM, 1)
    kpe_ref[...] = kpe[:, 0:ROPE_DIM]
    k_hi = jnp.where(lane < ROPE_DIM, kpe, k_tail).astype(BF16)
    for h in range(MLA_HEADS):
        c = h * 256
        q_ref[h, :, 0:NOPE_DIM] = (q[:, c:c + NOPE_DIM] * Q_SCALE).astype(BF16)
        t = q[:, c + NOPE_DIM:c + 256] * cs
        qpe = (t + pltpu.roll(t, ROPE_DIM, 1)) * Q_SCALE
        q_ref[h, :, NOPE_DIM:QK_PAD] = jnp.where(lane < ROPE_DIM, qpe, q_tail).astype(BF16)
        k_ref[h, :, 0:NOPE_DIM] = kn[:, h * NOPE_DIM:(h + 1) * NOPE_DIM].astype(BF16)
        k_ref[h, :, NOPE_DIM:QK_PAD] = k_hi
        vt_ref[h, 0:V_DIM, :] = _dot_nt(wvt_ref[h], lat_b).astype(BF16)
        vt_ref[h, V_DIM:V_ROWS, :] = ones_row


def _qkv(proj, cs, q_norm, wq_aug, kv_norm, w_k, w_vt):
    R = proj.shape[0]
    tb = 256
    H = MLA_HEADS
    return pl.pallas_call(
        functools.partial(_qkv_kernel, tb=tb),
        grid=(R // tb,),
        in_specs=[pl.BlockSpec((tb, Q_LORA), lambda i: (i, C_Q // Q_LORA)),
                  pl.BlockSpec((tb, KV_LORA), lambda i: (i, C_KV // KV_LORA)),
                  pl.BlockSpec((tb, 2 * ROPE_DIM), lambda i: (i, C_KPE // (2 * ROPE_DIM))),
                  pl.BlockSpec((tb, 2 * ROPE_DIM), lambda i: (i, 0)),
                  pl.BlockSpec((1, Q_LORA), lambda i: (0, 0)),
                  pl.BlockSpec((Q_LORA, H * 256), lambda i: (0, 0)),
                  pl.BlockSpec((1, KV_LORA), lambda i: (0, 0)),
                  pl.BlockSpec((KV_LORA, H * NOPE_DIM), lambda i: (0, 0)),
                  pl.BlockSpec((H, V_DIM, KV_LORA), lambda i: (0, 0, 0))],
        out_specs=[pl.BlockSpec((H, tb, QK_PAD), lambda i: (0, i, 0)),
                   pl.BlockSpec((H, tb, QK_PAD), lambda i: (0, i, 0)),
                   pl.BlockSpec((H, V_ROWS, tb), lambda i: (0, 0, i)),
                   pl.BlockSpec((tb, KV_LORA), lambda i: (i, 0)),
                   pl.BlockSpec((tb, ROPE_DIM), lambda i: (i, 0))],
        out_shape=[jax.ShapeDtypeStruct((H, R, QK_PAD), BF16),
                   jax.ShapeDtypeStruct((H, R, QK_PAD), BF16),
                   jax.ShapeDtypeStruct((H, V_ROWS, R), BF16),
                   jax.ShapeDtypeStruct((R, KV_LORA), F32),
                   jax.ShapeDtypeStruct((R, ROPE_DIM), F32)],
        compiler_params=_cparams("parallel"),
        name="qkv_rope",
    )(proj, proj, proj, cs, q_norm, wq_aug, kv_norm, w_k, w_vt)


FLASH_TQ = 512
FLASH_TK = 256


def _flash_kernel(it_ref, jt_ref, q_ref, k_ref, vt_ref, o_ref, m_ref, acc_ref, *, bq):
    tq, tk = FLASH_TQ, FLASH_TK
    t = pl.program_id(1)
    i = it_ref[t]
    j = jt_ref[t]

    @pl.when(j == 0)
    def _():
        m_ref[...] = jnp.full(m_ref.shape, NEG_INF, F32)
        acc_ref[...] = jnp.zeros(acc_ref.shape, F32)

    def run(diag):
        tiles = [(a, b) for a in range(bq // tq) for b in range(bq // tk)
                 if not (diag and (b * tk) // CHUNK > (a * tq + tq - 1) // CHUNK)]

        def scores(a, b):
            return _dot_nt(k_ref[b * tk:(b + 1) * tk, :], q_ref[a * tq:(a + 1) * tq, :])

        s_next = scores(*tiles[0])
        for n, (a, b) in enumerate(tiles):
            cols = slice(a * tq, (a + 1) * tq)
            s = s_next
            if n + 1 < len(tiles):
                s_next = scores(*tiles[n + 1])
            if n == 0 or tiles[n - 1][0] != a:
                m, acc = m_ref[:, cols], acc_ref[:, cols]
            if diag and (b * tk + tk - 1) // CHUNK > (a * tq) // CHUNK:
                kc = (b * tk + lax.broadcasted_iota(jnp.int32, (tk, tq), 0)) // CHUNK
                qc = (a * tq + lax.broadcasted_iota(jnp.int32, (tk, tq), 1)) // CHUNK
                s = jnp.where(kc <= qc, s, NEG_INF)
            m_new = jnp.maximum(m, jnp.max(s, axis=0, keepdims=True))
            alpha = jnp.exp2(m - m_new)
            p = jnp.exp2(s - m_new).astype(BF16)
            acc = alpha * acc + _dot(vt_ref[:, b * tk:(b + 1) * tk], p)
            m = m_new
            if n + 1 == len(tiles) or tiles[n + 1][0] != a:
                if diag:
                    row = i * bq + a * tq + lax.broadcasted_iota(jnp.int32, (V_DIM, tq), 1)
                    o = jnp.where(row >= FRONT - N_META, acc[0:V_DIM, :] / acc[V_DIM:V_DIM + 1, :], 0.0)
                    o_ref[cols, :] = o.T.astype(o_ref.dtype)
                else:
                    m_ref[:, cols], acc_ref[:, cols] = m, acc

    @pl.when(j < i)
    def _():
        run(False)

    @pl.when(j == i)
    def _():
        run(True)


def _flash(q, k, vt):
    H, R, _ = q.shape
    bq = next(b for b in (1536, 1024, 512) if R % b == 0)
    n = R // bq
    pairs = [(i, j) for i in range(n) for j in range(i + 1)]
    it = jnp.array([p[0] for p in pairs], jnp.int32)
    jt = jnp.array([p[1] for p in pairs], jnp.int32)
    grid_spec = pltpu.PrefetchScalarGridSpec(
        num_scalar_prefetch=2,
        grid=(H, len(pairs)),
        in_specs=[pl.BlockSpec((None, bq, QK_PAD), lambda h, t, it, jt: (h, it[t], 0)),
                  pl.BlockSpec((None, bq, QK_PAD), lambda h, t, it, jt: (h, jt[t], 0)),
                  pl.BlockSpec((None, V_ROWS, bq), lambda h, t, it, jt: (h, 0, jt[t]))],
        out_specs=pl.BlockSpec((bq, V_DIM), lambda h, t, it, jt: (it[t], h)),
        scratch_shapes=[pltpu.VMEM((1, bq), F32), pltpu.VMEM((V_ROWS, bq), F32)],
    )
    return pl.pallas_call(
        functools.partial(_flash_kernel, bq=bq),
        grid_spec=grid_spec,
        out_shape=jax.ShapeDtypeStruct((R, H * V_DIM), BF16),
        compiler_params=_cparams("parallel", "arbitrary"),
        name="prompt_attention",
    )(it, jt, q, k, vt)


def _cached_attn_kernel(q_ref, lat_ref, kpe_ref, w_ref, o_ref):
    kv = _dot(lat_ref[...], w_ref[...]).astype(BF16)
    kpe = kpe_ref[...]
    for h in range(MLA_HEADS):
        c = h * 256
        qh = q_ref[h]
        s = _dot_nt(qh[:, 0:NOPE_DIM], kv[:, c:c + NOPE_DIM]) + _dot_nt(qh[:, NOPE_DIM:QK_DIM], kpe)
        m = jnp.max(s, axis=1, keepdims=True)
        p = jnp.exp2(s - m)
        p = p / jnp.sum(p, axis=1, keepdims=True)
        o_ref[:, h * V_DIM:(h + 1) * V_DIM] = _dot(p.astype(BF16), kv[:, c + NOPE_DIM:c + 256]).astype(o_ref.dtype)


def _cached_attn(q, lat_all, kpe_all, w_ukv, lay):
    H = MLA_HEADS
    B, Lk, _ = lat_all.shape
    S = lay["dec_seq"]
    blk0 = lay["s0"] // S
    return pl.pallas_call(
        _cached_attn_kernel,
        grid=(B,),
        in_specs=[pl.BlockSpec((H, S, QK_PAD), lambda b: (0, blk0 + b, 0)),
                  pl.BlockSpec((None, Lk, KV_LORA), lambda b: (b, 0, 0)),
                  pl.BlockSpec((None, Lk, ROPE_DIM), lambda b: (b, 0, 0)),
                  pl.BlockSpec((KV_LORA, H * 256), lambda b: (0, 0))],
        out_specs=pl.BlockSpec((S, H * V_DIM), lambda b: (b, 0)),
        out_shape=jax.ShapeDtypeStruct((B * S, H * V_DIM), BF16),
        compiler_params=_cparams("parallel"),
        name="sample_attention",
    )(q, lat_all, kpe_all, w_ukv)


def _s5_kernel(u_ref, bm_ref, a_ref, cm_ref, d_ref, h0_ref, y_ref, ps_ref, ss_ref, hre, him, carry,
               *, p_blk, p_off, n_blk, tb, dec_seq):
    i = pl.program_id(0)
    u = u_ref[...]
    hb = _dot(u.astype(BF16), bm_ref[...])
    hre[...] = hb[:, 0:SSM_CH]
    him[...] = hb[:, SSM_CH:2 * SSM_CH]
    ar = a_ref[0:1, :]
    ai = a_ref[1:2, :]

    def scan(start, n, hr, hi):
        def body(t, c):
            hr, hi = c
            r = start + t
            nr = ar * hr - ai * hi + hre[pl.ds(r, 1), :]
            ni = ar * hi + ai * hr + him[pl.ds(r, 1), :]
            hre[pl.ds(r, 1), :] = nr
            him[pl.ds(r, 1), :] = ni
            return nr, ni
        return lax.fori_loop(0, n, body, (hr, hi), unroll=2)

    @pl.when(i == 0)
    def _():
        carry[...] = jnp.zeros(carry.shape, F32)

    @pl.when(i < n_blk - 1)
    def _():
        hr, hi = scan(0, tb, carry[0:1, :], carry[1:2, :])
        carry[0:1, :] = hr
        carry[1:2, :] = hi

    @pl.when(i == p_blk)
    def _():
        ps_ref[0:1, :] = hre[p_off:p_off + 1, :]
        ps_ref[1:2, :] = him[p_off:p_off + 1, :]

    @pl.when(i == n_blk - 1)
    def _():
        for s in range(tb // dec_seq):
            hr, hi = scan(s * dec_seq, dec_seq, h0_ref[0, s:s + 1, :], h0_ref[1, s:s + 1, :])
            ss_ref[0, s:s + 1, :] = hr
            ss_ref[1, s:s + 1, :] = hi

    y = _dot(hre[...].astype(BF16), cm_ref[0]) + _dot(him[...].astype(BF16), cm_ref[1])
    y_ref[...] = _gelu(y + d_ref[...] * u).astype(BF16)


def _s5(proj, bmat, abar, cmat, d, h0, lay):
    R = proj.shape[0]
    tb = lay["ns"]
    n_blk = R // tb
    last = lay["p_end"] - 1
    nseq = tb // lay["dec_seq"]
    kern = functools.partial(_s5_kernel, p_blk=last // tb, p_off=last % tb, n_blk=n_blk, tb=tb,
                             dec_seq=lay["dec_seq"])
    return pl.pallas_call(
        kern,
        grid=(n_blk,),
        in_specs=[pl.BlockSpec((tb, SSM_WIDTH), lambda i: (i, C_U // SSM_WIDTH)),
                  pl.BlockSpec((SSM_WIDTH, 2 * SSM_CH), lambda i: (0, 0)),
                  pl.BlockSpec((2, SSM_CH), lambda i: (0, 0)),
                  pl.BlockSpec((2, SSM_CH, SSM_WIDTH), lambda i: (0, 0, 0)),
                  pl.BlockSpec((1, SSM_WIDTH), lambda i: (0, 0)),
                  pl.BlockSpec((2, nseq, SSM_CH), lambda i: (0, 0, 0))],
        out_specs=[pl.BlockSpec((tb, SSM_WIDTH), lambda i: (i, 0)),
                   pl.BlockSpec((2, SSM_CH), lambda i: (0, 0)),
                   pl.BlockSpec((2, nseq, SSM_CH), lambda i: (0, 0, 0))],
        out_shape=[jax.ShapeDtypeStruct((R, SSM_WIDTH), BF16),
                   jax.ShapeDtypeStruct((2, SSM_CH), F32),
                   jax.ShapeDtypeStruct((2, nseq, SSM_CH), F32)],
        scratch_shapes=[pltpu.VMEM((tb, SSM_CH), F32), pltpu.VMEM((tb, SSM_CH), F32),
                        pltpu.VMEM((2, SSM_CH), F32)],
        compiler_params=_cparams("arbitrary"),
        name="s5_scan",
    )(proj, bmat, abar, cmat, d, h0)


MERGE_TN = 256
MERGE_NC = D_MODEL // MERGE_TN


def _merge_kernel(x_ref, g_ref, za_ref, o_ref, yc_ref, wg0_ref, wg1_ref, wg2_ref, b0_ref, b1_ref, b2_ref,
                  wa_ref, wb_ref, wga_ref, wgb_ref, wo_ref, out_ref, xn_s, mg_s):
    c = pl.program_id(1)

    @pl.when(c == 0)
    def _():
        xn_s[...] = _rms(x_ref[...], g_ref[...]).astype(BF16)

    xn = xn_s[...]
    g0 = _sigmoid(_dot(xn, wg0_ref[...]) + b0_ref[...])
    g1 = _sigmoid(_dot(xn, wg1_ref[...]) + b1_ref[...])
    g2 = _sigmoid(_dot(xn, wg2_ref[...]) + b2_ref[...])
    yc = yc_ref[...]
    y_a = _dot(za_ref[...], wa_ref[...])
    y_b = _dot(o_ref[...], wb_ref[...])
    y_c = _dot(yc, wga_ref[...]) * _sigmoid(_dot(yc, wgb_ref[...]))
    mg_s[c] = (g0 * y_a + g1 * y_b + g2 * y_c).astype(BF16)

    @pl.when(c == MERGE_NC - 1)
    def _():
        acc = x_ref[...]
        for cc in range(MERGE_NC):
            acc = acc + _dot(mg_s[cc], wo_ref[cc * MERGE_TN:(cc + 1) * MERGE_TN, :])
        out_ref[...] = acc


def _merge(x, g, za, o, yc, w_gate, b_gate, w_a, w_b, w_glu, w_o):
    R = x.shape[0]
    tb = 512
    tn, nc = MERGE_TN, MERGE_NC
    row = lambda w: pl.BlockSpec((tb, w), lambda i, c: (i, 0))
    col = lambda k, off: pl.BlockSpec((k, tn), lambda i, c: (0, off + c))
    return pl.pallas_call(
        _merge_kernel,
        grid=(R // tb, nc),
        in_specs=[row(D_MODEL), pl.BlockSpec((1, D_MODEL), lambda i, c: (0, 0)),
                  row(A_WIDTH), row(MLA_HEADS * V_DIM), row(SSM_WIDTH),
                  col(D_MODEL, 0), col(D_MODEL, nc), col(D_MODEL, 2 * nc),
                  col(1, 0), col(1, nc), col(1, 2 * nc),
                  col(A_WIDTH, 0), col(MLA_HEADS * V_DIM, 0), col(SSM_WIDTH, 0), col(SSM_WIDTH, nc),
                  pl.BlockSpec((D_MODEL, D_MODEL), lambda i, c: (0, 0))],
        out_specs=row(D_MODEL),
        out_shape=jax.ShapeDtypeStruct((R, D_MODEL), F32),
        scratch_shapes=[pltpu.VMEM((tb, D_MODEL), BF16), pltpu.VMEM((nc, tb, tn), BF16)],
        compiler_params=_cparams("parallel", "arbitrary"),
        name="branch_merge",
    )(x, g, za, o, yc, w_gate, w_gate, w_gate, b_gate, b_gate, b_gate, w_a, w_b, w_glu, w_glu, w_o)


def _top_desc(cur, n_out, out_ref):
    n = cur.shape[0]
    iota = lax.broadcasted_iota(jnp.int32, cur.shape, 0).astype(F32)
    for r in range(n_out):
        m = jnp.max(cur, axis=0, keepdims=True)
        out_ref[r:r + 1, :] = m
        if r + 1 < n_out:
            first = jnp.min(jnp.where(cur == m, iota, float(n)), axis=0, keepdims=True)
            cur = jnp.where(iota == first, -jnp.inf, cur)


def _route_kernel(x_ref, g_ref, wq_ref, k1_ref, k2_ref, xn_ref, s1_ref, s2_ref, e1_ref, e2_ref, tau_ref,
                  v1_s, v2_s, cand_s, val_s):
    K = PEER_TOPK
    xn = _rms(x_ref[...], g_ref[...]).astype(BF16)
    xn_ref[...] = xn
    q = _dot(xn, wq_ref[...]).astype(BF16)
    half = PEER_QDIM // 2
    for h in range(PEER_HEADS):
        c = h * PEER_QDIM
        s1_ref[h] = _dot_nt(k1_ref[...], q[:, c:c + half])
        s2_ref[h] = _dot_nt(k2_ref[...], q[:, c + half:c + PEER_QDIM])

    def head(h, carry):
        s1 = s1_ref[h]
        s2 = s2_ref[h]
        _top_desc(s1, K, v1_s)
        _top_desc(s2, K, v2_s)
        v2 = v2_s[...]
        for a in range(K):
            cand_s[a * K:(a + 1) * K, :] = v1_s[a:a + 1, :] + v2
        _top_desc(cand_s[...], K, val_s)
        vals = val_s[...]
        z = jnp.sum(jnp.exp(vals - vals[0:1, :]), axis=0, keepdims=True)
        e1_ref[h] = jnp.exp(s1 - v1_s[0:1, :]) / z
        e2_ref[h] = jnp.exp(s2 - v2_s[0:1, :])
        tau_ref[h] = vals[K - 1:K, :]
        return carry

    lax.fori_loop(0, PEER_HEADS, head, 0)


def _route(x, g, wq, k1, k2):
    R = x.shape[0]
    tb = 128
    H, NK, K = PEER_HEADS, PEER_KEYS, PEER_TOPK
    tmap = pl.BlockSpec((H, NK, tb), lambda i: (0, 0, i))
    tshape = jax.ShapeDtypeStruct((H, NK, R), F32)
    return pl.pallas_call(
        _route_kernel,
        grid=(R // tb,),
        in_specs=[pl.BlockSpec((tb, D_MODEL), lambda i: (i, 0)),
                  pl.BlockSpec((1, D_MODEL), lambda i: (0, 0)),
                  pl.BlockSpec((D_MODEL, H * PEER_QDIM), lambda i: (0, 0)),
                  pl.BlockSpec((NK, PEER_QDIM // 2), lambda i: (0, 0)),
                  pl.BlockSpec((NK, PEER_QDIM // 2), lambda i: (0, 0))],
        out_specs=[pl.BlockSpec((tb, D_MODEL), lambda i: (i, 0)), tmap, tmap, tmap, tmap,
                   pl.BlockSpec((H, 1, tb), lambda i: (0, 0, i))],
        out_shape=[jax.ShapeDtypeStruct((R, D_MODEL), BF16), tshape, tshape, tshape, tshape,
                   jax.ShapeDtypeStruct((H, 1, R), F32)],
        scratch_shapes=[pltpu.VMEM((K, tb), F32), pltpu.VMEM((K, tb), F32),
                        pltpu.VMEM((K * K, tb), F32), pltpu.VMEM((K, tb), F32)],
        compiler_params=_cparams("parallel"),
        name="peer_route",
    )(x, g, wq, k1, k2)


PEER_SUB = 8
PEER_EB = PEER_SUB * PEER_KEYS
PEER_TC = 128


def _expert_kernel(x_ref, xn_ref, s1_ref, s2_ref, e1_ref, e2_ref, tau_ref, eu_ref, evt_ref, out_ref, acc_ref,
                   act_s, p_s, *, tb):
    e = pl.program_id(1)
    ne = pl.num_programs(1)
    NK = PEER_KEYS

    @pl.when(e == 0)
    def _():
        acc_ref[...] = jnp.zeros(acc_ref.shape, F32)

    act_s[...] = _gelu(_dot_nt(eu_ref[...], xn_ref[...]))
    for sub in range(PEER_SUB):
        rows = slice(sub * NK, (sub + 1) * NK)
        for tc in range(tb // PEER_TC):
            lanes = slice(tc * PEER_TC, (tc + 1) * PEER_TC)
            w = jnp.zeros((NK, PEER_TC), F32)
            for h in range(PEER_HEADS):
                s = s1_ref[h, sub:sub + 1, lanes] + s2_ref[h, :, lanes]
                sel = s >= tau_ref[h, :, lanes]
                w = w + jnp.where(sel, e1_ref[h, sub:sub + 1, lanes] * e2_ref[h, :, lanes], 0.0)
            p_s[rows, lanes] = (w * act_s[rows, lanes]).astype(BF16)
    acc_ref[...] += _dot(evt_ref[...], p_s[...])

    @pl.when(e == ne - 1)
    def _():
        out_ref[...] = x_ref[...] + acc_ref[...].T


def _experts(x, xn, s1, s2, e1, e2, tau, e_u, e_vt):
    R = x.shape[0]
    tb = 512
    H, NK = PEER_HEADS, PEER_KEYS
    ne = e_u.shape[0] // PEER_EB
    tmap = pl.BlockSpec((H, NK, tb), lambda i, e: (0, 0, i))
    kmap = pl.BlockSpec((H, PEER_SUB, tb), lambda i, e: (0, e, i))
    kern = functools.partial(_expert_kernel, tb=tb)
    return pl.pallas_call(
        kern,
        grid=(R // tb, ne),
        in_specs=[pl.BlockSpec((tb, D_MODEL), lambda i, e: (i, 0)),
                  pl.BlockSpec((tb, D_MODEL), lambda i, e: (i, 0)),
                  kmap, tmap, kmap, tmap,
                  pl.BlockSpec((H, 1, tb), lambda i, e: (0, 0, i)),
                  pl.BlockSpec((PEER_EB, D_MODEL), lambda i, e: (e, 0)),
                  pl.BlockSpec((D_MODEL, PEER_EB), lambda i, e: (0, e))],
        out_specs=pl.BlockSpec((tb, D_MODEL), lambda i, e: (i, 0)),
        out_shape=jax.ShapeDtypeStruct((R, D_MODEL), F32),
        scratch_shapes=[pltpu.VMEM((D_MODEL, tb), F32), pltpu.VMEM((PEER_EB, tb), F32),
                        pltpu.VMEM((PEER_EB, tb), BF16)],
        compiler_params=_cparams("parallel", "arbitrary"),
        name="peer_experts",
    )(x, xn, s1, s2, e1, e2, tau, e_u, e_vt)


def _final_norm_kernel(x_ref, g_ref, o_ref):
    o_ref[...] = _rms(x_ref[...], g_ref[...])


def _final_norm(x, g):
    R = x.shape[0]
    tb = 512
    return pl.pallas_call(
        _final_norm_kernel,
        grid=(R // tb,),
        in_specs=[pl.BlockSpec((tb, D_MODEL), lambda i: (i, 0)), pl.BlockSpec((1, D_MODEL), lambda i: (0, 0))],
        out_specs=pl.BlockSpec((tb, D_MODEL), lambda i: (i, 0)),
        out_shape=jax.ShapeDtypeStruct((R, D_MODEL), F32),
        compiler_params=_cparams("parallel"),
        name="final_norm",
    )(x, g)


def _s5_params(a_re, a_im, b_re, b_im, c_re, c_im, log_dt):
    dt = jnp.exp(log_dt)[:, None]
    mag = jnp.exp(dt * a_re)
    abar_re = mag * jnp.cos(dt * a_im)
    abar_im = mag * jnp.sin(dt * a_im)
    nr, ni = abar_re - 1.0, abar_im
    den = a_re * a_re + a_im * a_im
    coef_re = (nr * a_re + ni * a_im) / den
    coef_im = (ni * a_re - nr * a_im) / den
    bbar_re = coef_re[..., None] * b_re - coef_im[..., None] * b_im
    bbar_im = coef_re[..., None] * b_im + coef_im[..., None] * b_re
    eye = jnp.eye(SSM_GROUPS, dtype=F32)
    expand_b = lambda b: jnp.einsum('gpc,gh->gchp', b, eye).reshape(SSM_WIDTH, SSM_CH)
    expand_c = lambda c: jnp.einsum('gcp,gh->gphc', c, eye).reshape(SSM_CH, SSM_WIDTH)
    bmat = jnp.concatenate([expand_b(bbar_re), expand_b(bbar_im)], axis=1).astype(BF16)
    cmat = jnp.stack([expand_c(c_re), -expand_c(c_im)]).astype(BF16)
    abar = jnp.stack([abar_re.reshape(SSM_CH), abar_im.reshape(SSM_CH)])
    return bmat, abar, cmat


def _swap_halves(w):
    half = ROPE_DIM // 2
    return jnp.concatenate([w[..., half:], w[..., :half]], axis=-1)


def _rope_table(pos):
    half = ROPE_DIM // 2
    inv = 1.0 / (ROPE_THETA ** (jnp.arange(half, dtype=F32) / half))
    ang = pos[:, None] * inv[None, :]
    cos, sin = jnp.cos(ang), jnp.sin(ang)
    return jnp.concatenate([cos, cos, -sin, sin], axis=1)


def kernel(x_prompt, x_sample, cache_ckv, cache_kpe, state_conv, state_ssm_re, state_ssm_im, meta_tokens, norm_mix, norm_ffn, w_in, b_gate, conv_w, conv_b, w_a_out, q_norm, w_uq, kv_norm, w_ukv, w_b_out, ssm_a_re, ssm_a_im, ssm_b_re, ssm_b_im, ssm_c_re, ssm_c_im, ssm_log_dt, ssm_d, w_glu, w_o, peer_wq, peer_k1, peer_k2, peer_u, peer_v, norm_final):
    B, seq, _ = x_prompt.shape
    assert B == 1
    nb, dec_seq, _ = x_sample.shape
    ns = nb * dec_seq
    past = cache_ckv.shape[2]
    depth = w_in.shape[0]
    p_end = FRONT + seq
    R = -(-(p_end + ns) // ROW_ALIGN) * ROW_ALIGN
    s0 = R - ns
    lay = dict(ns=ns, dec_seq=dec_seq, p_end=p_end, s0=s0)

    x = jnp.concatenate([
        jnp.zeros((FRONT - N_META, D_MODEL), F32), meta_tokens, x_prompt[0],
        jnp.zeros((s0 - p_end, D_MODEL), F32), x_sample.reshape(ns, D_MODEL)], axis=0)
    pos = jnp.concatenate([
        jnp.arange(s0, dtype=F32) - (FRONT - N_META),
        jnp.tile(past + jnp.arange(dec_seq, dtype=F32), nb)])
    cs = _rope_table(pos)

    outs = [[] for _ in range(10)]
    for l in range(depth):
        wl = w_in[l]
        kp = wl[:, 2560:2624]
        w_small = jnp.concatenate([wl[:, 0:2560], wl[:, 2624:3136], kp, _swap_halves(kp)], axis=1).astype(BF16)
        w_gate = wl[:, 3136:].astype(BF16)
        wq3 = w_uq[l].reshape(Q_LORA, MLA_HEADS, QK_DIM)
        wq_aug = jnp.concatenate([wq3, _swap_halves(wq3[..., NOPE_DIM:])], axis=-1).reshape(Q_LORA, MLA_HEADS * 256)
        wq_aug = wq_aug.astype(BF16)
        wkv = w_ukv[l].astype(BF16)
        wkv3 = wkv.reshape(KV_LORA, MLA_HEADS, NOPE_DIM + V_DIM)
        w_k = wkv3[..., :NOPE_DIM].reshape(KV_LORA, MLA_HEADS * NOPE_DIM)
        w_vt = wkv3[..., NOPE_DIM:].transpose(1, 2, 0)
        bmat, abar, cmat = _s5_params(ssm_a_re[l], ssm_a_im[l], ssm_b_re[l], ssm_b_im[l], ssm_c_re[l],
                                      ssm_c_im[l], ssm_log_dt[l])
        h0 = jnp.stack([state_ssm_re[l].reshape(nb, SSM_CH), state_ssm_im[l].reshape(nb, SSM_CH)])

        proj = _in_proj(x, norm_mix[l][None], w_small)
        za, p_conv, s_conv = _conv(proj, conv_w[l], conv_b[l][None], state_conv[l].reshape(2 * nb, A_WIDTH), lay)
        q, k, vt, lat, kpe = _qkv(proj, cs, q_norm[l][None], wq_aug, kv_norm[l][None], w_k, w_vt)
        o = _flash(q, k, vt)
        lat_all = jnp.concatenate([cache_ckv[l], lat[s0:].reshape(nb, dec_seq, KV_LORA)], axis=1).astype(BF16)
        kpe_all = jnp.concatenate([cache_kpe[l], kpe[s0:].reshape(nb, dec_seq, ROPE_DIM)], axis=1).astype(BF16)
        o_s = _cached_attn(q, lat_all, kpe_all, wkv, lay)
        o = lax.dynamic_update_slice(o, o_s, (s0, 0))
        yc, p_state, s_state = _s5(proj, bmat, abar, cmat, ssm_d[l][None], h0, lay)
        x = _merge(x, norm_mix[l][None], za, o, yc, w_gate, b_gate[l][None], w_a_out[l].astype(BF16),
                   w_b_out[l].astype(BF16), w_glu[l].astype(BF16), w_o[l].astype(BF16))
        xn2, s1, s2, e1, e2, tau = _route(x, norm_ffn[l][None], peer_wq[l].astype(BF16),
                                          peer_k1[l].astype(BF16), peer_k2[l].astype(BF16))
        x = _experts(x, xn2, s1, s2, e1, e2, tau, peer_u[l].astype(BF16), peer_v[l].astype(BF16).T)

        lo = FRONT - N_META
        outs[0].append(lat[lo:p_end][None])
        outs[1].append(kpe[lo:p_end][None])
        outs[2].append(p_conv[None])
        outs[3].append(p_state[0].reshape(1, SSM_GROUPS, SSM_STATE))
        outs[4].append(p_state[1].reshape(1, SSM_GROUPS, SSM_STATE))
        outs[5].append(lat[s0:].reshape(nb, dec_seq, KV_LORA))
        outs[6].append(kpe[s0:].reshape(nb, dec_seq, ROPE_DIM))
        outs[7].append(s_conv.reshape(nb, CONV_WIDTH - 1, A_WIDTH))
        outs[8].append(s_state[0].reshape(nb, SSM_GROUPS, SSM_STATE))
        outs[9].append(s_state[1].reshape(nb, SSM_GROUPS, SSM_STATE))

    y = _final_norm(x, norm_final[None])
    y_prompt = y[FRONT:p_end][None]
    y_sample = y[s0:].reshape(nb, dec_seq, D_MODEL)
    return (y_prompt, y_sample) + tuple(jnp.stack(o) for o in outs)
```

```python
import functools
import math

import jax
import jax.numpy as jnp
from jax import lax
from jax.experimental import pallas as pl
from jax.experimental.pallas import tpu as pltpu

F32 = jnp.float32
BF16 = jnp.bfloat16

D_MODEL = 2048
CHUNK = 64
N_META = 16
EPS = 1e-6
NEG_INF = -1e30
A_WIDTH = 512
CONV_WIDTH = 3
MLA_HEADS = 8
Q_LORA = 512
KV_LORA = 512
NOPE_DIM = 128
ROPE_DIM = 64
V_DIM = 128
QK_DIM = NOPE_DIM + ROPE_DIM
ROPE_THETA = 10000.0
ATTN_SCALE = 1.0 / math.sqrt(NOPE_DIM + ROPE_DIM)
SSM_GROUP = 16
SSM_GROUPS = 32
SSM_WIDTH = SSM_GROUP * SSM_GROUPS
SSM_STATE = 64
SSM_CH = SSM_GROUPS * SSM_STATE
PEER_HEADS = 8
PEER_KEYS = 128
PEER_QDIM = 256
PEER_TOPK = 16

FRONT = CHUNK
ROW_ALIGN = 512
VMEM_LIMIT = 60 * 1024 * 1024

C_AB, C_AC, C_AH, C_Q, C_KV, C_U, C_KPE, N_SMALL = 0, 512, 1024, 1536, 2048, 2560, 3072, 3200


def _cparams(*sem):
    return pltpu.CompilerParams(dimension_semantics=sem, vmem_limit_bytes=VMEM_LIMIT)


def _rms(x, g):
    return x * lax.rsqrt(jnp.mean(x * x, axis=-1, keepdims=True) + EPS) * g


def _sigmoid(x):
    return 1.0 / (1.0 + jnp.exp(-x))


def _gelu(x):
    return 0.5 * x * (1.0 + jnp.tanh(math.sqrt(2.0 / math.pi) * (x + 0.044715 * (x * x * x))))


def _dot(a, b):
    return jnp.dot(a, b, preferred_element_type=F32)


def _dot_nt(a, b):
    return lax.dot_general(a, b, (((1,), (1,)), ((), ())), preferred_element_type=F32)


def _in_proj_kernel(x_ref, g_ref, w_ref, o_ref):
    xn = _rms(x_ref[...], g_ref[...]).astype(BF16)
    o_ref[...] = _dot(xn, w_ref[...])


def _in_proj(x, g, w_small):
    R = x.shape[0]
    tb = 256
    return pl.pallas_call(
        _in_proj_kernel,
        grid=(R // tb,),
        in_specs=[pl.BlockSpec((tb, D_MODEL), lambda i: (i, 0)),
                  pl.BlockSpec((1, D_MODEL), lambda i: (0, 0)),
                  pl.BlockSpec((D_MODEL, N_SMALL), lambda i: (0, 0))],
        out_specs=pl.BlockSpec((tb, N_SMALL), lambda i: (i, 0)),
        out_shape=jax.ShapeDtypeStruct((R, N_SMALL), F32),
        compiler_params=_cparams("parallel"),
        name="in_proj",
    )(x, g, w_small)


def _conv_kernel(p_ref, w_ref, b_ref, st_ref, za_ref, pc_ref, sc_ref, zbuf, *, p_blk, p_off, n_blk, tb, dec_seq):
    i = pl.program_id(0)
    ab = p_ref[:, C_AB:C_AB + A_WIDTH]
    z = p_ref[:, C_AC:C_AC + A_WIDTH] * p_ref[:, C_AH:C_AH + A_WIDTH]
    w0, w1, w2 = w_ref[0:1, :], w_ref[1:2, :], w_ref[2:3, :]
    b = b_ref[...]

    @pl.when(i == 0)
    def _():
        zbuf[0:8, :] = jnp.zeros((8, A_WIDTH), F32)

    @pl.when(i < n_blk - 1)
    def _():
        zbuf[8:8 + tb, :] = z
        y = b + w0 * zbuf[6:6 + tb, :] + w1 * zbuf[7:7 + tb, :] + w2 * zbuf[8:8 + tb, :]
        za_ref[...] = (ab * y).astype(BF16)
        zbuf[0:8, :] = zbuf[tb:tb + 8, :]

    @pl.when(i == p_blk)
    def _():
        pc_ref[...] = z[p_off - 1:p_off + 1, :]

    @pl.when(i == n_blk - 1)
    def _():
        for s in range(tb // dec_seq):
            r0 = s * dec_seq
            zbuf[6:8, :] = st_ref[2 * s:2 * s + 2, :]
            zbuf[8:8 + dec_seq, :] = z[r0:r0 + dec_seq, :]
            y = (b + w0 * zbuf[6:6 + dec_seq, :] + w1 * zbuf[7:7 + dec_seq, :]
                 + w2 * zbuf[8:8 + dec_seq, :])
            za_ref[r0:r0 + dec_seq, :] = (ab[r0:r0 + dec_seq, :] * y).astype(BF16)
            sc_ref[2 * s:2 * s + 2, :] = z[r0 + dec_seq - 2:r0 + dec_seq, :]


def _conv(proj, conv_w, conv_b, state, lay):
    R = proj.shape[0]
    tb = lay["ns"]
    n_blk = R // tb
    last = lay["p_end"] - 1
    kern = functools.partial(_conv_kernel, p_blk=last // tb, p_off=last % tb, n_blk=n_blk, tb=tb,
                             dec_seq=lay["dec_seq"])
    assert last % tb >= 1
    nseq = tb // lay["dec_seq"]
    return pl.pallas_call(
        kern,
        grid=(n_blk,),
        in_specs=[pl.BlockSpec((tb, 3 * A_WIDTH), lambda i: (i, 0)),
                  pl.BlockSpec((CONV_WIDTH, A_WIDTH), lambda i: (0, 0)),
                  pl.BlockSpec((1, A_WIDTH), lambda i: (0, 0)),
                  pl.BlockSpec((2 * nseq, A_WIDTH), lambda i: (0, 0))],
        out_specs=[pl.BlockSpec((tb, A_WIDTH), lambda i: (i, 0)),
                   pl.BlockSpec((2, A_WIDTH), lambda i: (0, 0)),
                   pl.BlockSpec((2 * nseq, A_WIDTH), lambda i: (0, 0))],
        out_shape=[jax.ShapeDtypeStruct((R, A_WIDTH), BF16),
                   jax.ShapeDtypeStruct((2, A_WIDTH), F32),
                   jax.ShapeDtypeStruct((2 * nseq, A_WIDTH), F32)],
        scratch_shapes=[pltpu.VMEM((tb + 8, A_WIDTH), F32)],
        compiler_params=_cparams("arbitrary"),
        name="short_conv",
    )(proj, conv_w, conv_b, state)


QK_PAD = 256
Q_SCALE = ATTN_SCALE * math.log2(math.e)


V_ROWS = V_DIM + 16


def _qkv_kernel(cq_ref, ckv_ref, kp_ref, cs_ref, qn_ref, wq_ref, kn_ref, wk_ref, wvt_ref,
                q_ref, k_ref, vt_ref, lat_ref, kpe_ref, *, tb):
    cs = cs_ref[...]
    qc = _rms(cq_ref[...], qn_ref[...]).astype(BF16)
    q = _dot(qc, wq_ref[...])
    lat = _rms(ckv_ref[...], kn_ref[...])
    lat_ref[...] = lat
    lat_b = lat.astype(BF16)
    kn = _dot(lat_b, wk_ref[...])
    ones_row = jnp.where(lax.broadcasted_iota(jnp.int32, (V_ROWS - V_DIM, tb), 0) == 0, 1.0, 0.0).astype(BF16)
    lane = lax.broadcasted_iota(jnp.int32, (tb, 2 * ROPE_DIM), 1)
    row = pl.program_id(0) * tb + lax.broadcasted_iota(jnp.int32, (tb, 2 * ROPE_DIM), 0)
    q_tail = jnp.where(lane == ROPE_DIM, 1.0, 0.0)
    k_tail = jnp.where((lane == ROPE_DIM) & (row < FRONT - N_META), NEG_INF, 0.0)
    t = kp_ref[...] * cs
    kpe = t + pltpu.roll(t, ROPE_DIM, 1)
    kpe_ref[...] = kpe[:, 0:ROPE_DIM]
    k_hi = jnp.where(lane < ROPE_DIM, kpe, k_tail).astype(BF16)
    for h in range(MLA_HEADS):
        c = h * 256
        q_ref[h, :, 0:NOPE_DIM] = (q[:, c:c + NOPE_DIM] * Q_SCALE).astype(BF16)
        t = q[:, c + NOPE_DIM:c + 256] * cs
        qpe = (t + pltpu.roll(t, ROPE_DIM, 1)) * Q_SCALE
        q_ref[h, :, NOPE_DIM:QK_PAD] = jnp.where(lane < ROPE_DIM, qpe, q_tail).astype(BF16)
        k_ref[h, :, 0:NOPE_DIM] = kn[:, h * NOPE_DIM:(h + 1) * NOPE_DIM].astype(BF16)
        k_ref[h, :, NOPE_DIM:QK_PAD] = k_hi
        vt_ref[h, 0:V_DIM, :] = _dot_nt(wvt_ref[h], lat_b).astype(BF16)
        vt_ref[h, V_DIM:V_ROWS, :] = ones_row


def _qkv(proj, cs, q_norm, wq_aug, kv_norm, w_k, w_vt):
    R = proj.shape[0]
    tb = 256
    H = MLA_HEADS
    return pl.pallas_call(
        functools.partial(_qkv_kernel, tb=tb),
        grid=(R // tb,),
        in_specs=[pl.BlockSpec((tb, Q_LORA), lambda i: (i, C_Q // Q_LORA)),
                  pl.BlockSpec((tb, KV_LORA), lambda i: (i, C_KV // KV_LORA)),
                  pl.BlockSpec((tb, 2 * ROPE_DIM), lambda i: (i, C_KPE // (2 * ROPE_DIM))),
                  pl.BlockSpec((tb, 2 * ROPE_DIM), lambda i: (i, 0)),
                  pl.BlockSpec((1, Q_LORA), lambda i: (0, 0)),
                  pl.BlockSpec((Q_LORA, H * 256), lambda i: (0, 0)),
                  pl.BlockSpec((1, KV_LORA), lambda i: (0, 0)),
                  pl.BlockSpec((KV_LORA, H * NOPE_DIM), lambda i: (0, 0)),
                  pl.BlockSpec((H, V_DIM, KV_LORA), lambda i: (0, 0, 0))],
        out_specs=[pl.BlockSpec((H, tb, QK_PAD), lambda i: (0, i, 0)),
                   pl.BlockSpec((H, tb, QK_PAD), lambda i: (0, i, 0)),
                   pl.BlockSpec((H, V_ROWS, tb), lambda i: (0, 0, i)),
                   pl.BlockSpec((tb, KV_LORA), lambda i: (i, 0)),
                   pl.BlockSpec((tb, ROPE_DIM), lambda i: (i, 0))],
        out_shape=[jax.ShapeDtypeStruct((H, R, QK_PAD), BF16),
                   jax.ShapeDtypeStruct((H, R, QK_PAD), BF16),
                   jax.ShapeDtypeStruct((H, V_ROWS, R), BF16),
                   jax.ShapeDtypeStruct((R, KV_LORA), F32),
                   jax.ShapeDtypeStruct((R, ROPE_DIM), F32)],
        compiler_params=_cparams("parallel"),
        name="qkv_rope",
    )(proj, proj, proj, cs, q_norm, wq_aug, kv_norm, w_k, w_vt)


FLASH_TQ = 512
FLASH_TK = 256


def _flash_kernel(it_ref, jt_ref, q_ref, k_ref, vt_ref, o_ref, m_ref, acc_ref, *, bq):
    tq, tk = FLASH_TQ, FLASH_TK
    t = pl.program_id(1)
    i = it_ref[t]
    j = jt_ref[t]

    @pl.when(j == 0)
    def _():
        m_ref[...] = jnp.full(m_ref.shape, NEG_INF, F32)
        acc_ref[...] = jnp.zeros(acc_ref.shape, F32)

    def run(diag):
        tiles = [(a, b) for a in range(bq // tq) for b in range(bq // tk)
                 if not (diag and (b * tk) // CHUNK > (a * tq + tq - 1) // CHUNK)]

        def scores(a, b):
            return _dot_nt(k_ref[b * tk:(b + 1) * tk, :], q_ref[a * tq:(a + 1) * tq, :])

        s_next = scores(*tiles[0])
        for n, (a, b) in enumerate(tiles):
            cols = slice(a * tq, (a + 1) * tq)
            s = s_next
            if n + 1 < len(tiles):
                s_next = scores(*tiles[n + 1])
            if n == 0 or tiles[n - 1][0] != a:
                m, acc = m_ref[:, cols], acc_ref[:, cols]
            if diag and (b * tk + tk - 1) // CHUNK > (a * tq) // CHUNK:
                kc = (b * tk + lax.broadcasted_iota(jnp.int32, (tk, tq), 0)) // CHUNK
                qc = (a * tq + lax.broadcasted_iota(jnp.int32, (tk, tq), 1)) // CHUNK
                s = jnp.where(kc <= qc, s, NEG_INF)
            m_new = jnp.maximum(m, jnp.max(s, axis=0, keepdims=True))
            alpha = jnp.exp2(m - m_new)
            p = jnp.exp2(s - m_new).astype(BF16)
            acc = alpha * acc + _dot(vt_ref[:, b * tk:(b + 1) * tk], p)
            m = m_new
            if n + 1 == len(tiles) or tiles[n + 1][0] != a:
                if diag:
                    row = i * bq + a * tq + lax.broadcasted_iota(jnp.int32, (V_DIM, tq), 1)
                    o = jnp.where(row >= FRONT - N_META, acc[0:V_DIM, :] / acc[V_DIM:V_DIM + 1, :], 0.0)
                    o_ref[cols, :] = o.T.astype(o_ref.dtype)
                else:
                    m_ref[:, cols], acc_ref[:, cols] = m, acc

    @pl.when(j < i)
    def _():
        run(False)

    @pl.when(j == i)
    def _():
        run(True)


def _flash(q, k, vt):
    H, R, _ = q.shape
    bq = next(b for b in (1536, 1024, 512) if R % b == 0)
    n = R // bq
    pairs = [(i, j) for i in range(n) for j in range(i + 1)]
    it = jnp.array([p[0] for p in pairs], jnp.int32)
    jt = jnp.array([p[1] for p in pairs], jnp.int32)
    grid_spec = pltpu.PrefetchScalarGridSpec(
        num_scalar_prefetch=2,
        grid=(H, len(pairs)),
        in_specs=[pl.BlockSpec((None, bq, QK_PAD), lambda h, t, it, jt: (h, it[t], 0)),
                  pl.BlockSpec((None, bq, QK_PAD), lambda h, t, it, jt: (h, jt[t], 0)),
                  pl.BlockSpec((None, V_ROWS, bq), lambda h, t, it, jt: (h, 0, jt[t]))],
        out_specs=pl.BlockSpec((bq, V_DIM), lambda h, t, it, jt: (it[t], h)),
        scratch_shapes=[pltpu.VMEM((1, bq), F32), pltpu.VMEM((V_ROWS, bq), F32)],
    )
    return pl.pallas_call(
        functools.partial(_flash_kernel, bq=bq),
        grid_spec=grid_spec,
        out_shape=jax.ShapeDtypeStruct((R, H * V_DIM), BF16),
        compiler_params=_cparams("parallel", "arbitrary"),
        name="prompt_attention",
    )(it, jt, q, k, vt)


def _cached_attn_kernel(q_ref, lat_ref, kpe_ref, w_ref, o_ref):
    kv = _dot(lat_ref[...], w_ref[...]).astype(BF16)
    kpe = kpe_ref[...]
    for h in range(MLA_HEADS):
        c = h * 256
        qh = q_ref[h]
        s = _dot_nt(qh[:, 0:NOPE_DIM], kv[:, c:c + NOPE_DIM]) + _dot_nt(qh[:, NOPE_DIM:QK_DIM], kpe)
        m = jnp.max(s, axis=1, keepdims=True)
        p = jnp.exp2(s - m)
        p = p / jnp.sum(p, axis=1, keepdims=True)
        o_ref[:, h * V_DIM:(h + 1) * V_DIM] = _dot(p.astype(BF16), kv[:, c + NOPE_DIM:c + 256]).astype(o_ref.dtype)


def _cached_attn(q, lat_all, kpe_all, w_ukv, lay):
    H = MLA_HEADS
    B, Lk, _ = lat_all.shape
    S = lay["dec_seq"]
    blk0 = lay["s0"] // S
    return pl.pallas_call(
        _cached_attn_kernel,
        grid=(B,),
        in_specs=[pl.BlockSpec((H, S, QK_PAD), lambda b: (0, blk0 + b, 0)),
                  pl.BlockSpec((None, Lk, KV_LORA), lambda b: (b, 0, 0)),
                  pl.BlockSpec((None, Lk, ROPE_DIM), lambda b: (b, 0, 0)),
                  pl.BlockSpec((KV_LORA, H * 256), lambda b: (0, 0))],
        out_specs=pl.BlockSpec((S, H * V_DIM), lambda b: (b, 0)),
        out_shape=jax.ShapeDtypeStruct((B * S, H * V_DIM), BF16),
        compiler_params=_cparams("parallel"),
        name="sample_attention",
    )(q, lat_all, kpe_all, w_ukv)


def _s5_kernel(u_ref, bm_ref, a_ref, cm_ref, d_ref, h0_ref, y_ref, ps_ref, ss_ref, hre, him, carry,
               *, p_blk, p_off, n_blk, tb, dec_seq):
    i = pl.program_id(0)
    u = u_ref[...]
    hb = _dot(u.astype(BF16), bm_ref[...])
    hre[...] = hb[:, 0:SSM_CH]
    him[...] = hb[:, SSM_CH:2 * SSM_CH]
    ar = a_ref[0:1, :]
    ai = a_ref[1:2, :]

    def scan(start, n, hr, hi):
        def body(t, c):
            hr, hi = c
            r = start + t
            nr = ar * hr - ai * hi + hre[pl.ds(r, 1), :]
            ni = ar * hi + ai * hr + him[pl.ds(r, 1), :]
            hre[pl.ds(r, 1), :] = nr
            him[pl.ds(r, 1), :] = ni
            return nr, ni
        return lax.fori_loop(0, n, body, (hr, hi), unroll=2)

    @pl.when(i == 0)
    def _():
        carry[...] = jnp.zeros(carry.shape, F32)

    @pl.when(i < n_blk - 1)
    def _():
        hr, hi = scan(0, tb, carry[0:1, :], carry[1:2, :])
        carry[0:1, :] = hr
        carry[1:2, :] = hi

    @pl.when(i == p_blk)
    def _():
        ps_ref[0:1, :] = hre[p_off:p_off + 1, :]
        ps_ref[1:2, :] = him[p_off:p_off + 1, :]

    @pl.when(i == n_blk - 1)
    def _():
        for s in range(tb // dec_seq):
            hr, hi = scan(s * dec_seq, dec_seq, h0_ref[0, s:s + 1, :], h0_ref[1, s:s + 1, :])
            ss_ref[0, s:s + 1, :] = hr
            ss_ref[1, s:s + 1, :] = hi

    y = _dot(hre[...].astype(BF16), cm_ref[0]) + _dot(him[...].astype(BF16), cm_ref[1])
    y_ref[...] = _gelu(y + d_ref[...] * u).astype(BF16)


def _s5(proj, bmat, abar, cmat, d, h0, lay):
    R = proj.shape[0]
    tb = lay["ns"]
    n_blk = R // tb
    last = lay["p_end"] - 1
    nseq = tb // lay["dec_seq"]
    kern = functools.partial(_s5_kernel, p_blk=last // tb, p_off=last % tb, n_blk=n_blk, tb=tb,
                             dec_seq=lay["dec_seq"])
    return pl.pallas_call(
        kern,
        grid=(n_blk,),
        in_specs=[pl.BlockSpec((tb, SSM_WIDTH), lambda i: (i, C_U // SSM_WIDTH)),
                  pl.BlockSpec((SSM_WIDTH, 2 * SSM_CH), lambda i: (0, 0)),
                  pl.BlockSpec((2, SSM_CH), lambda i: (0, 0)),
                  pl.BlockSpec((2, SSM_CH, SSM_WIDTH), lambda i: (0, 0, 0)),
                  pl.BlockSpec((1, SSM_WIDTH), lambda i: (0, 0)),
                  pl.BlockSpec((2, nseq, SSM_CH), lambda i: (0, 0, 0))],
        out_specs=[pl.BlockSpec((tb, SSM_WIDTH), lambda i: (i, 0)),
                   pl.BlockSpec((2, SSM_CH), lambda i: (0, 0)),
                   pl.BlockSpec((2, nseq, SSM_CH), lambda i: (0, 0, 0))],
        out_shape=[jax.ShapeDtypeStruct((R, SSM_WIDTH), BF16),
                   jax.ShapeDtypeStruct((2, SSM_CH), F32),
                   jax.ShapeDtypeStruct((2, nseq, SSM_CH), F32)],
        scratch_shapes=[pltpu.VMEM((tb, SSM_CH), F32), pltpu.VMEM((tb, SSM_CH), F32),
                        pltpu.VMEM((2, SSM_CH), F32)],
        compiler_params=_cparams("arbitrary"),
        name="s5_scan",
    )(proj, bmat, abar, cmat, d, h0)


MERGE_TN = 256
MERGE_NC = D_MODEL // MERGE_TN


def _merge_kernel(x_ref, g_ref, za_ref, o_ref, yc_ref, wg0_ref, wg1_ref, wg2_ref, b0_ref, b1_ref, b2_ref,
                  wa_ref, wb_ref, wga_ref, wgb_ref, wo_ref, out_ref, xn_s, mg_s):
    c = pl.program_id(1)

    @pl.when(c == 0)
    def _():
        xn_s[...] = _rms(x_ref[...], g_ref[...]).astype(BF16)

    xn = xn_s[...]
    g0 = _sigmoid(_dot(xn, wg0_ref[...]) + b0_ref[...])
    g1 = _sigmoid(_dot(xn, wg1_ref[...]) + b1_ref[...])
    g2 = _sigmoid(_dot(xn, wg2_ref[...]) + b2_ref[...])
    yc = yc_ref[...]
    y_a = _dot(za_ref[...], wa_ref[...])
    y_b = _dot(o_ref[...], wb_ref[...])
    y_c = _dot(yc, wga_ref[...]) * _sigmoid(_dot(yc, wgb_ref[...]))
    mg_s[c] = (g0 * y_a + g1 * y_b + g2 * y_c).astype(BF16)

    @pl.when(c == MERGE_NC - 1)
    def _():
        acc = x_ref[...]
        for cc in range(MERGE_NC):
            acc = acc + _dot(mg_s[cc], wo_ref[cc * MERGE_TN:(cc + 1) * MERGE_TN, :])
        out_ref[...] = acc


def _merge(x, g, za, o, yc, w_gate, b_gate, w_a, w_b, w_glu, w_o):
    R = x.shape[0]
    tb = 512
    tn, nc = MERGE_TN, MERGE_NC
    row = lambda w: pl.BlockSpec((tb, w), lambda i, c: (i, 0))
    col = lambda k, off: pl.BlockSpec((k, tn), lambda i, c: (0, off + c))
    return pl.pallas_call(
        _merge_kernel,
        grid=(R // tb, nc),
        in_specs=[row(D_MODEL), pl.BlockSpec((1, D_MODEL), lambda i, c: (0, 0)),
                  row(A_WIDTH), row(MLA_HEADS * V_DIM), row(SSM_WIDTH),
                  col(D_MODEL, 0), col(D_MODEL, nc), col(D_MODEL, 2 * nc),
                  col(1, 0), col(1, nc), col(1, 2 * nc),
                  col(A_WIDTH, 0), col(MLA_HEADS * V_DIM, 0), col(SSM_WIDTH, 0), col(SSM_WIDTH, nc),
                  pl.BlockSpec((D_MODEL, D_MODEL), lambda i, c: (0, 0))],
        out_specs=row(D_MODEL),
        out_shape=jax.ShapeDtypeStruct((R, D_MODEL), F32),
        scratch_shapes=[pltpu.VMEM((tb, D_MODEL), BF16), pltpu.VMEM((nc, tb, tn), BF16)],
        compiler_params=_cparams("parallel", "arbitrary"),
        name="branch_merge",
    )(x, g, za, o, yc, w_gate, w_gate, w_gate, b_gate, b_gate, b_gate, w_a, w_b, w_glu, w_glu, w_o)


SUBLANES = 8


def _sort_desc(v):
    n = len(v)
    v = list(v)
    k = 2
    while k <= n:
        j = k // 2
        while j >= 1:
            for i in range(n):
                l = i ^ j
                if l > i:
                    hi, lo = jnp.maximum(v[i], v[l]), jnp.minimum(v[i], v[l])
                    v[i], v[l] = (hi, lo) if (i & k) == 0 else (lo, hi)
            j //= 2
        k *= 2
    return v


def _merge_top(a, b):
    n = len(a)
    v = [jnp.maximum(a[i], b[n - 1 - i]) for i in range(n)]
    j = n // 2
    while j >= 1:
        for i in range(n):
            l = i ^ j
            if l > i:
                v[i], v[l] = jnp.maximum(v[i], v[l]), jnp.minimum(v[i], v[l])
        j //= 2
    return v


def _top_desc(tiles):
    v = _sort_desc(tiles)
    shift = SUBLANES // 2
    while shift >= 1:
        v = _merge_top(v, [pltpu.roll(x, shift, 0) for x in v])
        shift //= 2
    return v


def _pack_sublanes(rows, sub):
    out = rows[0]
    for r in range(1, len(rows)):
        out = jnp.where(sub == r, rows[r], out)
    return out


def _route_kernel(x_ref, g_ref, wq_ref, k1_ref, k2_ref, xn_ref, th_ref, s2_ref, e1_ref, e2_ref, *, tb):
    K, NK, S = PEER_TOPK, PEER_KEYS, SUBLANES
    xn = _rms(x_ref[...], g_ref[...]).astype(BF16)
    xn_ref[...] = xn
    q = _dot(xn, wq_ref[...]).astype(BF16)
    half = PEER_QDIM // 2
    sub = lax.broadcasted_iota(jnp.int32, (S, tb), 0)
    neg = jnp.full((S, tb), -jnp.inf, F32)
    for h in range(PEER_HEADS):
        c = h * PEER_QDIM
        s1 = _dot_nt(k1_ref[...], q[:, c:c + half])
        s2 = _dot_nt(k2_ref[...], q[:, c + half:c + PEER_QDIM])
        t1 = [s1[S * r:S * (r + 1), :] for r in range(NK // S)]
        t2 = [s2[S * r:S * (r + 1), :] for r in range(NK // S)]
        v1 = _top_desc(t1)
        v2 = _top_desc(t2)
        v2_lo, v2_hi = _pack_sublanes(v2[0:S], sub), _pack_sublanes(v2[S:K], sub)
        v1_hi = _pack_sublanes(v1[S:K], sub)
        cand = [v1[0] + v2_lo, v1[0] + v2_hi] + [v1[a] + v2_lo for a in range(1, S)] + [v1_hi + v2[0]]
        vals = _top_desc(cand + [neg] * (K - len(cand)))
        tau = vals[K - 1]
        z = 1.0 + jnp.exp(vals[1] - vals[0])
        for r in range(2, K):
            z = z + jnp.exp(vals[r] - vals[0])
        scale = 0.5 / z
        for r in range(NK // S):
            rows = slice(S * r, S * (r + 1))
            th = jnp.full((S, tb), jnp.inf, F32)
            for b in range(K):
                th = jnp.where(t1[r] + v2[b] >= tau, v2[b], th)
            th_ref[h, rows, :] = th
            e1_ref[h, rows, :] = jnp.exp(t1[r] - v1[0]) * scale
            e2_ref[h, rows, :] = jnp.exp(t2[r] - v2[0])
        s2_ref[h] = s2


def _route(x, g, wq, k1, k2):
    R = x.shape[0]
    tb = 128
    H, NK = PEER_HEADS, PEER_KEYS
    tmap = pl.BlockSpec((H, NK, tb), lambda i: (0, 0, i))
    tshape = jax.ShapeDtypeStruct((H, NK, R), F32)
    return pl.pallas_call(
        functools.partial(_route_kernel, tb=tb),
        grid=(R // tb,),
        in_specs=[pl.BlockSpec((tb, D_MODEL), lambda i: (i, 0)),
                  pl.BlockSpec((1, D_MODEL), lambda i: (0, 0)),
                  pl.BlockSpec((D_MODEL, H * PEER_QDIM), lambda i: (0, 0)),
                  pl.BlockSpec((NK, PEER_QDIM // 2), lambda i: (0, 0)),
                  pl.BlockSpec((NK, PEER_QDIM // 2), lambda i: (0, 0))],
        out_specs=[pl.BlockSpec((tb, D_MODEL), lambda i: (i, 0)), tmap, tmap, tmap, tmap],
        out_shape=[jax.ShapeDtypeStruct((R, D_MODEL), BF16), tshape, tshape, tshape, tshape],
        compiler_params=_cparams("parallel"),
        name="peer_route",
    )(x, g, wq, k1, k2)


PEER_SUB = 8
PEER_EB = PEER_SUB * PEER_KEYS
PEER_TC = 128
GELU_C1 = math.sqrt(2.0 / math.pi)
GELU_C2 = GELU_C1 * 0.044715


def _expert_kernel(x_ref, xn_ref, th_ref, s2_ref, e1_ref, e2_ref, eu_ref, evt_ref, out_ref, acc_ref,
                   act_s, p_s, *, tb):
    e = pl.program_id(1)
    ne = pl.num_programs(1)
    NK = PEER_KEYS

    @pl.when(e == 0)
    def _():
        acc_ref[...] = jnp.zeros(acc_ref.shape, F32)

    act = _dot_nt(eu_ref[...], xn_ref[...])
    for sub in range(PEER_SUB):
        rows = slice(sub * NK, (sub + 1) * NK)
        for tc in range(tb // PEER_TC):
            lanes = slice(tc * PEER_TC, (tc + 1) * PEER_TC)
            w = jnp.zeros((NK, PEER_TC), F32)
            for h in range(PEER_HEADS):
                sel = s2_ref[h, :, lanes] >= th_ref[h, sub:sub + 1, lanes]
                w = w + jnp.where(sel, e1_ref[h, sub:sub + 1, lanes] * e2_ref[h, :, lanes], 0.0)
            act_s[rows, lanes] = w
    for sub in range(PEER_SUB):
        rows = slice(sub * NK, (sub + 1) * NK)
        for tc in range(tb // PEER_TC):
            lanes = slice(tc * PEER_TC, (tc + 1) * PEER_TC)
            a = act[rows, lanes]
            wa = act_s[rows, lanes] * a
            p_s[rows, lanes] = (wa + wa * jnp.tanh(a * (GELU_C1 + GELU_C2 * (a * a)))).astype(BF16)
    acc_ref[...] += _dot(evt_ref[...], p_s[...])

    @pl.when(e == ne - 1)
    def _():
        out_ref[...] = x_ref[...] + acc_ref[...].T


def _experts(x, xn, th, s2, e1, e2, e_u, e_vt):
    R = x.shape[0]
    tb = 512
    H, NK = PEER_HEADS, PEER_KEYS
    ne = e_u.shape[0] // PEER_EB
    tmap = pl.BlockSpec((H, NK, tb), lambda i, e: (0, 0, i))
    kmap = pl.BlockSpec((H, PEER_SUB, tb), lambda i, e: (0, e, i))
    kern = functools.partial(_expert_kernel, tb=tb)
    return pl.pallas_call(
        kern,
        grid=(R // tb, ne),
        in_specs=[pl.BlockSpec((tb, D_MODEL), lambda i, e: (i, 0)),
                  pl.BlockSpec((tb, D_MODEL), lambda i, e: (i, 0)),
                  kmap, tmap, kmap, tmap,
                  pl.BlockSpec((PEER_EB, D_MODEL), lambda i, e: (e, 0)),
                  pl.BlockSpec((D_MODEL, PEER_EB), lambda i, e: (0, e))],
        out_specs=pl.BlockSpec((tb, D_MODEL), lambda i, e: (i, 0)),
        out_shape=jax.ShapeDtypeStruct((R, D_MODEL), F32),
        scratch_shapes=[pltpu.VMEM((D_MODEL, tb), F32), pltpu.VMEM((PEER_EB, tb), F32),
                        pltpu.VMEM((PEER_EB, tb), BF16)],
        compiler_params=_cparams("parallel", "arbitrary"),
        name="peer_experts",
    )(x, xn, th, s2, e1, e2, e_u, e_vt)


def _final_norm_kernel(x_ref, g_ref, o_ref):
    o_ref[...] = _rms(x_ref[...], g_ref[...])


def _final_norm(x, g):
    R = x.shape[0]
    tb = 512
    return pl.pallas_call(
        _final_norm_kernel,
        grid=(R // tb,),
        in_specs=[pl.BlockSpec((tb, D_MODEL), lambda i: (i, 0)), pl.BlockSpec((1, D_MODEL), lambda i: (0, 0))],
        out_specs=pl.BlockSpec((tb, D_MODEL), lambda i: (i, 0)),
        out_shape=jax.ShapeDtypeStruct((R, D_MODEL), F32),
        compiler_params=_cparams("parallel"),
        name="final_norm",
    )(x, g)


def _s5_params(a_re, a_im, b_re, b_im, c_re, c_im, log_dt):
    dt = jnp.exp(log_dt)[:, None]
    mag = jnp.exp(dt * a_re)
    abar_re = mag * jnp.cos(dt * a_im)
    abar_im = mag * jnp.sin(dt * a_im)
    nr, ni = abar_re - 1.0, abar_im
    den = a_re * a_re + a_im * a_im
    coef_re = (nr * a_re + ni * a_im) / den
    coef_im = (ni * a_re - nr * a_im) / den
    bbar_re = coef_re[..., None] * b_re - coef_im[..., None] * b_im
    bbar_im = coef_re[..., None] * b_im + coef_im[..., None] * b_re
    eye = jnp.eye(SSM_GROUPS, dtype=F32)
    expand_b = lambda b: jnp.einsum('gpc,gh->gchp', b, eye).reshape(SSM_WIDTH, SSM_CH)
    expand_c = lambda c: jnp.einsum('gcp,gh->gphc', c, eye).reshape(SSM_CH, SSM_WIDTH)
    bmat = jnp.concatenate([expand_b(bbar_re), expand_b(bbar_im)], axis=1).astype(BF16)
    cmat = jnp.stack([expand_c(c_re), -expand_c(c_im)]).astype(BF16)
    abar = jnp.stack([abar_re.reshape(SSM_CH), abar_im.reshape(SSM_CH)])
    return bmat, abar, cmat


def _swap_halves(w):
    half = ROPE_DIM // 2
    return jnp.concatenate([w[..., half:], w[..., :half]], axis=-1)


def _rope_table(pos):
    half = ROPE_DIM // 2
    inv = 1.0 / (ROPE_THETA ** (jnp.arange(half, dtype=F32) / half))
    ang = pos[:, None] * inv[None, :]
    cos, sin = jnp.cos(ang), jnp.sin(ang)
    return jnp.concatenate([cos, cos, -sin, sin], axis=1)


def kernel(x_prompt, x_sample, cache_ckv, cache_kpe, state_conv, state_ssm_re, state_ssm_im, meta_tokens, norm_mix, norm_ffn, w_in, b_gate, conv_w, conv_b, w_a_out, q_norm, w_uq, kv_norm, w_ukv, w_b_out, ssm_a_re, ssm_a_im, ssm_b_re, ssm_b_im, ssm_c_re, ssm_c_im, ssm_log_dt, ssm_d, w_glu, w_o, peer_wq, peer_k1, peer_k2, peer_u, peer_v, norm_final):
    B, seq, _ = x_prompt.shape
    assert B == 1
    nb, dec_seq, _ = x_sample.shape
    ns = nb * dec_seq
    past = cache_ckv.shape[2]
    depth = w_in.shape[0]
    p_end = FRONT + seq
    R = -(-(p_end + ns) // ROW_ALIGN) * ROW_ALIGN
    s0 = R - ns
    lay = dict(ns=ns, dec_seq=dec_seq, p_end=p_end, s0=s0)

    x = jnp.concatenate([
        jnp.zeros((FRONT - N_META, D_MODEL), F32), meta_tokens, x_prompt[0],
        jnp.zeros((s0 - p_end, D_MODEL), F32), x_sample.reshape(ns, D_MODEL)], axis=0)
    pos = jnp.concatenate([
        jnp.arange(s0, dtype=F32) - (FRONT - N_META),
        jnp.tile(past + jnp.arange(dec_seq, dtype=F32), nb)])
    cs = _rope_table(pos)

    outs = [[] for _ in range(10)]
    for l in range(depth):
        wl = w_in[l]
        kp = wl[:, 2560:2624]
        w_small = jnp.concatenate([wl[:, 0:2560], wl[:, 2624:3136], kp, _swap_halves(kp)], axis=1).astype(BF16)
        w_gate = wl[:, 3136:].astype(BF16)
        wq3 = w_uq[l].reshape(Q_LORA, MLA_HEADS, QK_DIM)
        wq_aug = jnp.concatenate([wq3, _swap_halves(wq3[..., NOPE_DIM:])], axis=-1).reshape(Q_LORA, MLA_HEADS * 256)
        wq_aug = wq_aug.astype(BF16)
        wkv = w_ukv[l].astype(BF16)
        wkv3 = wkv.reshape(KV_LORA, MLA_HEADS, NOPE_DIM + V_DIM)
        w_k = wkv3[..., :NOPE_DIM].reshape(KV_LORA, MLA_HEADS * NOPE_DIM)
        w_vt = wkv3[..., NOPE_DIM:].transpose(1, 2, 0)
        bmat, abar, cmat = _s5_params(ssm_a_re[l], ssm_a_im[l], ssm_b_re[l], ssm_b_im[l], ssm_c_re[l],
                                      ssm_c_im[l], ssm_log_dt[l])
        h0 = jnp.stack([state_ssm_re[l].reshape(nb, SSM_CH), state_ssm_im[l].reshape(nb, SSM_CH)])

        proj = _in_proj(x, norm_mix[l][None], w_small)
        za, p_conv, s_conv = _conv(proj, conv_w[l], conv_b[l][None], state_conv[l].reshape(2 * nb, A_WIDTH), lay)
        q, k, vt, lat, kpe = _qkv(proj, cs, q_norm[l][None], wq_aug, kv_norm[l][None], w_k, w_vt)
        o = _flash(q, k, vt)
        lat_all = jnp.concatenate([cache_ckv[l], lat[s0:].reshape(nb, dec_seq, KV_LORA)], axis=1).astype(BF16)
        kpe_all = jnp.concatenate([cache_kpe[l], kpe[s0:].reshape(nb, dec_seq, ROPE_DIM)], axis=1).astype(BF16)
        o_s = _cached_attn(q, lat_all, kpe_all, wkv, lay)
        o = lax.dynamic_update_slice(o, o_s, (s0, 0))
        yc, p_state, s_state = _s5(proj, bmat, abar, cmat, ssm_d[l][None], h0, lay)
        x = _merge(x, norm_mix[l][None], za, o, yc, w_gate, b_gate[l][None], w_a_out[l].astype(BF16),
                   w_b_out[l].astype(BF16), w_glu[l].astype(BF16), w_o[l].astype(BF16))
        xn2, th, s2, e1, e2 = _route(x, norm_ffn[l][None], peer_wq[l].astype(BF16),
                                     peer_k1[l].astype(BF16), peer_k2[l].astype(BF16))
        x = _experts(x, xn2, th, s2, e1, e2, peer_u[l].astype(BF16), peer_v[l].astype(BF16).T)

        lo = FRONT - N_META
        outs[0].append(lat[lo:p_end][None])
        outs[1].append(kpe[lo:p_end][None])
        outs[2].append(p_conv[None])
        outs[3].append(p_state[0].reshape(1, SSM_GROUPS, SSM_STATE))
        outs[4].append(p_state[1].reshape(1, SSM_GROUPS, SSM_STATE))
        outs[5].append(lat[s0:].reshape(nb, dec_seq, KV_LORA))
        outs[6].append(kpe[s0:].reshape(nb, dec_seq, ROPE_DIM))
        outs[7].append(s_conv.reshape(nb, CONV_WIDTH - 1, A_WIDTH))
        outs[8].append(s_state[0].reshape(nb, SSM_GROUPS, SSM_STATE))
        outs[9].append(s_state[1].reshape(nb, SSM_GROUPS, SSM_STATE))

    y = _final_norm(x, norm_final[None])
    y_prompt = y[FRONT:p_end][None]
    y_sample = y[s0:].reshape(nb, dec_seq, D_MODEL)
    return (y_prompt, y_sample) + tuple(jnp.stack(o) for o in outs)
```

```python
import functools
import math

import jax
import jax.numpy as jnp
from jax import lax
from jax.experimental import pallas as pl
from jax.experimental.pallas import tpu as pltpu

F32 = jnp.float32
BF16 = jnp.bfloat16

D_MODEL = 2048
CHUNK = 64
N_META = 16
EPS = 1e-6
NEG_INF = -1e30
A_WIDTH = 512
CONV_WIDTH = 3
MLA_HEADS = 8
Q_LORA = 512
KV_LORA = 512
NOPE_DIM = 128
ROPE_DIM = 64
V_DIM = 128
QK_DIM = NOPE_DIM + ROPE_DIM
ROPE_THETA = 10000.0
ATTN_SCALE = 1.0 / math.sqrt(NOPE_DIM + ROPE_DIM)
SSM_GROUP = 16
SSM_GROUPS = 32
SSM_WIDTH = SSM_GROUP * SSM_GROUPS
SSM_STATE = 64
SSM_CH = SSM_GROUPS * SSM_STATE
PEER_HEADS = 8
PEER_KEYS = 128
PEER_QDIM = 256
PEER_TOPK = 16

FRONT = CHUNK
ROW_ALIGN = 512
VMEM_LIMIT = 60 * 1024 * 1024

C_AB, C_AC, C_AH, C_Q, C_KV, C_U, C_KPE, N_SMALL = 0, 512, 1024, 1536, 2048, 2560, 3072, 3200


def _cparams(*sem):
    return pltpu.CompilerParams(dimension_semantics=sem, vmem_limit_bytes=VMEM_LIMIT)


def _rms(x, g):
    return x * lax.rsqrt(jnp.mean(x * x, axis=-1, keepdims=True) + EPS) * g


def _sigmoid(x):
    return 1.0 / (1.0 + jnp.exp(-x))


def _gelu(x):
    return 0.5 * x * (1.0 + jnp.tanh(math.sqrt(2.0 / math.pi) * (x + 0.044715 * (x * x * x))))


def _dot(a, b):
    return jnp.dot(a, b, preferred_element_type=F32)


def _dot_nt(a, b):
    return lax.dot_general(a, b, (((1,), (1,)), ((), ())), preferred_element_type=F32)


def _in_proj_kernel(x_ref, g_ref, w_ref, o_ref):
    xn = _rms(x_ref[...], g_ref[...]).astype(BF16)
    o_ref[...] = _dot(xn, w_ref[...])


def _in_proj(x, g, w_small):
    R = x.shape[0]
    tb = 256
    return pl.pallas_call(
        _in_proj_kernel,
        grid=(R // tb,),
        in_specs=[pl.BlockSpec((tb, D_MODEL), lambda i: (i, 0)),
                  pl.BlockSpec((1, D_MODEL), lambda i: (0, 0)),
                  pl.BlockSpec((D_MODEL, N_SMALL), lambda i: (0, 0))],
        out_specs=pl.BlockSpec((tb, N_SMALL), lambda i: (i, 0)),
        out_shape=jax.ShapeDtypeStruct((R, N_SMALL), F32),
        compiler_params=_cparams("parallel"),
        name="in_proj",
    )(x, g, w_small)


def _conv_kernel(p_ref, w_ref, b_ref, st_ref, za_ref, pc_ref, sc_ref, zbuf, *, p_blk, p_off, n_blk, tb, dec_seq):
    i = pl.program_id(0)
    ab = p_ref[:, C_AB:C_AB + A_WIDTH]
    z = p_ref[:, C_AC:C_AC + A_WIDTH] * p_ref[:, C_AH:C_AH + A_WIDTH]
    w0, w1, w2 = w_ref[0:1, :], w_ref[1:2, :], w_ref[2:3, :]
    b = b_ref[...]

    @pl.when(i == 0)
    def _():
        zbuf[0:8, :] = jnp.zeros((8, A_WIDTH), F32)

    @pl.when(i < n_blk - 1)
    def _():
        zbuf[8:8 + tb, :] = z
        y = b + w0 * zbuf[6:6 + tb, :] + w1 * zbuf[7:7 + tb, :] + w2 * zbuf[8:8 + tb, :]
        za_ref[...] = (ab * y).astype(BF16)
        zbuf[0:8, :] = zbuf[tb:tb + 8, :]

    @pl.when(i == p_blk)
    def _():
        pc_ref[...] = z[p_off - 1:p_off + 1, :]

    @pl.when(i == n_blk - 1)
    def _():
        for s in range(tb // dec_seq):
            r0 = s * dec_seq
            zbuf[6:8, :] = st_ref[2 * s:2 * s + 2, :]
            zbuf[8:8 + dec_seq, :] = z[r0:r0 + dec_seq, :]
            y = (b + w0 * zbuf[6:6 + dec_seq, :] + w1 * zbuf[7:7 + dec_seq, :]
                 + w2 * zbuf[8:8 + dec_seq, :])
            za_ref[r0:r0 + dec_seq, :] = (ab[r0:r0 + dec_seq, :] * y).astype(BF16)
            sc_ref[2 * s:2 * s + 2, :] = z[r0 + dec_seq - 2:r0 + dec_seq, :]


def _conv(proj, conv_w, conv_b, state, lay):
    R = proj.shape[0]
    tb = lay["ns"]
    n_blk = R // tb
    last = lay["p_end"] - 1
    kern = functools.partial(_conv_kernel, p_blk=last // tb, p_off=last % tb, n_blk=n_blk, tb=tb,
                             dec_seq=lay["dec_seq"])
    assert last % tb >= 1
    nseq = tb // lay["dec_seq"]
    return pl.pallas_call(
        kern,
        grid=(n_blk,),
        in_specs=[pl.BlockSpec((tb, 3 * A_WIDTH), lambda i: (i, 0)),
                  pl.BlockSpec((CONV_WIDTH, A_WIDTH), lambda i: (0, 0)),
                  pl.BlockSpec((1, A_WIDTH), lambda i: (0, 0)),
                  pl.BlockSpec((2 * nseq, A_WIDTH), lambda i: (0, 0))],
        out_specs=[pl.BlockSpec((tb, A_WIDTH), lambda i: (i, 0)),
                   pl.BlockSpec((2, A_WIDTH), lambda i: (0, 0)),
                   pl.BlockSpec((2 * nseq, A_WIDTH), lambda i: (0, 0))],
        out_shape=[jax.ShapeDtypeStruct((R, A_WIDTH), BF16),
                   jax.ShapeDtypeStruct((2, A_WIDTH), F32),
                   jax.ShapeDtypeStruct((2 * nseq, A_WIDTH), F32)],
        scratch_shapes=[pltpu.VMEM((tb + 8, A_WIDTH), F32)],
        compiler_params=_cparams("arbitrary"),
        name="short_conv",
    )(proj, conv_w, conv_b, state)


QK_PAD = 256
Q_SCALE = ATTN_SCALE * math.log2(math.e)


V_ROWS = V_DIM + 16


def _qkv_kernel(cq_ref, ckv_ref, kp_ref, cs_ref, qn_ref, wq_ref, kn_ref, wk_ref, wvt_ref,
                q_ref, k_ref, vt_ref, lat_ref, kpe_ref, *, tb):
    cs = cs_ref[...]
    qc = _rms(cq_ref[...], qn_ref[...]).astype(BF16)
    q = _dot(qc, wq_ref[...])
    lat = _rms(ckv_ref[...], kn_ref[...])
    lat_ref[...] = lat
    lat_b = lat.astype(BF16)
    kn = _dot(lat_b, wk_ref[...])
    ones_row = jnp.where(lax.broadcasted_iota(jnp.int32, (V_ROWS - V_DIM, tb), 0) == 0, 1.0, 0.0).astype(BF16)
    lane = lax.broadcasted_iota(jnp.int32, (tb, 2 * ROPE_DIM), 1)
    row = pl.program_id(0) * tb + lax.broadcasted_iota(jnp.int32, (tb, 2 * ROPE_DIM), 0)
    q_tail = jnp.where(lane == ROPE_DIM, 1.0, 0.0)
    k_tail = jnp.where((lane == ROPE_DIM) & (row < FRONT - N_META), NEG_INF, 0.0)
    t = kp_ref[...] * cs
    kpe = t + pltpu.roll(t, ROPE_DIM, 1)
    kpe_ref[...] = kpe[:, 0:ROPE_DIM]
    k_hi = jnp.where(lane < ROPE_DIM, kpe, k_tail).astype(BF16)
    for h in range(MLA_HEADS):
        c = h * 256
        q_ref[h, :, 0:NOPE_DIM] = (q[:, c:c + NOPE_DIM] * Q_SCALE).astype(BF16)
        t = q[:, c + NOPE_DIM:c + 256] * cs
        qpe = (t + pltpu.roll(t, ROPE_DIM, 1)) * Q_SCALE
        q_ref[h, :, NOPE_DIM:QK_PAD] = jnp.where(lane < ROPE_DIM, qpe, q_tail).astype(BF16)
        k_ref[h, :, 0:NOPE_DIM] = kn[:, h * NOPE_DIM:(h + 1) * NOPE_DIM].astype(BF16)
        k_ref[h, :, NOPE_DIM:QK_PAD] = k_hi
        vt_ref[h, 0:V_DIM, :] = _dot_nt(wvt_ref[h], lat_b).astype(BF16)
        vt_ref[h, V_DIM:V_ROWS, :] = ones_row


def _qkv(proj, cs, q_norm, wq_aug, kv_norm, w_k, w_vt):
    R = proj.shape[0]
    tb = 256
    H = MLA_HEADS
    return pl.pallas_call(
        functools.partial(_qkv_kernel, tb=tb),
        grid=(R // tb,),
        in_specs=[pl.BlockSpec((tb, Q_LORA), lambda i: (i, C_Q // Q_LORA)),
                  pl.BlockSpec((tb, KV_LORA), lambda i: (i, C_KV // KV_LORA)),
                  pl.BlockSpec((tb, 2 * ROPE_DIM), lambda i: (i, C_KPE // (2 * ROPE_DIM))),
                  pl.BlockSpec((tb, 2 * ROPE_DIM), lambda i: (i, 0)),
                  pl.BlockSpec((1, Q_LORA), lambda i: (0, 0)),
                  pl.BlockSpec((Q_LORA, H * 256), lambda i: (0, 0)),
                  pl.BlockSpec((1, KV_LORA), lambda i: (0, 0)),
                  pl.BlockSpec((KV_LORA, H * NOPE_DIM), lambda i: (0, 0)),
                  pl.BlockSpec((H, V_DIM, KV_LORA), lambda i: (0, 0, 0))],
        out_specs=[pl.BlockSpec((H, tb, QK_PAD), lambda i: (0, i, 0)),
                   pl.BlockSpec((H, tb, QK_PAD), lambda i: (0, i, 0)),
                   pl.BlockSpec((H, V_ROWS, tb), lambda i: (0, 0, i)),
                   pl.BlockSpec((tb, KV_LORA), lambda i: (i, 0)),
                   pl.BlockSpec((tb, ROPE_DIM), lambda i: (i, 0))],
        out_shape=[jax.ShapeDtypeStruct((H, R, QK_PAD), BF16),
                   jax.ShapeDtypeStruct((H, R, QK_PAD), BF16),
                   jax.ShapeDtypeStruct((H, V_ROWS, R), BF16),
                   jax.ShapeDtypeStruct((R, KV_LORA), F32),
                   jax.ShapeDtypeStruct((R, ROPE_DIM), F32)],
        compiler_params=_cparams("parallel"),
        name="qkv_rope",
    )(proj, proj, proj, cs, q_norm, wq_aug, kv_norm, w_k, w_vt)


FLASH_TQ = 512
FLASH_TK = 256


def _flash_kernel(it_ref, jt_ref, q_ref, k_ref, vt_ref, o_ref, m_ref, acc_ref, *, bq):
    tq, tk = FLASH_TQ, FLASH_TK
    t = pl.program_id(1)
    i = it_ref[t]
    j = jt_ref[t]

    @pl.when(j == 0)
    def _():
        m_ref[...] = jnp.full(m_ref.shape, NEG_INF, F32)
        acc_ref[...] = jnp.zeros(acc_ref.shape, F32)

    def run(diag):
        tiles = [(a, b) for a in range(bq // tq) for b in range(bq // tk)
                 if not (diag and (b * tk) // CHUNK > (a * tq + tq - 1) // CHUNK)]

        def scores(a, b):
            return _dot_nt(k_ref[b * tk:(b + 1) * tk, :], q_ref[a * tq:(a + 1) * tq, :])

        s_next = scores(*tiles[0])
        for n, (a, b) in enumerate(tiles):
            cols = slice(a * tq, (a + 1) * tq)
            s = s_next
            if n + 1 < len(tiles):
                s_next = scores(*tiles[n + 1])
            if n == 0 or tiles[n - 1][0] != a:
                m, acc = m_ref[:, cols], acc_ref[:, cols]
            if diag and (b * tk + tk - 1) // CHUNK > (a * tq) // CHUNK:
                kc = (b * tk + lax.broadcasted_iota(jnp.int32, (tk, tq), 0)) // CHUNK
                qc = (a * tq + lax.broadcasted_iota(jnp.int32, (tk, tq), 1)) // CHUNK
                s = jnp.where(kc <= qc, s, NEG_INF)
            m_new = jnp.maximum(m, jnp.max(s, axis=0, keepdims=True))
            alpha = jnp.exp2(m - m_new)
            p = jnp.exp2(s - m_new).astype(BF16)
            acc = alpha * acc + _dot(vt_ref[:, b * tk:(b + 1) * tk], p)
            m = m_new
            if n + 1 == len(tiles) or tiles[n + 1][0] != a:
                if diag:
                    row = i * bq + a * tq + lax.broadcasted_iota(jnp.int32, (V_DIM, tq), 1)
                    o = jnp.where(row >= FRONT - N_META, acc[0:V_DIM, :] / acc[V_DIM:V_DIM + 1, :], 0.0)
                    o_ref[cols, :] = o.T.astype(o_ref.dtype)
                else:
                    m_ref[:, cols], acc_ref[:, cols] = m, acc

    @pl.when(j < i)
    def _():
        run(False)

    @pl.when(j == i)
    def _():
        run(True)


def _flash(q, k, vt):
    H, R, _ = q.shape
    bq = next(b for b in (1536, 1024, 512) if R % b == 0)
    n = R // bq
    pairs = [(i, j) for i in range(n) for j in range(i + 1)]
    it = jnp.array([p[0] for p in pairs], jnp.int32)
    jt = jnp.array([p[1] for p in pairs], jnp.int32)
    grid_spec = pltpu.PrefetchScalarGridSpec(
        num_scalar_prefetch=2,
        grid=(H, len(pairs)),
        in_specs=[pl.BlockSpec((None, bq, QK_PAD), lambda h, t, it, jt: (h, it[t], 0)),
                  pl.BlockSpec((None, bq, QK_PAD), lambda h, t, it, jt: (h, jt[t], 0)),
                  pl.BlockSpec((None, V_ROWS, bq), lambda h, t, it, jt: (h, 0, jt[t]))],
        out_specs=pl.BlockSpec((bq, V_DIM), lambda h, t, it, jt: (it[t], h)),
        scratch_shapes=[pltpu.VMEM((1, bq), F32), pltpu.VMEM((V_ROWS, bq), F32)],
    )
    return pl.pallas_call(
        functools.partial(_flash_kernel, bq=bq),
        grid_spec=grid_spec,
        out_shape=jax.ShapeDtypeStruct((R, H * V_DIM), BF16),
        compiler_params=_cparams("parallel", "arbitrary"),
        name="prompt_attention",
    )(it, jt, q, k, vt)


def _cached_attn_kernel(q_ref, lat_ref, kpe_ref, w_ref, o_ref):
    kv = _dot(lat_ref[...], w_ref[...]).astype(BF16)
    kpe = kpe_ref[...]
    for h in range(MLA_HEADS):
        c = h * 256
        qh = q_ref[h]
        s = _dot_nt(qh[:, 0:NOPE_DIM], kv[:, c:c + NOPE_DIM]) + _dot_nt(qh[:, NOPE_DIM:QK_DIM], kpe)
        m = jnp.max(s, axis=1, keepdims=True)
        p = jnp.exp2(s - m)
        p = p / jnp.sum(p, axis=1, keepdims=True)
        o_ref[:, h * V_DIM:(h + 1) * V_DIM] = _dot(p.astype(BF16), kv[:, c + NOPE_DIM:c + 256]).astype(o_ref.dtype)


def _cached_attn(q, lat_all, kpe_all, w_ukv, lay):
    H = MLA_HEADS
    B, Lk, _ = lat_all.shape
    S = lay["dec_seq"]
    blk0 = lay["s0"] // S
    return pl.pallas_call(
        _cached_attn_kernel,
        grid=(B,),
        in_specs=[pl.BlockSpec((H, S, QK_PAD), lambda b: (0, blk0 + b, 0)),
                  pl.BlockSpec((None, Lk, KV_LORA), lambda b: (b, 0, 0)),
                  pl.BlockSpec((None, Lk, ROPE_DIM), lambda b: (b, 0, 0)),
                  pl.BlockSpec((KV_LORA, H * 256), lambda b: (0, 0))],
        out_specs=pl.BlockSpec((S, H * V_DIM), lambda b: (b, 0)),
        out_shape=jax.ShapeDtypeStruct((B * S, H * V_DIM), BF16),
        compiler_params=_cparams("parallel"),
        name="sample_attention",
    )(q, lat_all, kpe_all, w_ukv)


def _s5_kernel(u_ref, bm_ref, a_ref, cm_ref, d_ref, h0_ref, y_ref, ps_ref, ss_ref, hre, him, carry,
               *, p_blk, p_off, n_blk, tb, dec_seq):
    i = pl.program_id(0)
    u = u_ref[...]
    hb = _dot(u.astype(BF16), bm_ref[...])
    hre[...] = hb[:, 0:SSM_CH]
    him[...] = hb[:, SSM_CH:2 * SSM_CH]
    ar = a_ref[0:1, :]
    ai = a_ref[1:2, :]

    def scan(start, n, hr, hi):
        def body(t, c):
            hr, hi = c
            r = start + t
            nr = ar * hr - ai * hi + hre[pl.ds(r, 1), :]
            ni = ar * hi + ai * hr + him[pl.ds(r, 1), :]
            hre[pl.ds(r, 1), :] = nr
            him[pl.ds(r, 1), :] = ni
            return nr, ni
        return lax.fori_loop(0, n, body, (hr, hi), unroll=2)

    @pl.when(i == 0)
    def _():
        carry[...] = jnp.zeros(carry.shape, F32)

    @pl.when(i < n_blk - 1)
    def _():
        hr, hi = scan(0, tb, carry[0:1, :], carry[1:2, :])
        carry[0:1, :] = hr
        carry[1:2, :] = hi

    @pl.when(i == p_blk)
    def _():
        ps_ref[0:1, :] = hre[p_off:p_off + 1, :]
        ps_ref[1:2, :] = him[p_off:p_off + 1, :]

    @pl.when(i == n_blk - 1)
    def _():
        for s in range(tb // dec_seq):
            hr, hi = scan(s * dec_seq, dec_seq, h0_ref[0, s:s + 1, :], h0_ref[1, s:s + 1, :])
            ss_ref[0, s:s + 1, :] = hr
            ss_ref[1, s:s + 1, :] = hi

    y = _dot(hre[...].astype(BF16), cm_ref[0]) + _dot(him[...].astype(BF16), cm_ref[1])
    y_ref[...] = _gelu(y + d_ref[...] * u).astype(BF16)


def _s5(proj, bmat, abar, cmat, d, h0, lay):
    R = proj.shape[0]
    tb = lay["ns"]
    n_blk = R // tb
    last = lay["p_end"] - 1
    nseq = tb // lay["dec_seq"]
    kern = functools.partial(_s5_kernel, p_blk=last // tb, p_off=last % tb, n_blk=n_blk, tb=tb,
                             dec_seq=lay["dec_seq"])
    return pl.pallas_call(
        kern,
        grid=(n_blk,),
        in_specs=[pl.BlockSpec((tb, SSM_WIDTH), lambda i: (i, C_U // SSM_WIDTH)),
                  pl.BlockSpec((SSM_WIDTH, 2 * SSM_CH), lambda i: (0, 0)),
                  pl.BlockSpec((2, SSM_CH), lambda i: (0, 0)),
                  pl.BlockSpec((2, SSM_CH, SSM_WIDTH), lambda i: (0, 0, 0)),
                  pl.BlockSpec((1, SSM_WIDTH), lambda i: (0, 0)),
                  pl.BlockSpec((2, nseq, SSM_CH), lambda i: (0, 0, 0))],
        out_specs=[pl.BlockSpec((tb, SSM_WIDTH), lambda i: (i, 0)),
                   pl.BlockSpec((2, SSM_CH), lambda i: (0, 0)),
                   pl.BlockSpec((2, nseq, SSM_CH), lambda i: (0, 0, 0))],
        out_shape=[jax.ShapeDtypeStruct((R, SSM_WIDTH), BF16),
                   jax.ShapeDtypeStruct((2, SSM_CH), F32),
                   jax.ShapeDtypeStruct((2, nseq, SSM_CH), F32)],
        scratch_shapes=[pltpu.VMEM((tb, SSM_CH), F32), pltpu.VMEM((tb, SSM_CH), F32),
                        pltpu.VMEM((2, SSM_CH), F32)],
        compiler_params=_cparams("arbitrary"),
        name="s5_scan",
    )(proj, bmat, abar, cmat, d, h0)


MERGE_TN = 256
MERGE_NC = D_MODEL // MERGE_TN


def _merge_kernel(x_ref, g_ref, za_ref, o_ref, yc_ref, wg0_ref, wg1_ref, wg2_ref, b0_ref, b1_ref, b2_ref,
                  wa_ref, wb_ref, wga_ref, wgb_ref, wo_ref, out_ref, xn_s, mg_s):
    c = pl.program_id(1)

    @pl.when(c == 0)
    def _():
        xn_s[...] = _rms(x_ref[...], g_ref[...]).astype(BF16)

    xn = xn_s[...]
    g0 = _sigmoid(_dot(xn, wg0_ref[...]) + b0_ref[...])
    g1 = _sigmoid(_dot(xn, wg1_ref[...]) + b1_ref[...])
    g2 = _sigmoid(_dot(xn, wg2_ref[...]) + b2_ref[...])
    yc = yc_ref[...]
    y_a = _dot(za_ref[...], wa_ref[...])
    y_b = _dot(o_ref[...], wb_ref[...])
    y_c = _dot(yc, wga_ref[...]) * _sigmoid(_dot(yc, wgb_ref[...]))
    mg_s[c] = (g0 * y_a + g1 * y_b + g2 * y_c).astype(BF16)

    @pl.when(c == MERGE_NC - 1)
    def _():
        acc = x_ref[...]
        for cc in range(MERGE_NC):
            acc = acc + _dot(mg_s[cc], wo_ref[cc * MERGE_TN:(cc + 1) * MERGE_TN, :])
        out_ref[...] = acc


def _merge(x, g, za, o, yc, w_gate, b_gate, w_a, w_b, w_glu, w_o):
    R = x.shape[0]
    tb = 512
    tn, nc = MERGE_TN, MERGE_NC
    row = lambda w: pl.BlockSpec((tb, w), lambda i, c: (i, 0))
    col = lambda k, off: pl.BlockSpec((k, tn), lambda i, c: (0, off + c))
    return pl.pallas_call(
        _merge_kernel,
        grid=(R // tb, nc),
        in_specs=[row(D_MODEL), pl.BlockSpec((1, D_MODEL), lambda i, c: (0, 0)),
                  row(A_WIDTH), row(MLA_HEADS * V_DIM), row(SSM_WIDTH),
                  col(D_MODEL, 0), col(D_MODEL, nc), col(D_MODEL, 2 * nc),
                  col(1, 0), col(1, nc), col(1, 2 * nc),
                  col(A_WIDTH, 0), col(MLA_HEADS * V_DIM, 0), col(SSM_WIDTH, 0), col(SSM_WIDTH, nc),
                  pl.BlockSpec((D_MODEL, D_MODEL), lambda i, c: (0, 0))],
        out_specs=row(D_MODEL),
        out_shape=jax.ShapeDtypeStruct((R, D_MODEL), F32),
        scratch_shapes=[pltpu.VMEM((tb, D_MODEL), BF16), pltpu.VMEM((nc, tb, tn), BF16)],
        compiler_params=_cparams("parallel", "arbitrary"),
        name="branch_merge",
    )(x, g, za, o, yc, w_gate, w_gate, w_gate, b_gate, b_gate, b_gate, w_a, w_b, w_glu, w_glu, w_o)


SUBLANES = 8


def _sort_desc(v):
    n = len(v)
    v = list(v)
    k = 2
    while k <= n:
        j = k // 2
        while j >= 1:
            for i in range(n):
                l = i ^ j
                if l > i:
                    hi, lo = jnp.maximum(v[i], v[l]), jnp.minimum(v[i], v[l])
                    v[i], v[l] = (hi, lo) if (i & k) == 0 else (lo, hi)
            j //= 2
        k *= 2
    return v


def _merge_top(a, b):
    n = len(a)
    v = [jnp.maximum(a[i], b[n - 1 - i]) for i in range(n)]
    j = n // 2
    while j >= 1:
        for i in range(n):
            l = i ^ j
            if l > i:
                v[i], v[l] = jnp.maximum(v[i], v[l]), jnp.minimum(v[i], v[l])
        j //= 2
    return v


def _top_desc(tiles):
    v = _sort_desc(tiles)
    shift = SUBLANES // 2
    while shift >= 1:
        v = _merge_top(v, [pltpu.roll(x, shift, 0) for x in v])
        shift //= 2
    return v


def _pack_sublanes(rows, sub):
    out = rows[0]
    for r in range(1, len(rows)):
        out = jnp.where(sub == r, rows[r], out)
    return out


def _route_kernel(x_ref, g_ref, wq_ref, k1_ref, k2_ref, xn_ref, th_ref, s2_ref, e1_ref, e2_ref, *, tb):
    K, NK, S = PEER_TOPK, PEER_KEYS, SUBLANES
    xn = _rms(x_ref[...], g_ref[...]).astype(BF16)
    xn_ref[...] = xn
    q = _dot(xn, wq_ref[...]).astype(BF16)
    half = PEER_QDIM // 2
    sub = lax.broadcasted_iota(jnp.int32, (S, tb), 0)
    neg = jnp.full((S, tb), -jnp.inf, F32)
    for h in range(PEER_HEADS):
        c = h * PEER_QDIM
        s1 = _dot_nt(k1_ref[...], q[:, c:c + half])
        s2 = _dot_nt(k2_ref[...], q[:, c + half:c + PEER_QDIM])
        t1 = [s1[S * r:S * (r + 1), :] for r in range(NK // S)]
        t2 = [s2[S * r:S * (r + 1), :] for r in range(NK // S)]
        v1 = _top_desc(t1)
        v2 = _top_desc(t2)
        v2_lo, v2_hi = _pack_sublanes(v2[0:S], sub), _pack_sublanes(v2[S:K], sub)
        v1_hi = _pack_sublanes(v1[S:K], sub)
        cand = [v1[0] + v2_lo, v1[0] + v2_hi] + [v1[a] + v2_lo for a in range(1, S)] + [v1_hi + v2[0]]
        vals = _top_desc(cand + [neg] * (K - len(cand)))
        tau = vals[K - 1]
        z = 1.0 + jnp.exp(vals[1] - vals[0])
        for r in range(2, K):
            z = z + jnp.exp(vals[r] - vals[0])
        scale = 0.5 / z
        for r in range(NK // S):
            rows = slice(S * r, S * (r + 1))
            th = jnp.full((S, tb), jnp.inf, F32)
            for b in range(K):
                th = jnp.where(t1[r] + v2[b] >= tau, v2[b], th)
            th_ref[h, rows, :] = th
            e1_ref[h, rows, :] = jnp.exp(t1[r] - v1[0]) * scale
            e2_ref[h, rows, :] = jnp.exp(t2[r] - v2[0])
        s2_ref[h] = s2


def _route(x, g, wq, k1, k2):
    R = x.shape[0]
    tb = 128
    H, NK = PEER_HEADS, PEER_KEYS
    tmap = pl.BlockSpec((H, None, NK, tb), lambda i: (0, i, 0, 0))
    tshape = jax.ShapeDtypeStruct((H, R // tb, NK, tb), F32)
    return pl.pallas_call(
        functools.partial(_route_kernel, tb=tb),
        grid=(R // tb,),
        in_specs=[pl.BlockSpec((tb, D_MODEL), lambda i: (i, 0)),
                  pl.BlockSpec((1, D_MODEL), lambda i: (0, 0)),
                  pl.BlockSpec((D_MODEL, H * PEER_QDIM), lambda i: (0, 0)),
                  pl.BlockSpec((NK, PEER_QDIM // 2), lambda i: (0, 0)),
                  pl.BlockSpec((NK, PEER_QDIM // 2), lambda i: (0, 0))],
        out_specs=[pl.BlockSpec((tb, D_MODEL), lambda i: (i, 0)), tmap, tmap, tmap, tmap],
        out_shape=[jax.ShapeDtypeStruct((R, D_MODEL), BF16), tshape, tshape, tshape, tshape],
        compiler_params=_cparams("parallel"),
        name="peer_route",
    )(x, g, wq, k1, k2)


PEER_SUB = 8
PEER_EB = PEER_SUB * PEER_KEYS
PEER_TC = 128
PEER_OUT_ROWS = 256
PEER_ACT_ROWS = 256
GELU_C1 = math.sqrt(2.0 / math.pi)
GELU_C2 = GELU_C1 * 0.044715


def _expert_kernel(x_ref, xn_ref, th_ref, s2_ref, e1_ref, e2_ref, eu_ref, evt_ref, out_ref, acc_ref,
                   act_even, act_odd, p_even, p_odd, *, tb, ne):
    s = pl.program_id(0)
    NK = PEER_KEYS
    proj_first = (s < 2) | ((s - 2) % ne == 0)
    proj_last = (s >= 2) & ((s - 2) % ne == ne - 1)

    @pl.when(s == 0)
    def _():
        act_odd[...] = jnp.zeros(act_odd.shape, F32)
        p_odd[...] = jnp.zeros(p_odd.shape, BF16)

    @pl.when(proj_first)
    def _():
        acc_ref[...] = jnp.zeros(acc_ref.shape, F32)

    n_piece = PEER_SUB * (tb // PEER_TC)
    n_out = D_MODEL // PEER_OUT_ROWS

    n_act = PEER_EB // PEER_ACT_ROWS

    def gate_and_project(p_new, p_old, act_new, act_old):
        for piece in range(n_piece):
            if piece < n_piece // 2 and piece % (n_piece // 2 // n_out) == 0:
                c = piece // (n_piece // 2 // n_out)
                orow = slice(c * PEER_OUT_ROWS, (c + 1) * PEER_OUT_ROWS)
                acc_ref[orow, :] += _dot(evt_ref[orow, :], p_old[...])
            if piece >= n_piece // 2 and piece % (n_piece // 2 // n_act) == 0:
                c = (piece - n_piece // 2) // (n_piece // 2 // n_act)
                erow = slice(c * PEER_ACT_ROWS, (c + 1) * PEER_ACT_ROWS)
                act = _dot_nt(eu_ref[erow, :], xn_ref[...])
                for t in range(tb // PEER_TC):
                    act_new[t, erow, :] = act[:, t * PEER_TC:(t + 1) * PEER_TC]
            sub, tc = divmod(piece, tb // PEER_TC)
            rows = slice(sub * NK, (sub + 1) * NK)
            lanes = slice(tc * PEER_TC, (tc + 1) * PEER_TC)
            w = jnp.zeros((NK, PEER_TC), F32)
            for h in range(PEER_HEADS):
                sel = s2_ref[h, tc] >= th_ref[h, tc, sub:sub + 1, :]
                w = w + jnp.where(sel, e1_ref[h, tc, sub:sub + 1, :] * e2_ref[h, tc], 0.0)
            a = act_old[tc, rows, :]
            wa = w * a
            p_new[rows, lanes] = (wa + wa * jnp.tanh(a * (GELU_C1 + GELU_C2 * (a * a)))).astype(BF16)

    @pl.when(s % 2 == 0)
    def _():
        gate_and_project(p_even, p_odd, act_even, act_odd)

    @pl.when(s % 2 == 1)
    def _():
        gate_and_project(p_odd, p_even, act_odd, act_even)

    @pl.when(proj_last)
    def _():
        out_ref[...] = x_ref[...] + acc_ref[...].T


def _experts(x, xn, th, s2, e1, e2, e_u, e_vt):
    R = x.shape[0]
    tb = 512
    H, NK = PEER_HEADS, PEER_KEYS
    ne = e_u.shape[0] // PEER_EB
    total = (R // tb) * ne
    pair = lambda s, lag: jnp.clip(s - lag, 0, total - 1)
    nt = tb // PEER_TC
    tmap = pl.BlockSpec((H, nt, NK, PEER_TC), lambda s: (0, pair(s, 1) // ne, 0, 0))
    kmap = pl.BlockSpec((H, nt, PEER_SUB, PEER_TC), lambda s: (0, pair(s, 1) // ne, pair(s, 1) % ne, 0))
    kern = functools.partial(_expert_kernel, tb=tb, ne=ne)
    return pl.pallas_call(
        kern,
        grid=(total + 2,),
        in_specs=[pl.BlockSpec((tb, D_MODEL), lambda s: (pair(s, 2) // ne, 0)),
                  pl.BlockSpec((tb, D_MODEL), lambda s: (pair(s, 0) // ne, 0)),
                  kmap, tmap, kmap, tmap,
                  pl.BlockSpec((PEER_EB, D_MODEL), lambda s: (pair(s, 0) % ne, 0)),
                  pl.BlockSpec((D_MODEL, PEER_EB), lambda s: (0, pair(s, 2) % ne))],
        out_specs=pl.BlockSpec((tb, D_MODEL), lambda s: (pair(s, 2) // ne, 0)),
        out_shape=jax.ShapeDtypeStruct((R, D_MODEL), F32),
        scratch_shapes=[pltpu.VMEM((D_MODEL, tb), F32),
                        pltpu.VMEM((nt, PEER_EB, PEER_TC), F32), pltpu.VMEM((nt, PEER_EB, PEER_TC), F32),
                        pltpu.VMEM((PEER_EB, tb), BF16), pltpu.VMEM((PEER_EB, tb), BF16)],
        compiler_params=_cparams("arbitrary"),
        name="peer_experts",
    )(x, xn, th, s2, e1, e2, e_u, e_vt)


def _final_norm_kernel(x_ref, g_ref, o_ref):
    o_ref[...] = _rms(x_ref[...], g_ref[...])


def _final_norm(x, g):
    R = x.shape[0]
    tb = 512
    return pl.pallas_call(
        _final_norm_kernel,
        grid=(R // tb,),
        in_specs=[pl.BlockSpec((tb, D_MODEL), lambda i: (i, 0)), pl.BlockSpec((1, D_MODEL), lambda i: (0, 0))],
        out_specs=pl.BlockSpec((tb, D_MODEL), lambda i: (i, 0)),
        out_shape=jax.ShapeDtypeStruct((R, D_MODEL), F32),
        compiler_params=_cparams("parallel"),
        name="final_norm",
    )(x, g)


def _s5_params(a_re, a_im, b_re, b_im, c_re, c_im, log_dt):
    dt = jnp.exp(log_dt)[:, None]
    mag = jnp.exp(dt * a_re)
    abar_re = mag * jnp.cos(dt * a_im)
    abar_im = mag * jnp.sin(dt * a_im)
    nr, ni = abar_re - 1.0, abar_im
    den = a_re * a_re + a_im * a_im
    coef_re = (nr * a_re + ni * a_im) / den
    coef_im = (ni * a_re - nr * a_im) / den
    bbar_re = coef_re[..., None] * b_re - coef_im[..., None] * b_im
    bbar_im = coef_re[..., None] * b_im + coef_im[..., None] * b_re
    eye = jnp.eye(SSM_GROUPS, dtype=F32)
    expand_b = lambda b: jnp.einsum('gpc,gh->gchp', b, eye).reshape(SSM_WIDTH, SSM_CH)
    expand_c = lambda c: jnp.einsum('gcp,gh->gphc', c, eye).reshape(SSM_CH, SSM_WIDTH)
    bmat = jnp.concatenate([expand_b(bbar_re), expand_b(bbar_im)], axis=1).astype(BF16)
    cmat = jnp.stack([expand_c(c_re), -expand_c(c_im)]).astype(BF16)
    abar = jnp.stack([abar_re.reshape(SSM_CH), abar_im.reshape(SSM_CH)])
    return bmat, abar, cmat


def _swap_halves(w):
    half = ROPE_DIM // 2
    return jnp.concatenate([w[..., half:], w[..., :half]], axis=-1)


def _rope_table(pos):
    half = ROPE_DIM // 2
    inv = 1.0 / (ROPE_THETA ** (jnp.arange(half, dtype=F32) / half))
    ang = pos[:, None] * inv[None, :]
    cos, sin = jnp.cos(ang), jnp.sin(ang)
    return jnp.concatenate([cos, cos, -sin, sin], axis=1)


def kernel(x_prompt, x_sample, cache_ckv, cache_kpe, state_conv, state_ssm_re, state_ssm_im, meta_tokens, norm_mix, norm_ffn, w_in, b_gate, conv_w, conv_b, w_a_out, q_norm, w_uq, kv_norm, w_ukv, w_b_out, ssm_a_re, ssm_a_im, ssm_b_re, ssm_b_im, ssm_c_re, ssm_c_im, ssm_log_dt, ssm_d, w_glu, w_o, peer_wq, peer_k1, peer_k2, peer_u, peer_v, norm_final):
    B, seq, _ = x_prompt.shape
    assert B == 1
    nb, dec_seq, _ = x_sample.shape
    ns = nb * dec_seq
    past = cache_ckv.shape[2]
    depth = w_in.shape[0]
    p_end = FRONT + seq
    R = -(-(p_end + ns) // ROW_ALIGN) * ROW_ALIGN
    s0 = R - ns
    lay = dict(ns=ns, dec_seq=dec_seq, p_end=p_end, s0=s0)

    x = jnp.concatenate([
        jnp.zeros((FRONT - N_META, D_MODEL), F32), meta_tokens, x_prompt[0],
        jnp.zeros((s0 - p_end, D_MODEL), F32), x_sample.reshape(ns, D_MODEL)], axis=0)
    pos = jnp.concatenate([
        jnp.arange(s0, dtype=F32) - (FRONT - N_META),
        jnp.tile(past + jnp.arange(dec_seq, dtype=F32), nb)])
    cs = _rope_table(pos)

    outs = [[] for _ in range(10)]
    for l in range(depth):
        wl = w_in[l]
        kp = wl[:, 2560:2624]
        w_small = jnp.concatenate([wl[:, 0:2560], wl[:, 2624:3136], kp, _swap_halves(kp)], axis=1).astype(BF16)
        w_gate = wl[:, 3136:].astype(BF16)
        wq3 = w_uq[l].reshape(Q_LORA, MLA_HEADS, QK_DIM)
        wq_aug = jnp.concatenate([wq3, _swap_halves(wq3[..., NOPE_DIM:])], axis=-1).reshape(Q_LORA, MLA_HEADS * 256)
        wq_aug = wq_aug.astype(BF16)
        wkv = w_ukv[l].astype(BF16)
        wkv3 = wkv.reshape(KV_LORA, MLA_HEADS, NOPE_DIM + V_DIM)
        w_k = wkv3[..., :NOPE_DIM].reshape(KV_LORA, MLA_HEADS * NOPE_DIM)
        w_vt = wkv3[..., NOPE_DIM:].transpose(1, 2, 0)
        bmat, abar, cmat = _s5_params(ssm_a_re[l], ssm_a_im[l], ssm_b_re[l], ssm_b_im[l], ssm_c_re[l],
                                      ssm_c_im[l], ssm_log_dt[l])
        h0 = jnp.stack([state_ssm_re[l].reshape(nb, SSM_CH), state_ssm_im[l].reshape(nb, SSM_CH)])

        proj = _in_proj(x, norm_mix[l][None], w_small)
        za, p_conv, s_conv = _conv(proj, conv_w[l], conv_b[l][None], state_conv[l].reshape(2 * nb, A_WIDTH), lay)
        q, k, vt, lat, kpe = _qkv(proj, cs, q_norm[l][None], wq_aug, kv_norm[l][None], w_k, w_vt)
        o = _flash(q, k, vt)
        lat_all = jnp.concatenate([cache_ckv[l], lat[s0:].reshape(nb, dec_seq, KV_LORA)], axis=1).astype(BF16)
        kpe_all = jnp.concatenate([cache_kpe[l], kpe[s0:].reshape(nb, dec_seq, ROPE_DIM)], axis=1).astype(BF16)
        o_s = _cached_attn(q, lat_all, kpe_all, wkv, lay)
        o = lax.dynamic_update_slice(o, o_s, (s0, 0))
        yc, p_state, s_state = _s5(proj, bmat, abar, cmat, ssm_d[l][None], h0, lay)
        x = _merge(x, norm_mix[l][None], za, o, yc, w_gate, b_gate[l][None], w_a_out[l].astype(BF16),
                   w_b_out[l].astype(BF16), w_glu[l].astype(BF16), w_o[l].astype(BF16))
        xn2, th, s2, e1, e2 = _route(x, norm_ffn[l][None], peer_wq[l].astype(BF16),
                                     peer_k1[l].astype(BF16), peer_k2[l].astype(BF16))
        x = _experts(x, xn2, th, s2, e1, e2, peer_u[l].astype(BF16), peer_v[l].astype(BF16).T)

        lo = FRONT - N_META
        outs[0].append(lat[lo:p_end][None])
        outs[1].append(kpe[lo:p_end][None])
        outs[2].append(p_conv[None])
        outs[3].append(p_state[0].reshape(1, SSM_GROUPS, SSM_STATE))
        outs[4].append(p_state[1].reshape(1, SSM_GROUPS, SSM_STATE))
        outs[5].append(lat[s0:].reshape(nb, dec_seq, KV_LORA))
        outs[6].append(kpe[s0:].reshape(nb, dec_seq, ROPE_DIM))
        outs[7].append(s_conv.reshape(nb, CONV_WIDTH - 1, A_WIDTH))
        outs[8].append(s_state[0].reshape(nb, SSM_GROUPS, SSM_STATE))
        outs[9].append(s_state[1].reshape(nb, SSM_GROUPS, SSM_STATE))

    y = _final_norm(x, norm_final[None])
    y_prompt = y[FRONT:p_end][None]
    y_sample = y[s0:].reshape(nb, dec_seq, D_MODEL)
    return (y_prompt, y_sample) + tuple(jnp.stack(o) for o in outs)
```

```python
import functools
import math

import jax
import jax.numpy as jnp
from jax import lax
from jax.experimental import pallas as pl
from jax.experimental.pallas import tpu as pltpu

F32 = jnp.float32
BF16 = jnp.bfloat16

D_MODEL = 2048
CHUNK = 64
N_META = 16
EPS = 1e-6
NEG_INF = -1e30
A_WIDTH = 512
CONV_WIDTH = 3
MLA_HEADS = 8
Q_LORA = 512
KV_LORA = 512
NOPE_DIM = 128
ROPE_DIM = 64
V_DIM = 128
QK_DIM = NOPE_DIM + ROPE_DIM
ROPE_THETA = 10000.0
ATTN_SCALE = 1.0 / math.sqrt(NOPE_DIM + ROPE_DIM)
SSM_GROUP = 16
SSM_GROUPS = 32
SSM_WIDTH = SSM_GROUP * SSM_GROUPS
SSM_STATE = 64
SSM_CH = SSM_GROUPS * SSM_STATE
PEER_HEADS = 8
PEER_KEYS = 128
PEER_QDIM = 256
PEER_TOPK = 16

FRONT = CHUNK
ROW_ALIGN = 512
VMEM_LIMIT = 60 * 1024 * 1024

C_AB, C_AC, C_AH, C_Q, C_KV, C_U, C_KPE, N_SMALL = 0, 512, 1024, 1536, 2048, 2560, 3072, 3200


def _cparams(*sem):
    return pltpu.CompilerParams(dimension_semantics=sem, vmem_limit_bytes=VMEM_LIMIT)


def _rms(x, g):
    return x * lax.rsqrt(jnp.mean(x * x, axis=-1, keepdims=True) + EPS) * g


def _sigmoid(x):
    return 1.0 / (1.0 + jnp.exp(-x))


def _gelu(x):
    return 0.5 * x * (1.0 + jnp.tanh(math.sqrt(2.0 / math.pi) * (x + 0.044715 * (x * x * x))))


def _dot(a, b):
    return jnp.dot(a, b, preferred_element_type=F32)


def _dot_nt(a, b):
    return lax.dot_general(a, b, (((1,), (1,)), ((), ())), preferred_element_type=F32)


def _in_proj_kernel(x_ref, g_ref, w_ref, o_ref):
    xn = _rms(x_ref[...], g_ref[...]).astype(BF16)
    o_ref[...] = _dot(xn, w_ref[...])


def _in_proj(x, g, w_small):
    R = x.shape[0]
    tb = 256
    return pl.pallas_call(
        _in_proj_kernel,
        grid=(R // tb,),
        in_specs=[pl.BlockSpec((tb, D_MODEL), lambda i: (i, 0)),
                  pl.BlockSpec((1, D_MODEL), lambda i: (0, 0)),
                  pl.BlockSpec((D_MODEL, N_SMALL), lambda i: (0, 0))],
        out_specs=pl.BlockSpec((tb, N_SMALL), lambda i: (i, 0)),
        out_shape=jax.ShapeDtypeStruct((R, N_SMALL), F32),
        compiler_params=_cparams("parallel"),
        name="in_proj",
    )(x, g, w_small)


def _conv_kernel(p_ref, w_ref, b_ref, st_ref, za_ref, pc_ref, sc_ref, zbuf, *, p_blk, p_off, n_blk, tb, dec_seq):
    i = pl.program_id(0)
    ab = p_ref[:, C_AB:C_AB + A_WIDTH]
    z = p_ref[:, C_AC:C_AC + A_WIDTH] * p_ref[:, C_AH:C_AH + A_WIDTH]
    w0, w1, w2 = w_ref[0:1, :], w_ref[1:2, :], w_ref[2:3, :]
    b = b_ref[...]

    @pl.when(i == 0)
    def _():
        zbuf[0:8, :] = jnp.zeros((8, A_WIDTH), F32)

    @pl.when(i < n_blk - 1)
    def _():
        zbuf[8:8 + tb, :] = z
        y = b + w0 * zbuf[6:6 + tb, :] + w1 * zbuf[7:7 + tb, :] + w2 * zbuf[8:8 + tb, :]
        za_ref[...] = (ab * y).astype(BF16)
        zbuf[0:8, :] = zbuf[tb:tb + 8, :]

    @pl.when(i == p_blk)
    def _():
        pc_ref[...] = z[p_off - 1:p_off + 1, :]

    @pl.when(i == n_blk - 1)
    def _():
        for s in range(tb // dec_seq):
            r0 = s * dec_seq
            zbuf[6:8, :] = st_ref[2 * s:2 * s + 2, :]
            zbuf[8:8 + dec_seq, :] = z[r0:r0 + dec_seq, :]
            y = (b + w0 * zbuf[6:6 + dec_seq, :] + w1 * zbuf[7:7 + dec_seq, :]
                 + w2 * zbuf[8:8 + dec_seq, :])
            za_ref[r0:r0 + dec_seq, :] = (ab[r0:r0 + dec_seq, :] * y).astype(BF16)
            sc_ref[2 * s:2 * s + 2, :] = z[r0 + dec_seq - 2:r0 + dec_seq, :]


def _conv(proj, conv_w, conv_b, state, lay):
    R = proj.shape[0]
    tb = lay["ns"]
    n_blk = R // tb
    last = lay["p_end"] - 1
    kern = functools.partial(_conv_kernel, p_blk=last // tb, p_off=last % tb, n_blk=n_blk, tb=tb,
                             dec_seq=lay["dec_seq"])
    assert last % tb >= 1
    nseq = tb // lay["dec_seq"]
    return pl.pallas_call(
        kern,
        grid=(n_blk,),
        in_specs=[pl.BlockSpec((tb, 3 * A_WIDTH), lambda i: (i, 0)),
                  pl.BlockSpec((CONV_WIDTH, A_WIDTH), lambda i: (0, 0)),
                  pl.BlockSpec((1, A_WIDTH), lambda i: (0, 0)),
                  pl.BlockSpec((2 * nseq, A_WIDTH), lambda i: (0, 0))],
        out_specs=[pl.BlockSpec((tb, A_WIDTH), lambda i: (i, 0)),
                   pl.BlockSpec((2, A_WIDTH), lambda i: (0, 0)),
                   pl.BlockSpec((2 * nseq, A_WIDTH), lambda i: (0, 0))],
        out_shape=[jax.ShapeDtypeStruct((R, A_WIDTH), BF16),
                   jax.ShapeDtypeStruct((2, A_WIDTH), F32),
                   jax.ShapeDtypeStruct((2 * nseq, A_WIDTH), F32)],
        scratch_shapes=[pltpu.VMEM((tb + 8, A_WIDTH), F32)],
        compiler_params=_cparams("arbitrary"),
        name="short_conv",
    )(proj, conv_w, conv_b, state)


QK_PAD = 256
Q_SCALE = ATTN_SCALE * math.log2(math.e)


V_ROWS = V_DIM + 16


def _qkv_kernel(cq_ref, ckv_ref, kp_ref, cs_ref, qn_ref, wq_ref, kn_ref, wk_ref, wvt_ref,
                q_ref, k_ref, vt_ref, lat_ref, kpe_ref, *, tb):
    cs = cs_ref[...]
    qc = _rms(cq_ref[...], qn_ref[...]).astype(BF16)
    q = _dot(qc, wq_ref[...])
    lat = _rms(ckv_ref[...], kn_ref[...])
    lat_ref[...] = lat
    lat_b = lat.astype(BF16)
    kn = _dot(lat_b, wk_ref[...])
    ones_row = jnp.where(lax.broadcasted_iota(jnp.int32, (V_ROWS - V_DIM, tb), 0) == 0, 1.0, 0.0).astype(BF16)
    lane = lax.broadcasted_iota(jnp.int32, (tb, 2 * ROPE_DIM), 1)
    row = pl.program_id(0) * tb + lax.broadcasted_iota(jnp.int32, (tb, 2 * ROPE_DIM), 0)
    q_tail = jnp.where(lane == ROPE_DIM, 1.0, 0.0)
    k_tail = jnp.where((lane == ROPE_DIM) & (row < FRONT - N_META), NEG_INF, 0.0)
    t = kp_ref[...] * cs
    kpe = t + pltpu.roll(t, ROPE_DIM, 1)
    kpe_ref[...] = kpe[:, 0:ROPE_DIM]
    k_hi = jnp.where(lane < ROPE_DIM, kpe, k_tail).astype(BF16)
    for h in range(MLA_HEADS):
        c = h * 256
        q_ref[h, :, 0:NOPE_DIM] = (q[:, c:c + NOPE_DIM] * Q_SCALE).astype(BF16)
        t = q[:, c + NOPE_DIM:c + 256] * cs
        qpe = (t + pltpu.roll(t, ROPE_DIM, 1)) * Q_SCALE
        q_ref[h, :, NOPE_DIM:QK_PAD] = jnp.where(lane < ROPE_DIM, qpe, q_tail).astype(BF16)
        k_ref[h, :, 0:NOPE_DIM] = kn[:, h * NOPE_DIM:(h + 1) * NOPE_DIM].astype(BF16)
        k_ref[h, :, NOPE_DIM:QK_PAD] = k_hi
        vt_ref[h, 0:V_DIM, :] = _dot_nt(wvt_ref[h], lat_b).astype(BF16)
        vt_ref[h, V_DIM:V_ROWS, :] = ones_row


def _qkv(proj, cs, q_norm, wq_aug, kv_norm, w_k, w_vt):
    R = proj.shape[0]
    tb = 256
    H = MLA_HEADS
    return pl.pallas_call(
        functools.partial(_qkv_kernel, tb=tb),
        grid=(R // tb,),
        in_specs=[pl.BlockSpec((tb, Q_LORA), lambda i: (i, C_Q // Q_LORA)),
                  pl.BlockSpec((tb, KV_LORA), lambda i: (i, C_KV // KV_LORA)),
                  pl.BlockSpec((tb, 2 * ROPE_DIM), lambda i: (i, C_KPE // (2 * ROPE_DIM))),
                  pl.BlockSpec((tb, 2 * ROPE_DIM), lambda i: (i, 0)),
                  pl.BlockSpec((1, Q_LORA), lambda i: (0, 0)),
                  pl.BlockSpec((Q_LORA, H * 256), lambda i: (0, 0)),
                  pl.BlockSpec((1, KV_LORA), lambda i: (0, 0)),
                  pl.BlockSpec((KV_LORA, H * NOPE_DIM), lambda i: (0, 0)),
                  pl.BlockSpec((H, V_DIM, KV_LORA), lambda i: (0, 0, 0))],
        out_specs=[pl.BlockSpec((H, tb, QK_PAD), lambda i: (0, i, 0)),
                   pl.BlockSpec((H, tb, QK_PAD), lambda i: (0, i, 0)),
                   pl.BlockSpec((H, V_ROWS, tb), lambda i: (0, 0, i)),
                   pl.BlockSpec((tb, KV_LORA), lambda i: (i, 0)),
                   pl.BlockSpec((tb, ROPE_DIM), lambda i: (i, 0))],
        out_shape=[jax.ShapeDtypeStruct((H, R, QK_PAD), BF16),
                   jax.ShapeDtypeStruct((H, R, QK_PAD), BF16),
                   jax.ShapeDtypeStruct((H, V_ROWS, R), BF16),
                   jax.ShapeDtypeStruct((R, KV_LORA), F32),
                   jax.ShapeDtypeStruct((R, ROPE_DIM), F32)],
        compiler_params=_cparams("parallel"),
        name="qkv_rope",
    )(proj, proj, proj, cs, q_norm, wq_aug, kv_norm, w_k, w_vt)


FLASH_TQ = 512
FLASH_TK = 512


def _flash_kernel(it_ref, jt_ref, q_ref, k_ref, vt_ref, o_ref, m_ref, acc_ref, *, bq):
    tq, tk = FLASH_TQ, FLASH_TK
    t = pl.program_id(1)
    i = it_ref[t]
    j = jt_ref[t]

    @pl.when(j == 0)
    def _():
        m_ref[...] = jnp.full(m_ref.shape, NEG_INF, F32)
        acc_ref[...] = jnp.zeros(acc_ref.shape, F32)

    def run(diag):
        tiles = [(a, b) for a in range(bq // tq) for b in range(bq // tk)
                 if not (diag and (b * tk) // CHUNK > (a * tq + tq - 1) // CHUNK)]

        def scores(a, b):
            return _dot_nt(k_ref[b * tk:(b + 1) * tk, :], q_ref[a * tq:(a + 1) * tq, :])

        tiles.sort(key=lambda ab: (ab[1], ab[0]))
        cols = {a: slice(a * tq, (a + 1) * tq) for a, _ in tiles}
        m = {a: m_ref[:, c] for a, c in cols.items()}
        acc = {a: acc_ref[:, c] for a, c in cols.items()}
        s_next = scores(*tiles[0])
        for n, (a, b) in enumerate(tiles):
            s = s_next
            if n + 1 < len(tiles):
                s_next = scores(*tiles[n + 1])
            if diag and (b * tk + tk - 1) // CHUNK > (a * tq) // CHUNK:
                kc = (b * tk + lax.broadcasted_iota(jnp.int32, (tk, tq), 0)) // CHUNK
                qc = (a * tq + lax.broadcasted_iota(jnp.int32, (tk, tq), 1)) // CHUNK
                s = jnp.where(kc <= qc, s, NEG_INF)
            m_new = jnp.maximum(m[a], jnp.max(s, axis=0, keepdims=True))
            alpha = jnp.exp2(m[a] - m_new)
            p = jnp.exp2(s - m_new).astype(BF16)
            acc[a] = alpha * acc[a] + _dot(vt_ref[:, b * tk:(b + 1) * tk], p)
            m[a] = m_new
        for a, c in cols.items():
            if diag:
                row = i * bq + a * tq + lax.broadcasted_iota(jnp.int32, (V_DIM, tq), 1)
                o = jnp.where(row >= FRONT - N_META, acc[a][0:V_DIM, :] / acc[a][V_DIM:V_DIM + 1, :], 0.0)
                o_ref[c, :] = o.T.astype(o_ref.dtype)
            else:
                m_ref[:, c], acc_ref[:, c] = m[a], acc[a]

    @pl.when(j < i)
    def _():
        run(False)

    @pl.when(j == i)
    def _():
        run(True)


def _flash(q, k, vt):
    H, R, _ = q.shape
    bq = next(b for b in (1536, 1024, 512) if R % b == 0)
    n = R // bq
    pairs = [(i, j) for i in range(n) for j in range(i + 1)]
    it = jnp.array([p[0] for p in pairs], jnp.int32)
    jt = jnp.array([p[1] for p in pairs], jnp.int32)
    grid_spec = pltpu.PrefetchScalarGridSpec(
        num_scalar_prefetch=2,
        grid=(H, len(pairs)),
        in_specs=[pl.BlockSpec((None, bq, QK_PAD), lambda h, t, it, jt: (h, it[t], 0)),
                  pl.BlockSpec((None, bq, QK_PAD), lambda h, t, it, jt: (h, jt[t], 0)),
                  pl.BlockSpec((None, V_ROWS, bq), lambda h, t, it, jt: (h, 0, jt[t]))],
        out_specs=pl.BlockSpec((bq, V_DIM), lambda h, t, it, jt: (it[t], h)),
        scratch_shapes=[pltpu.VMEM((1, bq), F32), pltpu.VMEM((V_ROWS, bq), F32)],
    )
    return pl.pallas_call(
        functools.partial(_flash_kernel, bq=bq),
        grid_spec=grid_spec,
        out_shape=jax.ShapeDtypeStruct((R, H * V_DIM), BF16),
        compiler_params=_cparams("parallel", "arbitrary"),
        name="prompt_attention",
    )(it, jt, q, k, vt)


def _cached_attn_kernel(q_ref, lat_ref, kpe_ref, w_ref, o_ref):
    kv = _dot(lat_ref[...], w_ref[...]).astype(BF16)
    kpe = kpe_ref[...]
    for h in range(MLA_HEADS):
        c = h * 256
        qh = q_ref[h]
        s = _dot_nt(qh[:, 0:NOPE_DIM], kv[:, c:c + NOPE_DIM]) + _dot_nt(qh[:, NOPE_DIM:QK_DIM], kpe)
        m = jnp.max(s, axis=1, keepdims=True)
        p = jnp.exp2(s - m)
        p = p / jnp.sum(p, axis=1, keepdims=True)
        o_ref[:, h * V_DIM:(h + 1) * V_DIM] = _dot(p.astype(BF16), kv[:, c + NOPE_DIM:c + 256]).astype(o_ref.dtype)


def _cached_attn(q, lat_all, kpe_all, w_ukv, lay):
    H = MLA_HEADS
    B, Lk, _ = lat_all.shape
    S = lay["dec_seq"]
    blk0 = lay["s0"] // S
    return pl.pallas_call(
        _cached_attn_kernel,
        grid=(B,),
        in_specs=[pl.BlockSpec((H, S, QK_PAD), lambda b: (0, blk0 + b, 0)),
                  pl.BlockSpec((None, Lk, KV_LORA), lambda b: (b, 0, 0)),
                  pl.BlockSpec((None, Lk, ROPE_DIM), lambda b: (b, 0, 0)),
                  pl.BlockSpec((KV_LORA, H * 256), lambda b: (0, 0))],
        out_specs=pl.BlockSpec((S, H * V_DIM), lambda b: (b, 0)),
        out_shape=jax.ShapeDtypeStruct((B * S, H * V_DIM), BF16),
        compiler_params=_cparams("parallel"),
        name="sample_attention",
    )(q, lat_all, kpe_all, w_ukv)


def _s5_kernel(u_ref, bm_ref, a_ref, cm_ref, d_ref, h0_ref, y_ref, ps_ref, ss_ref, hre, him, carry,
               *, p_blk, p_off, n_blk, tb, dec_seq):
    i = pl.program_id(0)
    u = u_ref[...]
    hb = _dot(u.astype(BF16), bm_ref[...])
    hre[...] = hb[:, 0:SSM_CH]
    him[...] = hb[:, SSM_CH:2 * SSM_CH]
    ar = a_ref[0:1, :]
    ai = a_ref[1:2, :]

    def scan(start, n, hr, hi):
        def body(t, c):
            hr, hi = c
            r = start + t
            nr = ar * hr - ai * hi + hre[pl.ds(r, 1), :]
            ni = ar * hi + ai * hr + him[pl.ds(r, 1), :]
            hre[pl.ds(r, 1), :] = nr
            him[pl.ds(r, 1), :] = ni
            return nr, ni
        return lax.fori_loop(0, n, body, (hr, hi), unroll=2)

    @pl.when(i == 0)
    def _():
        carry[...] = jnp.zeros(carry.shape, F32)

    @pl.when(i < n_blk - 1)
    def _():
        hr, hi = scan(0, tb, carry[0:1, :], carry[1:2, :])
        carry[0:1, :] = hr
        carry[1:2, :] = hi

    @pl.when(i == p_blk)
    def _():
        ps_ref[0:1, :] = hre[p_off:p_off + 1, :]
        ps_ref[1:2, :] = him[p_off:p_off + 1, :]

    @pl.when(i == n_blk - 1)
    def _():
        for s in range(tb // dec_seq):
            hr, hi = scan(s * dec_seq, dec_seq, h0_ref[0, s:s + 1, :], h0_ref[1, s:s + 1, :])
            ss_ref[0, s:s + 1, :] = hr
            ss_ref[1, s:s + 1, :] = hi

    y = _dot(hre[...].astype(BF16), cm_ref[0]) + _dot(him[...].astype(BF16), cm_ref[1])
    y_ref[...] = _gelu(y + d_ref[...] * u).astype(BF16)


def _s5(proj, bmat, abar, cmat, d, h0, lay):
    R = proj.shape[0]
    tb = lay["ns"]
    n_blk = R // tb
    last = lay["p_end"] - 1
    nseq = tb // lay["dec_seq"]
    kern = functools.partial(_s5_kernel, p_blk=last // tb, p_off=last % tb, n_blk=n_blk, tb=tb,
                             dec_seq=lay["dec_seq"])
    return pl.pallas_call(
        kern,
        grid=(n_blk,),
        in_specs=[pl.BlockSpec((tb, SSM_WIDTH), lambda i: (i, C_U // SSM_WIDTH)),
                  pl.BlockSpec((SSM_WIDTH, 2 * SSM_CH), lambda i: (0, 0)),
                  pl.BlockSpec((2, SSM_CH), lambda i: (0, 0)),
                  pl.BlockSpec((2, SSM_CH, SSM_WIDTH), lambda i: (0, 0, 0)),
                  pl.BlockSpec((1, SSM_WIDTH), lambda i: (0, 0)),
                  pl.BlockSpec((2, nseq, SSM_CH), lambda i: (0, 0, 0))],
        out_specs=[pl.BlockSpec((tb, SSM_WIDTH), lambda i: (i, 0)),
                   pl.BlockSpec((2, SSM_CH), lambda i: (0, 0)),
                   pl.BlockSpec((2, nseq, SSM_CH), lambda i: (0, 0, 0))],
        out_shape=[jax.ShapeDtypeStruct((R, SSM_WIDTH), BF16),
                   jax.ShapeDtypeStruct((2, SSM_CH), F32),
                   jax.ShapeDtypeStruct((2, nseq, SSM_CH), F32)],
        scratch_shapes=[pltpu.VMEM((tb, SSM_CH), F32), pltpu.VMEM((tb, SSM_CH), F32),
                        pltpu.VMEM((2, SSM_CH), F32)],
        compiler_params=_cparams("arbitrary"),
        name="s5_scan",
    )(proj, bmat, abar, cmat, d, h0)


MERGE_TN = 256
MERGE_NC = D_MODEL // MERGE_TN


def _merge_kernel(x_ref, g_ref, za_ref, o_ref, yc_ref, wg0_ref, wg1_ref, wg2_ref, b0_ref, b1_ref, b2_ref,
                  wa_ref, wb_ref, wga_ref, wgb_ref, wo_ref, out_ref, xn_s, mg_s):
    c = pl.program_id(1)

    @pl.when(c == 0)
    def _():
        xn_s[...] = _rms(x_ref[...], g_ref[...]).astype(BF16)

    xn = xn_s[...]
    g0 = _sigmoid(_dot(xn, wg0_ref[...]) + b0_ref[...])
    g1 = _sigmoid(_dot(xn, wg1_ref[...]) + b1_ref[...])
    g2 = _sigmoid(_dot(xn, wg2_ref[...]) + b2_ref[...])
    yc = yc_ref[...]
    y_a = _dot(za_ref[...], wa_ref[...])
    y_b = _dot(o_ref[...], wb_ref[...])
    y_c = _dot(yc, wga_ref[...]) * _sigmoid(_dot(yc, wgb_ref[...]))
    mg_s[c] = (g0 * y_a + g1 * y_b + g2 * y_c).astype(BF16)

    @pl.when(c == MERGE_NC - 1)
    def _():
        acc = x_ref[...]
        for cc in range(MERGE_NC):
            acc = acc + _dot(mg_s[cc], wo_ref[cc * MERGE_TN:(cc + 1) * MERGE_TN, :])
        out_ref[...] = acc


def _merge(x, g, za, o, yc, w_gate, b_gate, w_a, w_b, w_glu, w_o):
    R = x.shape[0]
    tb = 512
    tn, nc = MERGE_TN, MERGE_NC
    row = lambda w: pl.BlockSpec((tb, w), lambda i, c: (i, 0))
    col = lambda k, off: pl.BlockSpec((k, tn), lambda i, c: (0, off + c))
    return pl.pallas_call(
        _merge_kernel,
        grid=(R // tb, nc),
        in_specs=[row(D_MODEL), pl.BlockSpec((1, D_MODEL), lambda i, c: (0, 0)),
                  row(A_WIDTH), row(MLA_HEADS * V_DIM), row(SSM_WIDTH),
                  col(D_MODEL, 0), col(D_MODEL, nc), col(D_MODEL, 2 * nc),
                  col(1, 0), col(1, nc), col(1, 2 * nc),
                  col(A_WIDTH, 0), col(MLA_HEADS * V_DIM, 0), col(SSM_WIDTH, 0), col(SSM_WIDTH, nc),
                  pl.BlockSpec((D_MODEL, D_MODEL), lambda i, c: (0, 0))],
        out_specs=row(D_MODEL),
        out_shape=jax.ShapeDtypeStruct((R, D_MODEL), F32),
        scratch_shapes=[pltpu.VMEM((tb, D_MODEL), BF16), pltpu.VMEM((nc, tb, tn), BF16)],
        compiler_params=_cparams("parallel", "arbitrary"),
        name="branch_merge",
    )(x, g, za, o, yc, w_gate, w_gate, w_gate, b_gate, b_gate, b_gate, w_a, w_b, w_glu, w_glu, w_o)


SUBLANES = 8


def _sort_desc(v):
    n = len(v)
    v = list(v)
    k = 2
    while k <= n:
        j = k // 2
        while j >= 1:
            for i in range(n):
                l = i ^ j
                if l > i:
                    hi, lo = jnp.maximum(v[i], v[l]), jnp.minimum(v[i], v[l])
                    v[i], v[l] = (hi, lo) if (i & k) == 0 else (lo, hi)
            j //= 2
        k *= 2
    return v


def _merge_top(a, b):
    n = len(a)
    v = [jnp.maximum(a[i], b[n - 1 - i]) for i in range(n)]
    j = n // 2
    while j >= 1:
        for i in range(n):
            l = i ^ j
            if l > i:
                v[i], v[l] = jnp.maximum(v[i], v[l]), jnp.minimum(v[i], v[l])
        j //= 2
    return v


def _top_desc(tiles):
    v = _sort_desc(tiles)
    shift = SUBLANES // 2
    while shift >= 1:
        v = _merge_top(v, [pltpu.roll(x, shift, 0) for x in v])
        shift //= 2
    return v


def _pack_sublanes(rows, sub):
    out = rows[0]
    for r in range(1, len(rows)):
        out = jnp.where(sub == r, rows[r], out)
    return out


def _route_kernel(x_ref, g_ref, wq_ref, k1_ref, k2_ref, xn_ref, th_ref, s2_ref, e1_ref, e2_ref, *, tb):
    K, NK, S = PEER_TOPK, PEER_KEYS, SUBLANES
    xn = _rms(x_ref[...], g_ref[...]).astype(BF16)
    xn_ref[...] = xn
    q = _dot(xn, wq_ref[...]).astype(BF16)
    half = PEER_QDIM // 2
    sub = lax.broadcasted_iota(jnp.int32, (S, tb), 0)
    neg = jnp.full((S, tb), -jnp.inf, F32)
    for h in range(PEER_HEADS):
        c = h * PEER_QDIM
        s1 = _dot_nt(k1_ref[...], q[:, c:c + half])
        s2 = _dot_nt(k2_ref[...], q[:, c + half:c + PEER_QDIM])
        t1 = [s1[S * r:S * (r + 1), :] for r in range(NK // S)]
        t2 = [s2[S * r:S * (r + 1), :] for r in range(NK // S)]
        v1 = _top_desc(t1)
        v2 = _top_desc(t2)
        v2_lo, v2_hi = _pack_sublanes(v2[0:S], sub), _pack_sublanes(v2[S:K], sub)
        v1_hi = _pack_sublanes(v1[S:K], sub)
        cand = [v1[0] + v2_lo, v1[0] + v2_hi] + [v1[a] + v2_lo for a in range(1, S)] + [v1_hi + v2[0]]
        vals = _top_desc(cand + [neg] * (K - len(cand)))
        tau = vals[K - 1]
        z = 1.0 + jnp.exp(vals[1] - vals[0])
        for r in range(2, K):
            z = z + jnp.exp(vals[r] - vals[0])
        scale = 0.5 / z
        for r in range(NK // S):
            rows = slice(S * r, S * (r + 1))
            th = jnp.full((S, tb), jnp.inf, F32)
            for b in range(K):
                th = jnp.where(t1[r] + v2[b] >= tau, v2[b], th)
            th_ref[h, rows, :] = th
            e1_ref[h, rows, :] = jnp.exp(t1[r] - v1[0]) * scale
            e2_ref[h, rows, :] = jnp.exp(t2[r] - v2[0])
        s2_ref[h] = s2


def _route(x, g, wq, k1, k2):
    R = x.shape[0]
    tb = 128
    H, NK = PEER_HEADS, PEER_KEYS
    tmap = pl.BlockSpec((H, None, NK, tb), lambda i: (0, i, 0, 0))
    tshape = jax.ShapeDtypeStruct((H, R // tb, NK, tb), F32)
    return pl.pallas_call(
        functools.partial(_route_kernel, tb=tb),
        grid=(R // tb,),
        in_specs=[pl.BlockSpec((tb, D_MODEL), lambda i: (i, 0)),
                  pl.BlockSpec((1, D_MODEL), lambda i: (0, 0)),
                  pl.BlockSpec((D_MODEL, H * PEER_QDIM), lambda i: (0, 0)),
                  pl.BlockSpec((NK, PEER_QDIM // 2), lambda i: (0, 0)),
                  pl.BlockSpec((NK, PEER_QDIM // 2), lambda i: (0, 0))],
        out_specs=[pl.BlockSpec((tb, D_MODEL), lambda i: (i, 0)), tmap, tmap, tmap, tmap],
        out_shape=[jax.ShapeDtypeStruct((R, D_MODEL), BF16), tshape, tshape, tshape, tshape],
        compiler_params=_cparams("parallel"),
        name="peer_route",
    )(x, g, wq, k1, k2)


PEER_SUB = 8
PEER_EB = PEER_SUB * PEER_KEYS
PEER_TC = 128
PEER_OUT_ROWS = 256
PEER_ACT_ROWS = 256
GELU_C1 = math.sqrt(2.0 / math.pi)
GELU_C2 = GELU_C1 * 0.044715


def _expert_kernel(x_ref, xn_ref, th_ref, s2_ref, e1_ref, e2_ref, eu_ref, evt_ref, out_ref, acc_ref,
                   act_even, act_odd, p_even, p_odd, *, tb, ne):
    s = pl.program_id(0)
    NK = PEER_KEYS
    proj_first = (s < 2) | ((s - 2) % ne == 0)
    proj_last = (s >= 2) & ((s - 2) % ne == ne - 1)

    @pl.when(s == 0)
    def _():
        act_odd[...] = jnp.zeros(act_odd.shape, F32)
        p_odd[...] = jnp.zeros(p_odd.shape, BF16)

    @pl.when(proj_first)
    def _():
        acc_ref[...] = jnp.zeros(acc_ref.shape, F32)

    n_piece = PEER_SUB * (tb // PEER_TC)
    n_out = D_MODEL // PEER_OUT_ROWS

    n_act = PEER_EB // PEER_ACT_ROWS

    def gate_and_project(p_new, p_old, act_new, act_old):
        for piece in range(n_piece):
            if piece < n_piece // 2 and piece % (n_piece // 2 // n_out) == 0:
                c = piece // (n_piece // 2 // n_out)
                orow = slice(c * PEER_OUT_ROWS, (c + 1) * PEER_OUT_ROWS)
                acc_ref[orow, :] += _dot(evt_ref[orow, :], p_old[...])
            if piece >= n_piece // 2 and piece % (n_piece // 2 // n_act) == 0:
                c = (piece - n_piece // 2) // (n_piece // 2 // n_act)
                erow = slice(c * PEER_ACT_ROWS, (c + 1) * PEER_ACT_ROWS)
                act = _dot_nt(eu_ref[erow, :], xn_ref[...])
                for t in range(tb // PEER_TC):
                    act_new[t, erow, :] = act[:, t * PEER_TC:(t + 1) * PEER_TC]
            sub, tc = divmod(piece, tb // PEER_TC)
            rows = slice(sub * NK, (sub + 1) * NK)
            lanes = slice(tc * PEER_TC, (tc + 1) * PEER_TC)
            w = jnp.zeros((NK, PEER_TC), F32)
            for h in range(PEER_HEADS):
                sel = s2_ref[h, tc] >= th_ref[h, tc, sub:sub + 1, :]
                w = w + jnp.where(sel, e1_ref[h, tc, sub:sub + 1, :] * e2_ref[h, tc], 0.0)
            a = act_old[tc, rows, :]
            wa = w * a
            p_new[rows, lanes] = (wa + wa * jnp.tanh(a * (GELU_C1 + GELU_C2 * (a * a)))).astype(BF16)

    @pl.when(s % 2 == 0)
    def _():
        gate_and_project(p_even, p_odd, act_even, act_odd)

    @pl.when(s % 2 == 1)
    def _():
        gate_and_project(p_odd, p_even, act_odd, act_even)

    @pl.when(proj_last)
    def _():
        out_ref[...] = x_ref[...] + acc_ref[...].T


def _experts(x, xn, th, s2, e1, e2, e_u, e_vt):
    R = x.shape[0]
    tb = 512
    H, NK = PEER_HEADS, PEER_KEYS
    ne = e_u.shape[0] // PEER_EB
    total = (R // tb) * ne
    pair = lambda s, lag: jnp.clip(s - lag, 0, total - 1)
    nt = tb // PEER_TC
    tmap = pl.BlockSpec((H, nt, NK, PEER_TC), lambda s: (0, pair(s, 1) // ne, 0, 0))
    kmap = pl.BlockSpec((H, nt, PEER_SUB, PEER_TC), lambda s: (0, pair(s, 1) // ne, pair(s, 1) % ne, 0))
    kern = functools.partial(_expert_kernel, tb=tb, ne=ne)
    return pl.pallas_call(
        kern,
        grid=(total + 2,),
        in_specs=[pl.BlockSpec((tb, D_MODEL), lambda s: (pair(s, 2) // ne, 0)),
                  pl.BlockSpec((tb, D_MODEL), lambda s: (pair(s, 0) // ne, 0)),
                  kmap, tmap, kmap, tmap,
                  pl.BlockSpec((PEER_EB, D_MODEL), lambda s: (pair(s, 0) % ne, 0)),
                  pl.BlockSpec((D_MODEL, PEER_EB), lambda s: (0, pair(s, 2) % ne))],
        out_specs=pl.BlockSpec((tb, D_MODEL), lambda s: (pair(s, 2) // ne, 0)),
        out_shape=jax.ShapeDtypeStruct((R, D_MODEL), F32),
        scratch_shapes=[pltpu.VMEM((D_MODEL, tb), F32),
                        pltpu.VMEM((nt, PEER_EB, PEER_TC), F32), pltpu.VMEM((nt, PEER_EB, PEER_TC), F32),
                        pltpu.VMEM((PEER_EB, tb), BF16), pltpu.VMEM((PEER_EB, tb), BF16)],
        compiler_params=_cparams("arbitrary"),
        name="peer_experts",
    )(x, xn, th, s2, e1, e2, e_u, e_vt)


def _final_norm_kernel(x_ref, g_ref, o_ref):
    o_ref[...] = _rms(x_ref[...], g_ref[...])


def _final_norm(x, g):
    R = x.shape[0]
    tb = 512
    return pl.pallas_call(
        _final_norm_kernel,
        grid=(R // tb,),
        in_specs=[pl.BlockSpec((tb, D_MODEL), lambda i: (i, 0)), pl.BlockSpec((1, D_MODEL), lambda i: (0, 0))],
        out_specs=pl.BlockSpec((tb, D_MODEL), lambda i: (i, 0)),
        out_shape=jax.ShapeDtypeStruct((R, D_MODEL), F32),
        compiler_params=_cparams("parallel"),
        name="final_norm",
    )(x, g)


def _s5_params(a_re, a_im, b_re, b_im, c_re, c_im, log_dt):
    dt = jnp.exp(log_dt)[:, None]
    mag = jnp.exp(dt * a_re)
    abar_re = mag * jnp.cos(dt * a_im)
    abar_im = mag * jnp.sin(dt * a_im)
    nr, ni = abar_re - 1.0, abar_im
    den = a_re * a_re + a_im * a_im
    coef_re = (nr * a_re + ni * a_im) / den
    coef_im = (ni * a_re - nr * a_im) / den
    bbar_re = coef_re[..., None] * b_re - coef_im[..., None] * b_im
    bbar_im = coef_re[..., None] * b_im + coef_im[..., None] * b_re
    eye = jnp.eye(SSM_GROUPS, dtype=F32)
    expand_b = lambda b: jnp.einsum('gpc,gh->gchp', b, eye).reshape(SSM_WIDTH, SSM_CH)
    expand_c = lambda c: jnp.einsum('gcp,gh->gphc', c, eye).reshape(SSM_CH, SSM_WIDTH)
    bmat = jnp.concatenate([expand_b(bbar_re), expand_b(bbar_im)], axis=1).astype(BF16)
    cmat = jnp.stack([expand_c(c_re), -expand_c(c_im)]).astype(BF16)
    abar = jnp.stack([abar_re.reshape(SSM_CH), abar_im.reshape(SSM_CH)])
    return bmat, abar, cmat


def _swap_halves(w):
    half = ROPE_DIM // 2
    return jnp.concatenate([w[..., half:], w[..., :half]], axis=-1)


def _rope_table(pos):
    half = ROPE_DIM // 2
    inv = 1.0 / (ROPE_THETA ** (jnp.arange(half, dtype=F32) / half))
    ang = pos[:, None] * inv[None, :]
    cos, sin = jnp.cos(ang), jnp.sin(ang)
    return jnp.concatenate([cos, cos, -sin, sin], axis=1)


def kernel(x_prompt, x_sample, cache_ckv, cache_kpe, state_conv, state_ssm_re, state_ssm_im, meta_tokens, norm_mix, norm_ffn, w_in, b_gate, conv_w, conv_b, w_a_out, q_norm, w_uq, kv_norm, w_ukv, w_b_out, ssm_a_re, ssm_a_im, ssm_b_re, ssm_b_im, ssm_c_re, ssm_c_im, ssm_log_dt, ssm_d, w_glu, w_o, peer_wq, peer_k1, peer_k2, peer_u, peer_v, norm_final):
    B, seq, _ = x_prompt.shape
    assert B == 1
    nb, dec_seq, _ = x_sample.shape
    ns = nb * dec_seq
    past = cache_ckv.shape[2]
    depth = w_in.shape[0]
    p_end = FRONT + seq
    R = -(-(p_end + ns) // ROW_ALIGN) * ROW_ALIGN
    s0 = R - ns
    lay = dict(ns=ns, dec_seq=dec_seq, p_end=p_end, s0=s0)

    x = jnp.concatenate([
        jnp.zeros((FRONT - N_META, D_MODEL), F32), meta_tokens, x_prompt[0],
        jnp.zeros((s0 - p_end, D_MODEL), F32), x_sample.reshape(ns, D_MODEL)], axis=0)
    pos = jnp.concatenate([
        jnp.arange(s0, dtype=F32) - (FRONT - N_META),
        jnp.tile(past + jnp.arange(dec_seq, dtype=F32), nb)])
    cs = _rope_table(pos)

    outs = [[] for _ in range(10)]
    for l in range(depth):
        wl = w_in[l]
        kp = wl[:, 2560:2624]
        w_small = jnp.concatenate([wl[:, 0:2560], wl[:, 2624:3136], kp, _swap_halves(kp)], axis=1).astype(BF16)
        w_gate = wl[:, 3136:].astype(BF16)
        wq3 = w_uq[l].reshape(Q_LORA, MLA_HEADS, QK_DIM)
        wq_aug = jnp.concatenate([wq3, _swap_halves(wq3[..., NOPE_DIM:])], axis=-1).reshape(Q_LORA, MLA_HEADS * 256)
        wq_aug = wq_aug.astype(BF16)
        wkv = w_ukv[l].astype(BF16)
        wkv3 = wkv.reshape(KV_LORA, MLA_HEADS, NOPE_DIM + V_DIM)
        w_k = wkv3[..., :NOPE_DIM].reshape(KV_LORA, MLA_HEADS * NOPE_DIM)
        w_vt = wkv3[..., NOPE_DIM:].transpose(1, 2, 0)
        bmat, abar, cmat = _s5_params(ssm_a_re[l], ssm_a_im[l], ssm_b_re[l], ssm_b_im[l], ssm_c_re[l],
                                      ssm_c_im[l], ssm_log_dt[l])
        h0 = jnp.stack([state_ssm_re[l].reshape(nb, SSM_CH), state_ssm_im[l].reshape(nb, SSM_CH)])

        proj = _in_proj(x, norm_mix[l][None], w_small)
        za, p_conv, s_conv = _conv(proj, conv_w[l], conv_b[l][None], state_conv[l].reshape(2 * nb, A_WIDTH), lay)
        q, k, vt, lat, kpe = _qkv(proj, cs, q_norm[l][None], wq_aug, kv_norm[l][None], w_k, w_vt)
        o = _flash(q, k, vt)
        lat_all = jnp.concatenate([cache_ckv[l], lat[s0:].reshape(nb, dec_seq, KV_LORA)], axis=1).astype(BF16)
        kpe_all = jnp.concatenate([cache_kpe[l], kpe[s0:].reshape(nb, dec_seq, ROPE_DIM)], axis=1).astype(BF16)
        o_s = _cached_attn(q, lat_all, kpe_all, wkv, lay)
        o = lax.dynamic_update_slice(o, o_s, (s0, 0))
        yc, p_state, s_state = _s5(proj, bmat, abar, cmat, ssm_d[l][None], h0, lay)
        x = _merge(x, norm_mix[l][None], za, o, yc, w_gate, b_gate[l][None], w_a_out[l].astype(BF16),
                   w_b_out[l].astype(BF16), w_glu[l].astype(BF16), w_o[l].astype(BF16))
        xn2, th, s2, e1, e2 = _route(x, norm_ffn[l][None], peer_wq[l].astype(BF16),
                                     peer_k1[l].astype(BF16), peer_k2[l].astype(BF16))
        x = _experts(x, xn2, th, s2, e1, e2, peer_u[l].astype(BF16), peer_v[l].astype(BF16).T)

        lo = FRONT - N_META
        outs[0].append(lat[lo:p_end][None])
        outs[1].append(kpe[lo:p_end][None])
        outs[2].append(p_conv[None])
        outs[3].append(p_state[0].reshape(1, SSM_GROUPS, SSM_STATE))
        outs[4].append(p_state[1].reshape(1, SSM_GROUPS, SSM_STATE))
        outs[5].append(lat[s0:].reshape(nb, dec_seq, KV_LORA))
        outs[6].append(kpe[s0:].reshape(nb, dec_seq, ROPE_DIM))
        outs[7].append(s_conv.reshape(nb, CONV_WIDTH - 1, A_WIDTH))
        outs[8].append(s_state[0].reshape(nb, SSM_GROUPS, SSM_STATE))
        outs[9].append(s_state[1].reshape(nb, SSM_GROUPS, SSM_STATE))

    y = _final_norm(x, norm_final[None])
    y_prompt = y[FRONT:p_end][None]
    y_sample = y[s0:].reshape(nb, dec_seq, D_MODEL)
    return (y_prompt, y_sample) + tuple(jnp.stack(o) for o in outs)
```

```python
import functools
import math

import jax
import jax.numpy as jnp
from jax import lax
from jax.experimental import pallas as pl
from jax.experimental.pallas import tpu as pltpu

F32 = jnp.float32
BF16 = jnp.bfloat16

D_MODEL = 2048
CHUNK = 64
N_META = 16
EPS = 1e-6
NEG_INF = -1e30
A_WIDTH = 512
CONV_WIDTH = 3
MLA_HEADS = 8
Q_LORA = 512
KV_LORA = 512
NOPE_DIM = 128
ROPE_DIM = 64
V_DIM = 128
QK_DIM = NOPE_DIM + ROPE_DIM
ROPE_THETA = 10000.0
ATTN_SCALE = 1.0 / math.sqrt(NOPE_DIM + ROPE_DIM)
SSM_GROUP = 16
SSM_GROUPS = 32
SSM_WIDTH = SSM_GROUP * SSM_GROUPS
SSM_STATE = 64
SSM_CH = SSM_GROUPS * SSM_STATE
PEER_HEADS = 8
PEER_KEYS = 128
PEER_QDIM = 256
PEER_TOPK = 16

FRONT = CHUNK
ROW_ALIGN = 512
VMEM_LIMIT = 60 * 1024 * 1024

C_AB, C_AC, C_AH, C_Q, C_KV, C_U, C_KPE, N_SMALL = 0, 512, 1024, 1536, 2048, 2560, 3072, 3200


def _cparams(*sem):
    return pltpu.CompilerParams(dimension_semantics=sem, vmem_limit_bytes=VMEM_LIMIT)


def _rms(x, g):
    return x * lax.rsqrt(jnp.mean(x * x, axis=-1, keepdims=True) + EPS) * g


def _sigmoid(x):
    return 1.0 / (1.0 + jnp.exp(-x))


def _gelu(x):
    return 0.5 * x * (1.0 + jnp.tanh(math.sqrt(2.0 / math.pi) * (x + 0.044715 * (x * x * x))))


def _dot(a, b):
    return jnp.dot(a, b, preferred_element_type=F32)


def _dot_nt(a, b):
    return lax.dot_general(a, b, (((1,), (1,)), ((), ())), preferred_element_type=F32)


def _in_proj_kernel(x_ref, g_ref, w_ref, o_ref):
    xn = _rms(x_ref[...], g_ref[...]).astype(BF16)
    o_ref[...] = _dot(xn, w_ref[...])


def _in_proj(x, g, w_small):
    R = x.shape[0]
    tb = 256
    return pl.pallas_call(
        _in_proj_kernel,
        grid=(R // tb,),
        in_specs=[pl.BlockSpec((tb, D_MODEL), lambda i: (i, 0)),
                  pl.BlockSpec((1, D_MODEL), lambda i: (0, 0)),
                  pl.BlockSpec((D_MODEL, N_SMALL), lambda i: (0, 0))],
        out_specs=pl.BlockSpec((tb, N_SMALL), lambda i: (i, 0)),
        out_shape=jax.ShapeDtypeStruct((R, N_SMALL), F32),
        compiler_params=_cparams("parallel"),
        name="in_proj",
    )(x, g, w_small)


def _conv_kernel(p_ref, w_ref, b_ref, st_ref, za_ref, pc_ref, sc_ref, zbuf, *, p_blk, p_off, n_blk, tb, dec_seq):
    i = pl.program_id(0)
    ab = p_ref[:, C_AB:C_AB + A_WIDTH]
    z = p_ref[:, C_AC:C_AC + A_WIDTH] * p_ref[:, C_AH:C_AH + A_WIDTH]
    w0, w1, w2 = w_ref[0:1, :], w_ref[1:2, :], w_ref[2:3, :]
    b = b_ref[...]

    @pl.when(i == 0)
    def _():
        zbuf[0:8, :] = jnp.zeros((8, A_WIDTH), F32)

    @pl.when(i < n_blk - 1)
    def _():
        zbuf[8:8 + tb, :] = z
        y = b + w0 * zbuf[6:6 + tb, :] + w1 * zbuf[7:7 + tb, :] + w2 * zbuf[8:8 + tb, :]
        za_ref[...] = (ab * y).astype(BF16)
        zbuf[0:8, :] = zbuf[tb:tb + 8, :]

    @pl.when(i == p_blk)
    def _():
        pc_ref[...] = z[p_off - 1:p_off + 1, :]

    @pl.when(i == n_blk - 1)
    def _():
        for s in range(tb // dec_seq):
            r0 = s * dec_seq
            zbuf[6:8, :] = st_ref[2 * s:2 * s + 2, :]
            zbuf[8:8 + dec_seq, :] = z[r0:r0 + dec_seq, :]
            y = (b + w0 * zbuf[6:6 + dec_seq, :] + w1 * zbuf[7:7 + dec_seq, :]
                 + w2 * zbuf[8:8 + dec_seq, :])
            za_ref[r0:r0 + dec_seq, :] = (ab[r0:r0 + dec_seq, :] * y).astype(BF16)
            sc_ref[2 * s:2 * s + 2, :] = z[r0 + dec_seq - 2:r0 + dec_seq, :]


def _conv(proj, conv_w, conv_b, state, lay):
    R = proj.shape[0]
    tb = lay["ns"]
    n_blk = R // tb
    last = lay["p_end"] - 1
    kern = functools.partial(_conv_kernel, p_blk=last // tb, p_off=last % tb, n_blk=n_blk, tb=tb,
                             dec_seq=lay["dec_seq"])
    assert last % tb >= 1
    nseq = tb // lay["dec_seq"]
    return pl.pallas_call(
        kern,
        grid=(n_blk,),
        in_specs=[pl.BlockSpec((tb, 3 * A_WIDTH), lambda i: (i, 0)),
                  pl.BlockSpec((CONV_WIDTH, A_WIDTH), lambda i: (0, 0)),
                  pl.BlockSpec((1, A_WIDTH), lambda i: (0, 0)),
                  pl.BlockSpec((2 * nseq, A_WIDTH), lambda i: (0, 0))],
        out_specs=[pl.BlockSpec((tb, A_WIDTH), lambda i: (i, 0)),
                   pl.BlockSpec((2, A_WIDTH), lambda i: (0, 0)),
                   pl.BlockSpec((2 * nseq, A_WIDTH), lambda i: (0, 0))],
        out_shape=[jax.ShapeDtypeStruct((R, A_WIDTH), BF16),
                   jax.ShapeDtypeStruct((2, A_WIDTH), F32),
                   jax.ShapeDtypeStruct((2 * nseq, A_WIDTH), F32)],
        scratch_shapes=[pltpu.VMEM((tb + 8, A_WIDTH), F32)],
        compiler_params=_cparams("arbitrary"),
        name="short_conv",
    )(proj, conv_w, conv_b, state)


QK_PAD = 256
Q_SCALE = ATTN_SCALE * math.log2(math.e)


V_ROWS = V_DIM + 16


def _qkv_kernel(cq_ref, ckv_ref, kp_ref, cs_ref, qn_ref, wq_ref, kn_ref, wk_ref, wvt_ref,
                q_ref, k_ref, vt_ref, lat_ref, kpe_ref, *, tb):
    cs = cs_ref[...]
    qc = _rms(cq_ref[...], qn_ref[...]).astype(BF16)
    q = _dot(qc, wq_ref[...])
    lat = _rms(ckv_ref[...], kn_ref[...])
    lat_ref[...] = lat
    lat_b = lat.astype(BF16)
    kn = _dot(lat_b, wk_ref[...])
    ones_row = jnp.where(lax.broadcasted_iota(jnp.int32, (V_ROWS - V_DIM, tb), 0) == 0, 1.0, 0.0).astype(BF16)
    lane = lax.broadcasted_iota(jnp.int32, (tb, 2 * ROPE_DIM), 1)
    row = pl.program_id(0) * tb + lax.broadcasted_iota(jnp.int32, (tb, 2 * ROPE_DIM), 0)
    q_tail = jnp.where(lane == ROPE_DIM, 1.0, 0.0)
    k_tail = jnp.where((lane == ROPE_DIM) & (row < FRONT - N_META), NEG_INF, 0.0)
    t = kp_ref[...] * cs
    kpe = t + pltpu.roll(t, ROPE_DIM, 1)
    kpe_ref[...] = kpe[:, 0:ROPE_DIM]
    k_hi = jnp.where(lane < ROPE_DIM, kpe, k_tail).astype(BF16)
    for h in range(MLA_HEADS):
        c = h * 256
        q_ref[h, :, 0:NOPE_DIM] = (q[:, c:c + NOPE_DIM] * Q_SCALE).astype(BF16)
        t = q[:, c + NOPE_DIM:c + 256] * cs
        qpe = (t + pltpu.roll(t, ROPE_DIM, 1)) * Q_SCALE
        q_ref[h, :, NOPE_DIM:QK_PAD] = jnp.where(lane < ROPE_DIM, qpe, q_tail).astype(BF16)
        k_ref[h, :, 0:NOPE_DIM] = kn[:, h * NOPE_DIM:(h + 1) * NOPE_DIM].astype(BF16)
        k_ref[h, :, NOPE_DIM:QK_PAD] = k_hi
        vt_ref[h, 0:V_DIM, :] = _dot_nt(wvt_ref[h], lat_b).astype(BF16)
        vt_ref[h, V_DIM:V_ROWS, :] = ones_row


def _qkv(proj, cs, q_norm, wq_aug, kv_norm, w_k, w_vt):
    R = proj.shape[0]
    tb = 256
    H = MLA_HEADS
    return pl.pallas_call(
        functools.partial(_qkv_kernel, tb=tb),
        grid=(R // tb,),
        in_specs=[pl.BlockSpec((tb, Q_LORA), lambda i: (i, C_Q // Q_LORA)),
                  pl.BlockSpec((tb, KV_LORA), lambda i: (i, C_KV // KV_LORA)),
                  pl.BlockSpec((tb, 2 * ROPE_DIM), lambda i: (i, C_KPE // (2 * ROPE_DIM))),
                  pl.BlockSpec((tb, 2 * ROPE_DIM), lambda i: (i, 0)),
                  pl.BlockSpec((1, Q_LORA), lambda i: (0, 0)),
                  pl.BlockSpec((Q_LORA, H * 256), lambda i: (0, 0)),
                  pl.BlockSpec((1, KV_LORA), lambda i: (0, 0)),
                  pl.BlockSpec((KV_LORA, H * NOPE_DIM), lambda i: (0, 0)),
                  pl.BlockSpec((H, V_DIM, KV_LORA), lambda i: (0, 0, 0))],
        out_specs=[pl.BlockSpec((H, tb, QK_PAD), lambda i: (0, i, 0)),
                   pl.BlockSpec((H, tb, QK_PAD), lambda i: (0, i, 0)),
                   pl.BlockSpec((H, V_ROWS, tb), lambda i: (0, 0, i)),
                   pl.BlockSpec((tb, KV_LORA), lambda i: (i, 0)),
                   pl.BlockSpec((tb, ROPE_DIM), lambda i: (i, 0))],
        out_shape=[jax.ShapeDtypeStruct((H, R, QK_PAD), BF16),
                   jax.ShapeDtypeStruct((H, R, QK_PAD), BF16),
                   jax.ShapeDtypeStruct((H, V_ROWS, R), BF16),
                   jax.ShapeDtypeStruct((R, KV_LORA), F32),
                   jax.ShapeDtypeStruct((R, ROPE_DIM), F32)],
        compiler_params=_cparams("parallel"),
        name="qkv_rope",
    )(proj, proj, proj, cs, q_norm, wq_aug, kv_norm, w_k, w_vt)


FLASH_TQ = 512
FLASH_TK = 512


def _flash_kernel(it_ref, jt_ref, q_ref, k_ref, vt_ref, o_ref, m_ref, acc_ref, *, bq):
    tq, tk = FLASH_TQ, FLASH_TK
    t = pl.program_id(1)
    i = it_ref[t]
    j = jt_ref[t]

    @pl.when(j == 0)
    def _():
        m_ref[...] = jnp.full(m_ref.shape, NEG_INF, F32)
        acc_ref[...] = jnp.zeros(acc_ref.shape, F32)

    def run(diag):
        tiles = [(a, b) for a in range(bq // tq) for b in range(bq // tk)
                 if not (diag and (b * tk) // CHUNK > (a * tq + tq - 1) // CHUNK)]

        def scores(a, b):
            return _dot_nt(k_ref[b * tk:(b + 1) * tk, :], q_ref[a * tq:(a + 1) * tq, :])

        tiles.sort(key=lambda ab: (ab[1], ab[0]))
        cols = {a: slice(a * tq, (a + 1) * tq) for a, _ in tiles}
        m = {a: m_ref[:, c] for a, c in cols.items()}
        acc = {a: acc_ref[:, c] for a, c in cols.items()}
        s_next = scores(*tiles[0])
        for n, (a, b) in enumerate(tiles):
            s = s_next
            if n + 1 < len(tiles):
                s_next = scores(*tiles[n + 1])
            if diag and (b * tk + tk - 1) // CHUNK > (a * tq) // CHUNK:
                kc = (b * tk + lax.broadcasted_iota(jnp.int32, (tk, tq), 0)) // CHUNK
                qc = (a * tq + lax.broadcasted_iota(jnp.int32, (tk, tq), 1)) // CHUNK
                s = jnp.where(kc <= qc, s, NEG_INF)
            m_new = jnp.maximum(m[a], jnp.max(s, axis=0, keepdims=True))
            alpha = jnp.exp2(m[a] - m_new)
            p = jnp.exp2(s - m_new).astype(BF16)
            acc[a] = alpha * acc[a] + _dot(vt_ref[:, b * tk:(b + 1) * tk], p)
            m[a] = m_new
        for a, c in cols.items():
            if diag:
                row = i * bq + a * tq + lax.broadcasted_iota(jnp.int32, (V_DIM, tq), 1)
                o = jnp.where(row >= FRONT - N_META, acc[a][0:V_DIM, :] / acc[a][V_DIM:V_DIM + 1, :], 0.0)
                o_ref[c, :] = o.T.astype(o_ref.dtype)
            else:
                m_ref[:, c], acc_ref[:, c] = m[a], acc[a]

    @pl.when(j < i)
    def _():
        run(False)

    @pl.when(j == i)
    def _():
        run(True)


def _flash(q, k, vt):
    H, R, _ = q.shape
    bq = next(b for b in (1536, 1024, 512) if R % b == 0)
    n = R // bq
    pairs = [(i, j) for i in range(n) for j in range(i + 1)]
    it = jnp.array([p[0] for p in pairs], jnp.int32)
    jt = jnp.array([p[1] for p in pairs], jnp.int32)
    grid_spec = pltpu.PrefetchScalarGridSpec(
        num_scalar_prefetch=2,
        grid=(H, len(pairs)),
        in_specs=[pl.BlockSpec((None, bq, QK_PAD), lambda h, t, it, jt: (h, it[t], 0)),
                  pl.BlockSpec((None, bq, QK_PAD), lambda h, t, it, jt: (h, jt[t], 0)),
                  pl.BlockSpec((None, V_ROWS, bq), lambda h, t, it, jt: (h, 0, jt[t]))],
        out_specs=pl.BlockSpec((bq, V_DIM), lambda h, t, it, jt: (it[t], h)),
        scratch_shapes=[pltpu.VMEM((1, bq), F32), pltpu.VMEM((V_ROWS, bq), F32)],
    )
    return pl.pallas_call(
        functools.partial(_flash_kernel, bq=bq),
        grid_spec=grid_spec,
        out_shape=jax.ShapeDtypeStruct((R, H * V_DIM), BF16),
        compiler_params=_cparams("parallel", "arbitrary"),
        name="prompt_attention",
    )(it, jt, q, k, vt)


def _cached_attn_kernel(q_ref, lat_ref, kpe_ref, w_ref, o_ref):
    kv = _dot(lat_ref[...], w_ref[...]).astype(BF16)
    kpe = kpe_ref[...]
    for h in range(MLA_HEADS):
        c = h * 256
        qh = q_ref[h]
        s = _dot_nt(qh[:, 0:NOPE_DIM], kv[:, c:c + NOPE_DIM]) + _dot_nt(qh[:, NOPE_DIM:QK_DIM], kpe)
        m = jnp.max(s, axis=1, keepdims=True)
        p = jnp.exp2(s - m)
        p = p / jnp.sum(p, axis=1, keepdims=True)
        o_ref[:, h * V_DIM:(h + 1) * V_DIM] = _dot(p.astype(BF16), kv[:, c + NOPE_DIM:c + 256]).astype(o_ref.dtype)


def _cached_attn(q, lat_all, kpe_all, w_ukv, lay):
    H = MLA_HEADS
    B, Lk, _ = lat_all.shape
    S = lay["dec_seq"]
    blk0 = lay["s0"] // S
    return pl.pallas_call(
        _cached_attn_kernel,
        grid=(B,),
        in_specs=[pl.BlockSpec((H, S, QK_PAD), lambda b: (0, blk0 + b, 0)),
                  pl.BlockSpec((None, Lk, KV_LORA), lambda b: (b, 0, 0)),
                  pl.BlockSpec((None, Lk, ROPE_DIM), lambda b: (b, 0, 0)),
                  pl.BlockSpec((KV_LORA, H * 256), lambda b: (0, 0))],
        out_specs=pl.BlockSpec((S, H * V_DIM), lambda b: (b, 0)),
        out_shape=jax.ShapeDtypeStruct((B * S, H * V_DIM), BF16),
        compiler_params=_cparams("parallel"),
        name="sample_attention",
    )(q, lat_all, kpe_all, w_ukv)


def _s5_kernel(u_ref, bm_ref, a_ref, cm_ref, d_ref, h0_ref, y_ref, ps_ref, ss_ref, hre, him, carry,
               *, p_blk, p_off, n_blk, tb, dec_seq):
    i = pl.program_id(0)
    u = u_ref[...]
    hb = _dot(u.astype(BF16), bm_ref[...])
    hre[...] = hb[:, 0:SSM_CH]
    him[...] = hb[:, SSM_CH:2 * SSM_CH]
    ar = a_ref[0:1, :]
    ai = a_ref[1:2, :]

    def scan(start, n, hr, hi):
        def body(t, c):
            hr, hi = c
            r = start + t
            nr = ar * hr - ai * hi + hre[pl.ds(r, 1), :]
            ni = ar * hi + ai * hr + him[pl.ds(r, 1), :]
            hre[pl.ds(r, 1), :] = nr
            him[pl.ds(r, 1), :] = ni
            return nr, ni
        return lax.fori_loop(0, n, body, (hr, hi), unroll=2)

    @pl.when(i == 0)
    def _():
        carry[...] = jnp.zeros(carry.shape, F32)

    @pl.when(i < n_blk - 1)
    def _():
        hr, hi = scan(0, tb, carry[0:1, :], carry[1:2, :])
        carry[0:1, :] = hr
        carry[1:2, :] = hi

    @pl.when(i == p_blk)
    def _():
        ps_ref[0:1, :] = hre[p_off:p_off + 1, :]
        ps_ref[1:2, :] = him[p_off:p_off + 1, :]

    @pl.when(i == n_blk - 1)
    def _():
        for s in range(tb // dec_seq):
            hr, hi = scan(s * dec_seq, dec_seq, h0_ref[0, s:s + 1, :], h0_ref[1, s:s + 1, :])
            ss_ref[0, s:s + 1, :] = hr
            ss_ref[1, s:s + 1, :] = hi

    y = _dot(hre[...].astype(BF16), cm_ref[0]) + _dot(him[...].astype(BF16), cm_ref[1])
    y_ref[...] = _gelu(y + d_ref[...] * u).astype(BF16)


def _s5(proj, bmat, abar, cmat, d, h0, lay):
    R = proj.shape[0]
    tb = lay["ns"]
    n_blk = R // tb
    last = lay["p_end"] - 1
    nseq = tb // lay["dec_seq"]
    kern = functools.partial(_s5_kernel, p_blk=last // tb, p_off=last % tb, n_blk=n_blk, tb=tb,
                             dec_seq=lay["dec_seq"])
    return pl.pallas_call(
        kern,
        grid=(n_blk,),
        in_specs=[pl.BlockSpec((tb, SSM_WIDTH), lambda i: (i, C_U // SSM_WIDTH)),
                  pl.BlockSpec((SSM_WIDTH, 2 * SSM_CH), lambda i: (0, 0)),
                  pl.BlockSpec((2, SSM_CH), lambda i: (0, 0)),
                  pl.BlockSpec((2, SSM_CH, SSM_WIDTH), lambda i: (0, 0, 0)),
                  pl.BlockSpec((1, SSM_WIDTH), lambda i: (0, 0)),
                  pl.BlockSpec((2, nseq, SSM_CH), lambda i: (0, 0, 0))],
        out_specs=[pl.BlockSpec((tb, SSM_WIDTH), lambda i: (i, 0)),
                   pl.BlockSpec((2, SSM_CH), lambda i: (0, 0)),
                   pl.BlockSpec((2, nseq, SSM_CH), lambda i: (0, 0, 0))],
        out_shape=[jax.ShapeDtypeStruct((R, SSM_WIDTH), BF16),
                   jax.ShapeDtypeStruct((2, SSM_CH), F32),
                   jax.ShapeDtypeStruct((2, nseq, SSM_CH), F32)],
        scratch_shapes=[pltpu.VMEM((tb, SSM_CH), F32), pltpu.VMEM((tb, SSM_CH), F32),
                        pltpu.VMEM((2, SSM_CH), F32)],
        compiler_params=_cparams("arbitrary"),
        name="s5_scan",
    )(proj, bmat, abar, cmat, d, h0)


MERGE_TN = 256
MERGE_NC = D_MODEL // MERGE_TN


def _merge_kernel(x_ref, g_ref, za_ref, o_ref, yc_ref, wg0_ref, wg1_ref, wg2_ref, b0_ref, b1_ref, b2_ref,
                  wa_ref, wb_ref, wga_ref, wgb_ref, wo_ref, out_ref, xn_s, mg_s):
    c = pl.program_id(1)

    @pl.when(c == 0)
    def _():
        xn_s[...] = _rms(x_ref[...], g_ref[...]).astype(BF16)

    xn = xn_s[...]
    g0 = _sigmoid(_dot(xn, wg0_ref[...]) + b0_ref[...])
    g1 = _sigmoid(_dot(xn, wg1_ref[...]) + b1_ref[...])
    g2 = _sigmoid(_dot(xn, wg2_ref[...]) + b2_ref[...])
    yc = yc_ref[...]
    y_a = _dot(za_ref[...], wa_ref[...])
    y_b = _dot(o_ref[...], wb_ref[...])
    y_c = _dot(yc, wga_ref[...]) * _sigmoid(_dot(yc, wgb_ref[...]))
    mg_s[c] = (g0 * y_a + g1 * y_b + g2 * y_c).astype(BF16)

    @pl.when(c == MERGE_NC - 1)
    def _():
        acc = x_ref[...]
        for cc in range(MERGE_NC):
            acc = acc + _dot(mg_s[cc], wo_ref[cc * MERGE_TN:(cc + 1) * MERGE_TN, :])
        out_ref[...] = acc


def _merge(x, g, za, o, yc, w_gate, b_gate, w_a, w_b, w_glu, w_o):
    R = x.shape[0]
    tb = 512
    tn, nc = MERGE_TN, MERGE_NC
    row = lambda w: pl.BlockSpec((tb, w), lambda i, c: (i, 0))
    col = lambda k, off: pl.BlockSpec((k, tn), lambda i, c: (0, off + c))
    return pl.pallas_call(
        _merge_kernel,
        grid=(R // tb, nc),
        in_specs=[row(D_MODEL), pl.BlockSpec((1, D_MODEL), lambda i, c: (0, 0)),
                  row(A_WIDTH), row(MLA_HEADS * V_DIM), row(SSM_WIDTH),
                  col(D_MODEL, 0), col(D_MODEL, nc), col(D_MODEL, 2 * nc),
                  col(1, 0), col(1, nc), col(1, 2 * nc),
                  col(A_WIDTH, 0), col(MLA_HEADS * V_DIM, 0), col(SSM_WIDTH, 0), col(SSM_WIDTH, nc),
                  pl.BlockSpec((D_MODEL, D_MODEL), lambda i, c: (0, 0))],
        out_specs=row(D_MODEL),
        out_shape=jax.ShapeDtypeStruct((R, D_MODEL), F32),
        scratch_shapes=[pltpu.VMEM((tb, D_MODEL), BF16), pltpu.VMEM((nc, tb, tn), BF16)],
        compiler_params=_cparams("parallel", "arbitrary"),
        name="branch_merge",
    )(x, g, za, o, yc, w_gate, w_gate, w_gate, b_gate, b_gate, b_gate, w_a, w_b, w_glu, w_glu, w_o)


SUBLANES = 8


def _sort_desc(v):
    n = len(v)
    v = list(v)
    k = 2
    while k <= n:
        j = k // 2
        while j >= 1:
            for i in range(n):
                l = i ^ j
                if l > i:
                    hi, lo = jnp.maximum(v[i], v[l]), jnp.minimum(v[i], v[l])
                    v[i], v[l] = (hi, lo) if (i & k) == 0 else (lo, hi)
            j //= 2
        k *= 2
    return v


def _merge_top(a, b):
    n = len(a)
    v = [jnp.maximum(a[i], b[n - 1 - i]) for i in range(n)]
    j = n // 2
    while j >= 1:
        for i in range(n):
            l = i ^ j
            if l > i:
                v[i], v[l] = jnp.maximum(v[i], v[l]), jnp.minimum(v[i], v[l])
        j //= 2
    return v


def _top_desc(tiles):
    v = _sort_desc(tiles)
    shift = SUBLANES // 2
    while shift >= 1:
        v = _merge_top(v, [pltpu.roll(x, shift, 0) for x in v])
        shift //= 2
    return v


def _pack_sublanes(rows, sub):
    out = rows[0]
    for r in range(1, len(rows)):
        out = jnp.where(sub == r, rows[r], out)
    return out


def _route_kernel(x_ref, g_ref, wq_ref, k1_ref, k2_ref, xnt_ref, th_ref, s2_ref, e1_ref, e2_ref, *, tb):
    K, NK, S = PEER_TOPK, PEER_KEYS, SUBLANES
    xn32 = _rms(x_ref[...], g_ref[...])
    xnt_ref[...] = xn32.T.astype(BF16)
    xn = xn32.astype(BF16)
    q = _dot(xn, wq_ref[...]).astype(BF16)
    half = PEER_QDIM // 2
    sub = lax.broadcasted_iota(jnp.int32, (S, tb), 0)
    neg = jnp.full((S, tb), -jnp.inf, F32)
    for h in range(PEER_HEADS):
        c = h * PEER_QDIM
        s1 = _dot_nt(k1_ref[...], q[:, c:c + half])
        s2 = _dot_nt(k2_ref[...], q[:, c + half:c + PEER_QDIM])
        t1 = [s1[S * r:S * (r + 1), :] for r in range(NK // S)]
        t2 = [s2[S * r:S * (r + 1), :] for r in range(NK // S)]
        v1 = _top_desc(t1)
        v2 = _top_desc(t2)
        v2_lo, v2_hi = _pack_sublanes(v2[0:S], sub), _pack_sublanes(v2[S:K], sub)
        v1_hi = _pack_sublanes(v1[S:K], sub)
        cand = [v1[0] + v2_lo, v1[0] + v2_hi] + [v1[a] + v2_lo for a in range(1, S)] + [v1_hi + v2[0]]
        vals = _top_desc(cand + [neg] * (K - len(cand)))
        tau = vals[K - 1]
        z = 1.0 + jnp.exp(vals[1] - vals[0])
        for r in range(2, K):
            z = z + jnp.exp(vals[r] - vals[0])
        scale = 0.5 / z
        for r in range(NK // S):
            rows = slice(S * r, S * (r + 1))
            th = jnp.full((S, tb), jnp.inf, F32)
            for b in range(K):
                th = jnp.where(t1[r] + v2[b] >= tau, v2[b], th)
            th_ref[h, rows, :] = th
            e1_ref[h, rows, :] = jnp.exp(t1[r] - v1[0]) * scale
            e2_ref[h, rows, :] = jnp.exp(t2[r] - v2[0])
        s2_ref[h] = s2


def _route(x, g, wq, k1, k2):
    R = x.shape[0]
    tb = 128
    H, NK = PEER_HEADS, PEER_KEYS
    tmap = pl.BlockSpec((H, None, NK, tb), lambda i: (0, i, 0, 0))
    tshape = jax.ShapeDtypeStruct((H, R // tb, NK, tb), F32)
    return pl.pallas_call(
        functools.partial(_route_kernel, tb=tb),
        grid=(R // tb,),
        in_specs=[pl.BlockSpec((tb, D_MODEL), lambda i: (i, 0)),
                  pl.BlockSpec((1, D_MODEL), lambda i: (0, 0)),
                  pl.BlockSpec((D_MODEL, H * PEER_QDIM), lambda i: (0, 0)),
                  pl.BlockSpec((NK, PEER_QDIM // 2), lambda i: (0, 0)),
                  pl.BlockSpec((NK, PEER_QDIM // 2), lambda i: (0, 0))],
        out_specs=[pl.BlockSpec((D_MODEL, tb), lambda i: (0, i)), tmap, tmap, tmap, tmap],
        out_shape=[jax.ShapeDtypeStruct((D_MODEL, R), BF16), tshape, tshape, tshape, tshape],
        compiler_params=_cparams("parallel"),
        name="peer_route",
    )(x, g, wq, k1, k2)


PEER_SUB = 8
PEER_EB = PEER_SUB * PEER_KEYS
PEER_TC = 128
PEER_OUT_ROWS = 256
PEER_ACT_ROWS = 128
GELU_C1 = math.sqrt(2.0 / math.pi)
GELU_C2 = GELU_C1 * 0.044715


def _expert_kernel(x_ref, xnt_ref, th_ref, s2_ref, e1_ref, e2_ref, eu_ref, evt_ref, out_ref, acc_ref,
                   act_even, act_odd, p_even, p_odd, *, tb, ne):
    s = pl.program_id(0)
    NK = PEER_KEYS
    proj_first = (s < 2) | ((s - 2) % ne == 0)
    proj_last = (s >= 2) & ((s - 2) % ne == ne - 1)

    @pl.when(s == 0)
    def _():
        act_odd[...] = jnp.zeros(act_odd.shape, F32)
        p_odd[...] = jnp.zeros(p_odd.shape, BF16)

    @pl.when(proj_first)
    def _():
        acc_ref[...] = jnp.zeros(acc_ref.shape, F32)

    n_piece = PEER_SUB * (tb // PEER_TC)
    n_out = D_MODEL // PEER_OUT_ROWS

    n_act = PEER_EB // PEER_ACT_ROWS

    def gate_and_project(p_new, p_old, act_new, act_old):
        for piece in range(n_piece):
            if piece < n_piece // 2 and piece % (n_piece // 2 // n_out) == 0:
                c = piece // (n_piece // 2 // n_out)
                orow = slice(c * PEER_OUT_ROWS, (c + 1) * PEER_OUT_ROWS)
                acc_ref[orow, :] += _dot(evt_ref[orow, :], p_old[...])
            if piece >= n_piece // 2 and piece % (n_piece // 2 // n_act) == 0:
                c = (piece - n_piece // 2) // (n_piece // 2 // n_act)
                erow = slice(c * PEER_ACT_ROWS, (c + 1) * PEER_ACT_ROWS)
                act = _dot(eu_ref[erow, :], xnt_ref[...])
                for t in range(tb // PEER_TC):
                    act_new[t, erow, :] = act[:, t * PEER_TC:(t + 1) * PEER_TC]
            sub, tc = divmod(piece, tb // PEER_TC)
            rows = slice(sub * NK, (sub + 1) * NK)
            lanes = slice(tc * PEER_TC, (tc + 1) * PEER_TC)
            w = jnp.zeros((NK, PEER_TC), F32)
            for h in range(PEER_HEADS):
                sel = s2_ref[h, tc] >= th_ref[h, tc, sub:sub + 1, :]
                w = w + jnp.where(sel, e1_ref[h, tc, sub:sub + 1, :] * e2_ref[h, tc], 0.0)
            a = act_old[tc, rows, :]
            wa = w * a
            p_new[rows, lanes] = (wa + wa * jnp.tanh(a * (GELU_C1 + GELU_C2 * (a * a)))).astype(BF16)

    @pl.when(s % 2 == 0)
    def _():
        gate_and_project(p_even, p_odd, act_even, act_odd)

    @pl.when(s % 2 == 1)
    def _():
        gate_and_project(p_odd, p_even, act_odd, act_even)

    @pl.when(proj_last)
    def _():
        out_ref[...] = x_ref[...] + acc_ref[...].T


def _experts(x, xn, th, s2, e1, e2, e_u, e_vt):
    R = x.shape[0]
    tb = 512
    H, NK = PEER_HEADS, PEER_KEYS
    ne = e_u.shape[0] // PEER_EB
    total = (R // tb) * ne
    pair = lambda s, lag: jnp.clip(s - lag, 0, total - 1)
    nt = tb // PEER_TC
    tmap = pl.BlockSpec((H, nt, NK, PEER_TC), lambda s: (0, pair(s, 1) // ne, 0, 0))
    kmap = pl.BlockSpec((H, nt, PEER_SUB, PEER_TC), lambda s: (0, pair(s, 1) // ne, pair(s, 1) % ne, 0))
    kern = functools.partial(_expert_kernel, tb=tb, ne=ne)
    return pl.pallas_call(
        kern,
        grid=(total + 2,),
        in_specs=[pl.BlockSpec((tb, D_MODEL), lambda s: (pair(s, 2) // ne, 0)),
                  pl.BlockSpec((D_MODEL, tb), lambda s: (0, pair(s, 0) // ne)),
                  kmap, tmap, kmap, tmap,
                  pl.BlockSpec((PEER_EB, D_MODEL), lambda s: (pair(s, 0) % ne, 0)),
                  pl.BlockSpec((D_MODEL, PEER_EB), lambda s: (0, pair(s, 2) % ne))],
        out_specs=pl.BlockSpec((tb, D_MODEL), lambda s: (pair(s, 2) // ne, 0)),
        out_shape=jax.ShapeDtypeStruct((R, D_MODEL), F32),
        scratch_shapes=[pltpu.VMEM((D_MODEL, tb), F32),
                        pltpu.VMEM((nt, PEER_EB, PEER_TC), F32), pltpu.VMEM((nt, PEER_EB, PEER_TC), F32),
                        pltpu.VMEM((PEER_EB, tb), BF16), pltpu.VMEM((PEER_EB, tb), BF16)],
        compiler_params=_cparams("arbitrary"),
        name="peer_experts",
    )(x, xn, th, s2, e1, e2, e_u, e_vt)


def _final_norm_kernel(x_ref, g_ref, o_ref):
    o_ref[...] = _rms(x_ref[...], g_ref[...])


def _final_norm(x, g):
    R = x.shape[0]
    tb = 512
    return pl.pallas_call(
        _final_norm_kernel,
        grid=(R // tb,),
        in_specs=[pl.BlockSpec((tb, D_MODEL), lambda i: (i, 0)), pl.BlockSpec((1, D_MODEL), lambda i: (0, 0))],
        out_specs=pl.BlockSpec((tb, D_MODEL), lambda i: (i, 0)),
        out_shape=jax.ShapeDtypeStruct((R, D_MODEL), F32),
        compiler_params=_cparams("parallel"),
        name="final_norm",
    )(x, g)


def _s5_params(a_re, a_im, b_re, b_im, c_re, c_im, log_dt):
    dt = jnp.exp(log_dt)[:, None]
    mag = jnp.exp(dt * a_re)
    abar_re = mag * jnp.cos(dt * a_im)
    abar_im = mag * jnp.sin(dt * a_im)
    nr, ni = abar_re - 1.0, abar_im
    den = a_re * a_re + a_im * a_im
    coef_re = (nr * a_re + ni * a_im) / den
    coef_im = (ni * a_re - nr * a_im) / den
    bbar_re = coef_re[..., None] * b_re - coef_im[..., None] * b_im
    bbar_im = coef_re[..., None] * b_im + coef_im[..., None] * b_re
    eye = jnp.eye(SSM_GROUPS, dtype=F32)
    expand_b = lambda b: jnp.einsum('gpc,gh->gchp', b, eye).reshape(SSM_WIDTH, SSM_CH)
    expand_c = lambda c: jnp.einsum('gcp,gh->gphc', c, eye).reshape(SSM_CH, SSM_WIDTH)
    bmat = jnp.concatenate([expand_b(bbar_re), expand_b(bbar_im)], axis=1).astype(BF16)
    cmat = jnp.stack([expand_c(c_re), -expand_c(c_im)]).astype(BF16)
    abar = jnp.stack([abar_re.reshape(SSM_CH), abar_im.reshape(SSM_CH)])
    return bmat, abar, cmat


def _swap_halves(w):
    half = ROPE_DIM // 2
    return jnp.concatenate([w[..., half:], w[..., :half]], axis=-1)


def _rope_table(pos):
    half = ROPE_DIM // 2
    inv = 1.0 / (ROPE_THETA ** (jnp.arange(half, dtype=F32) / half))
    ang = pos[:, None] * inv[None, :]
    cos, sin = jnp.cos(ang), jnp.sin(ang)
    return jnp.concatenate([cos, cos, -sin, sin], axis=1)


def kernel(x_prompt, x_sample, cache_ckv, cache_kpe, state_conv, state_ssm_re, state_ssm_im, meta_tokens, norm_mix, norm_ffn, w_in, b_gate, conv_w, conv_b, w_a_out, q_norm, w_uq, kv_norm, w_ukv, w_b_out, ssm_a_re, ssm_a_im, ssm_b_re, ssm_b_im, ssm_c_re, ssm_c_im, ssm_log_dt, ssm_d, w_glu, w_o, peer_wq, peer_k1, peer_k2, peer_u, peer_v, norm_final):
    B, seq, _ = x_prompt.shape
    assert B == 1
    nb, dec_seq, _ = x_sample.shape
    ns = nb * dec_seq
    past = cache_ckv.shape[2]
    depth = w_in.shape[0]
    p_end = FRONT + seq
    R = -(-(p_end + ns) // ROW_ALIGN) * ROW_ALIGN
    s0 = R - ns
    lay = dict(ns=ns, dec_seq=dec_seq, p_end=p_end, s0=s0)

    x = jnp.concatenate([
        jnp.zeros((FRONT - N_META, D_MODEL), F32), meta_tokens, x_prompt[0],
        jnp.zeros((s0 - p_end, D_MODEL), F32), x_sample.reshape(ns, D_MODEL)], axis=0)
    pos = jnp.concatenate([
        jnp.arange(s0, dtype=F32) - (FRONT - N_META),
        jnp.tile(past + jnp.arange(dec_seq, dtype=F32), nb)])
    cs = _rope_table(pos)

    outs = [[] for _ in range(10)]
    for l in range(depth):
        wl = w_in[l]
        kp = wl[:, 2560:2624]
        w_small = jnp.concatenate([wl[:, 0:2560], wl[:, 2624:3136], kp, _swap_halves(kp)], axis=1).astype(BF16)
        w_gate = wl[:, 3136:].astype(BF16)
        wq3 = w_uq[l].reshape(Q_LORA, MLA_HEADS, QK_DIM)
        wq_aug = jnp.concatenate([wq3, _swap_halves(wq3[..., NOPE_DIM:])], axis=-1).reshape(Q_LORA, MLA_HEADS * 256)
        wq_aug = wq_aug.astype(BF16)
        wkv = w_ukv[l].astype(BF16)
        wkv3 = wkv.reshape(KV_LORA, MLA_HEADS, NOPE_DIM + V_DIM)
        w_k = wkv3[..., :NOPE_DIM].reshape(KV_LORA, MLA_HEADS * NOPE_DIM)
        w_vt = wkv3[..., NOPE_DIM:].transpose(1, 2, 0)
        bmat, abar, cmat = _s5_params(ssm_a_re[l], ssm_a_im[l], ssm_b_re[l], ssm_b_im[l], ssm_c_re[l],
                                      ssm_c_im[l], ssm_log_dt[l])
        h0 = jnp.stack([state_ssm_re[l].reshape(nb, SSM_CH), state_ssm_im[l].reshape(nb, SSM_CH)])

        proj = _in_proj(x, norm_mix[l][None], w_small)
        za, p_conv, s_conv = _conv(proj, conv_w[l], conv_b[l][None], state_conv[l].reshape(2 * nb, A_WIDTH), lay)
        q, k, vt, lat, kpe = _qkv(proj, cs, q_norm[l][None], wq_aug, kv_norm[l][None], w_k, w_vt)
        o = _flash(q, k, vt)
        lat_all = jnp.concatenate([cache_ckv[l], lat[s0:].reshape(nb, dec_seq, KV_LORA)], axis=1).astype(BF16)
        kpe_all = jnp.concatenate([cache_kpe[l], kpe[s0:].reshape(nb, dec_seq, ROPE_DIM)], axis=1).astype(BF16)
        o_s = _cached_attn(q, lat_all, kpe_all, wkv, lay)
        o = lax.dynamic_update_slice(o, o_s, (s0, 0))
        yc, p_state, s_state = _s5(proj, bmat, abar, cmat, ssm_d[l][None], h0, lay)
        x = _merge(x, norm_mix[l][None], za, o, yc, w_gate, b_gate[l][None], w_a_out[l].astype(BF16),
                   w_b_out[l].astype(BF16), w_glu[l].astype(BF16), w_o[l].astype(BF16))
        xn2, th, s2, e1, e2 = _route(x, norm_ffn[l][None], peer_wq[l].astype(BF16),
                                     peer_k1[l].astype(BF16), peer_k2[l].astype(BF16))
        x = _experts(x, xn2, th, s2, e1, e2, peer_u[l].astype(BF16), peer_v[l].astype(BF16).T)

        lo = FRONT - N_META
        outs[0].append(lat[lo:p_end][None])
        outs[1].append(kpe[lo:p_end][None])
        outs[2].append(p_conv[None])
        outs[3].append(p_state[0].reshape(1, SSM_GROUPS, SSM_STATE))
        outs[4].append(p_state[1].reshape(1, SSM_GROUPS, SSM_STATE))
        outs[5].append(lat[s0:].reshape(nb, dec_seq, KV_LORA))
        outs[6].append(kpe[s0:].reshape(nb, dec_seq, ROPE_DIM))
        outs[7].append(s_conv.reshape(nb, CONV_WIDTH - 1, A_WIDTH))
        outs[8].append(s_state[0].reshape(nb, SSM_GROUPS, SSM_STATE))
        outs[9].append(s_state[1].reshape(nb, SSM_GROUPS, SSM_STATE))

    y = _final_norm(x, norm_final[None])
    y_prompt = y[FRONT:p_end][None]
    y_sample = y[s0:].reshape(nb, dec_seq, D_MODEL)
    return (y_prompt, y_sample) + tuple(jnp.stack(o) for o in outs)
```

```python
import functools
import math

import jax
import jax.numpy as jnp
from jax import lax
from jax.experimental import pallas as pl
from jax.experimental.pallas import tpu as pltpu

F32 = jnp.float32
BF16 = jnp.bfloat16

D_MODEL = 2048
CHUNK = 64
N_META = 16
EPS = 1e-6
NEG_INF = -1e30
A_WIDTH = 512
CONV_WIDTH = 3
MLA_HEADS = 8
Q_LORA = 512
KV_LORA = 512
NOPE_DIM = 128
ROPE_DIM = 64
V_DIM = 128
QK_DIM = NOPE_DIM + ROPE_DIM
ROPE_THETA = 10000.0
ATTN_SCALE = 1.0 / math.sqrt(NOPE_DIM + ROPE_DIM)
SSM_GROUP = 16
SSM_GROUPS = 32
SSM_WIDTH = SSM_GROUP * SSM_GROUPS
SSM_STATE = 64
SSM_CH = SSM_GROUPS * SSM_STATE
PEER_HEADS = 8
PEER_KEYS = 128
PEER_QDIM = 256
PEER_TOPK = 16

FRONT = CHUNK
ROW_ALIGN = 512
VMEM_LIMIT = 60 * 1024 * 1024

C_AB, C_AC, C_AH, C_Q, C_KV, C_U, C_KPE, N_SMALL = 0, 512, 1024, 1536, 2048, 2560, 3072, 3200


def _cparams(*sem):
    return pltpu.CompilerParams(dimension_semantics=sem, vmem_limit_bytes=VMEM_LIMIT)


def _rms(x, g):
    return x * lax.rsqrt(jnp.mean(x * x, axis=-1, keepdims=True) + EPS) * g


def _sigmoid(x):
    return 1.0 / (1.0 + jnp.exp(-x))


def _gelu(x):
    return 0.5 * x * (1.0 + jnp.tanh(math.sqrt(2.0 / math.pi) * (x + 0.044715 * (x * x * x))))


def _dot(a, b):
    return jnp.dot(a, b, preferred_element_type=F32)


def _dot_nt(a, b):
    return lax.dot_general(a, b, (((1,), (1,)), ((), ())), preferred_element_type=F32)


def _in_proj_kernel(x_ref, g_ref, w_ref, o_ref):
    xn = _rms(x_ref[...], g_ref[...]).astype(BF16)
    o_ref[...] = _dot(xn, w_ref[...])


def _in_proj(x, g, w_small):
    R = x.shape[0]
    tb = 256
    return pl.pallas_call(
        _in_proj_kernel,
        grid=(R // tb,),
        in_specs=[pl.BlockSpec((tb, D_MODEL), lambda i: (i, 0)),
                  pl.BlockSpec((1, D_MODEL), lambda i: (0, 0)),
                  pl.BlockSpec((D_MODEL, N_SMALL), lambda i: (0, 0))],
        out_specs=pl.BlockSpec((tb, N_SMALL), lambda i: (i, 0)),
        out_shape=jax.ShapeDtypeStruct((R, N_SMALL), F32),
        compiler_params=_cparams("parallel"),
        name="in_proj",
    )(x, g, w_small)


def _conv_kernel(p_ref, w_ref, b_ref, st_ref, za_ref, pc_ref, sc_ref, zbuf, *, p_blk, p_off, n_blk, tb, dec_seq):
    i = pl.program_id(0)
    ab = p_ref[:, C_AB:C_AB + A_WIDTH]
    z = p_ref[:, C_AC:C_AC + A_WIDTH] * p_ref[:, C_AH:C_AH + A_WIDTH]
    w0, w1, w2 = w_ref[0:1, :], w_ref[1:2, :], w_ref[2:3, :]
    b = b_ref[...]

    @pl.when(i == 0)
    def _():
        zbuf[0:8, :] = jnp.zeros((8, A_WIDTH), F32)

    @pl.when(i < n_blk - 1)
    def _():
        zbuf[8:8 + tb, :] = z
        y = b + w0 * zbuf[6:6 + tb, :] + w1 * zbuf[7:7 + tb, :] + w2 * zbuf[8:8 + tb, :]
        za_ref[...] = (ab * y).astype(BF16)
        zbuf[0:8, :] = zbuf[tb:tb + 8, :]

    @pl.when(i == p_blk)
    def _():
        pc_ref[...] = z[p_off - 1:p_off + 1, :]

    @pl.when(i == n_blk - 1)
    def _():
        for s in range(tb // dec_seq):
            r0 = s * dec_seq
            zbuf[6:8, :] = st_ref[2 * s:2 * s + 2, :]
            zbuf[8:8 + dec_seq, :] = z[r0:r0 + dec_seq, :]
            y = (b + w0 * zbuf[6:6 + dec_seq, :] + w1 * zbuf[7:7 + dec_seq, :]
                 + w2 * zbuf[8:8 + dec_seq, :])
            za_ref[r0:r0 + dec_seq, :] = (ab[r0:r0 + dec_seq, :] * y).astype(BF16)
            sc_ref[2 * s:2 * s + 2, :] = z[r0 + dec_seq - 2:r0 + dec_seq, :]


def _conv(proj, conv_w, conv_b, state, lay):
    R = proj.shape[0]
    tb = lay["ns"]
    n_blk = R // tb
    last = lay["p_end"] - 1
    kern = functools.partial(_conv_kernel, p_blk=last // tb, p_off=last % tb, n_blk=n_blk, tb=tb,
                             dec_seq=lay["dec_seq"])
    assert last % tb >= 1
    nseq = tb // lay["dec_seq"]
    return pl.pallas_call(
        kern,
        grid=(n_blk,),
        in_specs=[pl.BlockSpec((tb, 3 * A_WIDTH), lambda i: (i, 0)),
                  pl.BlockSpec((CONV_WIDTH, A_WIDTH), lambda i: (0, 0)),
                  pl.BlockSpec((1, A_WIDTH), lambda i: (0, 0)),
                  pl.BlockSpec((2 * nseq, A_WIDTH), lambda i: (0, 0))],
        out_specs=[pl.BlockSpec((tb, A_WIDTH), lambda i: (i, 0)),
                   pl.BlockSpec((2, A_WIDTH), lambda i: (0, 0)),
                   pl.BlockSpec((2 * nseq, A_WIDTH), lambda i: (0, 0))],
        out_shape=[jax.ShapeDtypeStruct((R, A_WIDTH), BF16),
                   jax.ShapeDtypeStruct((2, A_WIDTH), F32),
                   jax.ShapeDtypeStruct((2 * nseq, A_WIDTH), F32)],
        scratch_shapes=[pltpu.VMEM((tb + 8, A_WIDTH), F32)],
        compiler_params=_cparams("arbitrary"),
        name="short_conv",
    )(proj, conv_w, conv_b, state)


QK_PAD = 256
Q_SCALE = ATTN_SCALE * math.log2(math.e)


V_ROWS = V_DIM + 16


def _qkv_kernel(cq_ref, ckv_ref, kp_ref, cs_ref, qn_ref, wq_ref, kn_ref, wk_ref, wvt_ref,
                q_ref, k_ref, vt_ref, lat_ref, kpe_ref, *, tb):
    cs = cs_ref[...]
    qc = _rms(cq_ref[...], qn_ref[...]).astype(BF16)
    q = _dot(qc, wq_ref[...])
    lat = _rms(ckv_ref[...], kn_ref[...])
    lat_ref[...] = lat
    lat_b = lat.astype(BF16)
    kn = _dot(lat_b, wk_ref[...])
    ones_row = jnp.where(lax.broadcasted_iota(jnp.int32, (V_ROWS - V_DIM, tb), 0) == 0, 1.0, 0.0).astype(BF16)
    lane = lax.broadcasted_iota(jnp.int32, (tb, 2 * ROPE_DIM), 1)
    row = pl.program_id(0) * tb + lax.broadcasted_iota(jnp.int32, (tb, 2 * ROPE_DIM), 0)
    q_tail = jnp.where(lane == ROPE_DIM, 1.0, 0.0)
    k_tail = jnp.where((lane == ROPE_DIM) & (row < FRONT - N_META), NEG_INF, 0.0)
    t = kp_ref[...] * cs
    kpe = t + pltpu.roll(t, ROPE_DIM, 1)
    kpe_ref[...] = kpe[:, 0:ROPE_DIM]
    k_hi = jnp.where(lane < ROPE_DIM, kpe, k_tail).astype(BF16)
    for h in range(MLA_HEADS):
        c = h * 256
        q_ref[h, :, 0:NOPE_DIM] = (q[:, c:c + NOPE_DIM] * Q_SCALE).astype(BF16)
        t = q[:, c + NOPE_DIM:c + 256] * cs
        qpe = (t + pltpu.roll(t, ROPE_DIM, 1)) * Q_SCALE
        q_ref[h, :, NOPE_DIM:QK_PAD] = jnp.where(lane < ROPE_DIM, qpe, q_tail).astype(BF16)
        k_ref[h, :, 0:NOPE_DIM] = kn[:, h * NOPE_DIM:(h + 1) * NOPE_DIM].astype(BF16)
        k_ref[h, :, NOPE_DIM:QK_PAD] = k_hi
        vt_ref[h, 0:V_DIM, :] = _dot_nt(wvt_ref[h], lat_b).astype(BF16)
        vt_ref[h, V_DIM:V_ROWS, :] = ones_row


def _qkv(proj, cs, q_norm, wq_aug, kv_norm, w_k, w_vt):
    R = proj.shape[0]
    tb = 256
    H = MLA_HEADS
    return pl.pallas_call(
        functools.partial(_qkv_kernel, tb=tb),
        grid=(R // tb,),
        in_specs=[pl.BlockSpec((tb, Q_LORA), lambda i: (i, C_Q // Q_LORA)),
                  pl.BlockSpec((tb, KV_LORA), lambda i: (i, C_KV // KV_LORA)),
                  pl.BlockSpec((tb, 2 * ROPE_DIM), lambda i: (i, C_KPE // (2 * ROPE_DIM))),
                  pl.BlockSpec((tb, 2 * ROPE_DIM), lambda i: (i, 0)),
                  pl.BlockSpec((1, Q_LORA), lambda i: (0, 0)),
                  pl.BlockSpec((Q_LORA, H * 256), lambda i: (0, 0)),
                  pl.BlockSpec((1, KV_LORA), lambda i: (0, 0)),
                  pl.BlockSpec((KV_LORA, H * NOPE_DIM), lambda i: (0, 0)),
                  pl.BlockSpec((H, V_DIM, KV_LORA), lambda i: (0, 0, 0))],
        out_specs=[pl.BlockSpec((H, tb, QK_PAD), lambda i: (0, i, 0)),
                   pl.BlockSpec((H, tb, QK_PAD), lambda i: (0, i, 0)),
                   pl.BlockSpec((H, V_ROWS, tb), lambda i: (0, 0, i)),
                   pl.BlockSpec((tb, KV_LORA), lambda i: (i, 0)),
                   pl.BlockSpec((tb, ROPE_DIM), lambda i: (i, 0))],
        out_shape=[jax.ShapeDtypeStruct((H, R, QK_PAD), BF16),
                   jax.ShapeDtypeStruct((H, R, QK_PAD), BF16),
                   jax.ShapeDtypeStruct((H, V_ROWS, R), BF16),
                   jax.ShapeDtypeStruct((R, KV_LORA), F32),
                   jax.ShapeDtypeStruct((R, ROPE_DIM), F32)],
        compiler_params=_cparams("parallel"),
        name="qkv_rope",
    )(proj, proj, proj, cs, q_norm, wq_aug, kv_norm, w_k, w_vt)


FLASH_TQ = 512
FLASH_TK = 512


def _flash_kernel(it_ref, jt_ref, q_ref, k_ref, vt_ref, o_ref, m_ref, acc_ref, *, bq):
    tq, tk = FLASH_TQ, FLASH_TK
    t = pl.program_id(1)
    i = it_ref[t]
    j = jt_ref[t]

    @pl.when(j == 0)
    def _():
        m_ref[...] = jnp.full(m_ref.shape, NEG_INF, F32)
        acc_ref[...] = jnp.zeros(acc_ref.shape, F32)

    def run(diag):
        tiles = [(a, b) for a in range(bq // tq) for b in range(bq // tk)
                 if not (diag and (b * tk) // CHUNK > (a * tq + tq - 1) // CHUNK)]

        def scores(a, b):
            return _dot_nt(k_ref[b * tk:(b + 1) * tk, :], q_ref[a * tq:(a + 1) * tq, :])

        tiles.sort(key=lambda ab: (ab[1], ab[0]))
        cols = {a: slice(a * tq, (a + 1) * tq) for a, _ in tiles}
        m = {a: m_ref[:, c] for a, c in cols.items()}
        acc = {a: acc_ref[:, c] for a, c in cols.items()}
        s_next = scores(*tiles[0])
        for n, (a, b) in enumerate(tiles):
            s = s_next
            if n + 1 < len(tiles):
                s_next = scores(*tiles[n + 1])
            if diag and (b * tk + tk - 1) // CHUNK > (a * tq) // CHUNK:
                kc = (b * tk + lax.broadcasted_iota(jnp.int32, (tk, tq), 0)) // CHUNK
                qc = (a * tq + lax.broadcasted_iota(jnp.int32, (tk, tq), 1)) // CHUNK
                s = jnp.where(kc <= qc, s, NEG_INF)
            m_new = jnp.maximum(m[a], jnp.max(s, axis=0, keepdims=True))
            alpha = jnp.exp2(m[a] - m_new)
            p = jnp.exp2(s - m_new).astype(BF16)
            acc[a] = alpha * acc[a] + _dot(vt_ref[:, b * tk:(b + 1) * tk], p)
            m[a] = m_new
        for a, c in cols.items():
            if diag:
                row = i * bq + a * tq + lax.broadcasted_iota(jnp.int32, (V_DIM, tq), 1)
                o = jnp.where(row >= FRONT - N_META, acc[a][0:V_DIM, :] / acc[a][V_DIM:V_DIM + 1, :], 0.0)
                o_ref[c, :] = o.T.astype(o_ref.dtype)
            else:
                m_ref[:, c], acc_ref[:, c] = m[a], acc[a]

    @pl.when(j < i)
    def _():
        run(False)

    @pl.when(j == i)
    def _():
        run(True)


def _flash(q, k, vt):
    H, R, _ = q.shape
    bq = next(b for b in (1536, 1024, 512) if R % b == 0)
    n = R // bq
    pairs = [(i, j) for i in range(n) for j in range(i + 1)]
    it = jnp.array([p[0] for p in pairs], jnp.int32)
    jt = jnp.array([p[1] for p in pairs], jnp.int32)
    grid_spec = pltpu.PrefetchScalarGridSpec(
        num_scalar_prefetch=2,
        grid=(H, len(pairs)),
        in_specs=[pl.BlockSpec((None, bq, QK_PAD), lambda h, t, it, jt: (h, it[t], 0)),
                  pl.BlockSpec((None, bq, QK_PAD), lambda h, t, it, jt: (h, jt[t], 0)),
                  pl.BlockSpec((None, V_ROWS, bq), lambda h, t, it, jt: (h, 0, jt[t]))],
        out_specs=pl.BlockSpec((bq, V_DIM), lambda h, t, it, jt: (it[t], h)),
        scratch_shapes=[pltpu.VMEM((1, bq), F32), pltpu.VMEM((V_ROWS, bq), F32)],
    )
    return pl.pallas_call(
        functools.partial(_flash_kernel, bq=bq),
        grid_spec=grid_spec,
        out_shape=jax.ShapeDtypeStruct((R, H * V_DIM), BF16),
        compiler_params=_cparams("parallel", "arbitrary"),
        name="prompt_attention",
    )(it, jt, q, k, vt)


def _cached_attn_kernel(q_ref, lat_ref, kpe_ref, w_ref, o_ref):
    kv = _dot(lat_ref[...], w_ref[...]).astype(BF16)
    kpe = kpe_ref[...]
    for h in range(MLA_HEADS):
        c = h * 256
        qh = q_ref[h]
        s = _dot_nt(qh[:, 0:NOPE_DIM], kv[:, c:c + NOPE_DIM]) + _dot_nt(qh[:, NOPE_DIM:QK_DIM], kpe)
        m = jnp.max(s, axis=1, keepdims=True)
        p = jnp.exp2(s - m)
        p = p / jnp.sum(p, axis=1, keepdims=True)
        o_ref[:, h * V_DIM:(h + 1) * V_DIM] = _dot(p.astype(BF16), kv[:, c + NOPE_DIM:c + 256]).astype(o_ref.dtype)


def _cached_attn(q, lat_all, kpe_all, w_ukv, lay):
    H = MLA_HEADS
    B, Lk, _ = lat_all.shape
    S = lay["dec_seq"]
    blk0 = lay["s0"] // S
    return pl.pallas_call(
        _cached_attn_kernel,
        grid=(B,),
        in_specs=[pl.BlockSpec((H, S, QK_PAD), lambda b: (0, blk0 + b, 0)),
                  pl.BlockSpec((None, Lk, KV_LORA), lambda b: (b, 0, 0)),
                  pl.BlockSpec((None, Lk, ROPE_DIM), lambda b: (b, 0, 0)),
                  pl.BlockSpec((KV_LORA, H * 256), lambda b: (0, 0))],
        out_specs=pl.BlockSpec((S, H * V_DIM), lambda b: (b, 0)),
        out_shape=jax.ShapeDtypeStruct((B * S, H * V_DIM), BF16),
        compiler_params=_cparams("parallel"),
        name="sample_attention",
    )(q, lat_all, kpe_all, w_ukv)


def _s5_kernel(u_ref, bm_ref, a_ref, cm_ref, d_ref, h0_ref, y_ref, ps_ref, ss_ref, hre, him, carry,
               *, p_blk, p_off, n_blk, tb, dec_seq):
    i = pl.program_id(0)
    u = u_ref[...]
    hb = _dot(u.astype(BF16), bm_ref[...])
    hre[...] = hb[:, 0:SSM_CH]
    him[...] = hb[:, SSM_CH:2 * SSM_CH]
    ar = a_ref[0:1, :]
    ai = a_ref[1:2, :]

    def scan(start, n, hr, hi):
        def body(t, c):
            hr, hi = c
            r = start + t
            nr = ar * hr - ai * hi + hre[pl.ds(r, 1), :]
            ni = ar * hi + ai * hr + him[pl.ds(r, 1), :]
            hre[pl.ds(r, 1), :] = nr
            him[pl.ds(r, 1), :] = ni
            return nr, ni
        return lax.fori_loop(0, n, body, (hr, hi), unroll=2)

    @pl.when(i == 0)
    def _():
        carry[...] = jnp.zeros(carry.shape, F32)

    @pl.when(i < n_blk - 1)
    def _():
        hr, hi = scan(0, tb, carry[0:1, :], carry[1:2, :])
        carry[0:1, :] = hr
        carry[1:2, :] = hi

    @pl.when(i == p_blk)
    def _():
        ps_ref[0:1, :] = hre[p_off:p_off + 1, :]
        ps_ref[1:2, :] = him[p_off:p_off + 1, :]

    @pl.when(i == n_blk - 1)
    def _():
        for s in range(tb // dec_seq):
            hr, hi = scan(s * dec_seq, dec_seq, h0_ref[0, s:s + 1, :], h0_ref[1, s:s + 1, :])
            ss_ref[0, s:s + 1, :] = hr
            ss_ref[1, s:s + 1, :] = hi

    y = _dot(hre[...].astype(BF16), cm_ref[0]) + _dot(him[...].astype(BF16), cm_ref[1])
    y_ref[...] = _gelu(y + d_ref[...] * u).astype(BF16)


def _s5(proj, bmat, abar, cmat, d, h0, lay):
    R = proj.shape[0]
    tb = lay["ns"]
    n_blk = R // tb
    last = lay["p_end"] - 1
    nseq = tb // lay["dec_seq"]
    kern = functools.partial(_s5_kernel, p_blk=last // tb, p_off=last % tb, n_blk=n_blk, tb=tb,
                             dec_seq=lay["dec_seq"])
    return pl.pallas_call(
        kern,
        grid=(n_blk,),
        in_specs=[pl.BlockSpec((tb, SSM_WIDTH), lambda i: (i, C_U // SSM_WIDTH)),
                  pl.BlockSpec((SSM_WIDTH, 2 * SSM_CH), lambda i: (0, 0)),
                  pl.BlockSpec((2, SSM_CH), lambda i: (0, 0)),
                  pl.BlockSpec((2, SSM_CH, SSM_WIDTH), lambda i: (0, 0, 0)),
                  pl.BlockSpec((1, SSM_WIDTH), lambda i: (0, 0)),
                  pl.BlockSpec((2, nseq, SSM_CH), lambda i: (0, 0, 0))],
        out_specs=[pl.BlockSpec((tb, SSM_WIDTH), lambda i: (i, 0)),
                   pl.BlockSpec((2, SSM_CH), lambda i: (0, 0)),
                   pl.BlockSpec((2, nseq, SSM_CH), lambda i: (0, 0, 0))],
        out_shape=[jax.ShapeDtypeStruct((R, SSM_WIDTH), BF16),
                   jax.ShapeDtypeStruct((2, SSM_CH), F32),
                   jax.ShapeDtypeStruct((2, nseq, SSM_CH), F32)],
        scratch_shapes=[pltpu.VMEM((tb, SSM_CH), F32), pltpu.VMEM((tb, SSM_CH), F32),
                        pltpu.VMEM((2, SSM_CH), F32)],
        compiler_params=_cparams("arbitrary"),
        name="s5_scan",
    )(proj, bmat, abar, cmat, d, h0)


MERGE_TN = 256
MERGE_NC = D_MODEL // MERGE_TN


def _merge_kernel(x_ref, g_ref, za_ref, o_ref, yc_ref, wg0_ref, wg1_ref, wg2_ref, b0_ref, b1_ref, b2_ref,
                  wa_ref, wb_ref, wga_ref, wgb_ref, wo_ref, out_ref, xn_s, mg_s):
    c = pl.program_id(1)

    @pl.when(c == 0)
    def _():
        xn_s[...] = _rms(x_ref[...], g_ref[...]).astype(BF16)

    xn = xn_s[...]
    g0 = _sigmoid(_dot(xn, wg0_ref[...]) + b0_ref[...])
    g1 = _sigmoid(_dot(xn, wg1_ref[...]) + b1_ref[...])
    g2 = _sigmoid(_dot(xn, wg2_ref[...]) + b2_ref[...])
    yc = yc_ref[...]
    y_a = _dot(za_ref[...], wa_ref[...])
    y_b = _dot(o_ref[...], wb_ref[...])
    y_c = _dot(yc, wga_ref[...]) * _sigmoid(_dot(yc, wgb_ref[...]))
    mg_s[c] = (g0 * y_a + g1 * y_b + g2 * y_c).astype(BF16)

    @pl.when(c == MERGE_NC - 1)
    def _():
        acc = x_ref[...]
        for cc in range(MERGE_NC):
            acc = acc + _dot(mg_s[cc], wo_ref[cc * MERGE_TN:(cc + 1) * MERGE_TN, :])
        out_ref[...] = acc


def _merge(x, g, za, o, yc, w_gate, b_gate, w_a, w_b, w_glu, w_o):
    R = x.shape[0]
    tb = 512
    tn, nc = MERGE_TN, MERGE_NC
    row = lambda w: pl.BlockSpec((tb, w), lambda i, c: (i, 0))
    col = lambda k, off: pl.BlockSpec((k, tn), lambda i, c: (0, off + c))
    return pl.pallas_call(
        _merge_kernel,
        grid=(R // tb, nc),
        in_specs=[row(D_MODEL), pl.BlockSpec((1, D_MODEL), lambda i, c: (0, 0)),
                  row(A_WIDTH), row(MLA_HEADS * V_DIM), row(SSM_WIDTH),
                  col(D_MODEL, 0), col(D_MODEL, nc), col(D_MODEL, 2 * nc),
                  col(1, 0), col(1, nc), col(1, 2 * nc),
                  col(A_WIDTH, 0), col(MLA_HEADS * V_DIM, 0), col(SSM_WIDTH, 0), col(SSM_WIDTH, nc),
                  pl.BlockSpec((D_MODEL, D_MODEL), lambda i, c: (0, 0))],
        out_specs=row(D_MODEL),
        out_shape=jax.ShapeDtypeStruct((R, D_MODEL), F32),
        scratch_shapes=[pltpu.VMEM((tb, D_MODEL), BF16), pltpu.VMEM((nc, tb, tn), BF16)],
        compiler_params=_cparams("parallel", "arbitrary"),
        name="branch_merge",
    )(x, g, za, o, yc, w_gate, w_gate, w_gate, b_gate, b_gate, b_gate, w_a, w_b, w_glu, w_glu, w_o)


SUBLANES = 8


def _sort_desc(v):
    n = len(v)
    v = list(v)
    k = 2
    while k <= n:
        j = k // 2
        while j >= 1:
            for i in range(n):
                l = i ^ j
                if l > i:
                    hi, lo = jnp.maximum(v[i], v[l]), jnp.minimum(v[i], v[l])
                    v[i], v[l] = (hi, lo) if (i & k) == 0 else (lo, hi)
            j //= 2
        k *= 2
    return v


def _merge_top(a, b):
    n = len(a)
    v = [jnp.maximum(a[i], b[n - 1 - i]) for i in range(n)]
    j = n // 2
    while j >= 1:
        for i in range(n):
            l = i ^ j
            if l > i:
                v[i], v[l] = jnp.maximum(v[i], v[l]), jnp.minimum(v[i], v[l])
        j //= 2
    return v


def _top_desc(tiles):
    v = _sort_desc(tiles)
    shift = SUBLANES // 2
    while shift >= 1:
        v = _merge_top(v, [pltpu.roll(x, shift, 0) for x in v])
        shift //= 2
    return v


def _pack_sublanes(rows, sub):
    out = rows[0]
    for r in range(1, len(rows)):
        out = jnp.where(sub == r, rows[r], out)
    return out


def _route_kernel(x_ref, g_ref, wq_ref, k1_ref, k2_ref, xnt_ref, th_ref, s2_ref, e1_ref, e2_ref, *, tb):
    K, NK, S = PEER_TOPK, PEER_KEYS, SUBLANES
    xn32 = _rms(x_ref[...], g_ref[...])
    xnt_ref[...] = xn32.T.astype(BF16)
    xn = xn32.astype(BF16)
    q = _dot(xn, wq_ref[...]).astype(BF16)
    half = PEER_QDIM // 2
    sub = lax.broadcasted_iota(jnp.int32, (S, tb), 0)
    neg = jnp.full((S, tb), -jnp.inf, F32)
    for h in range(PEER_HEADS):
        c = h * PEER_QDIM
        s1 = _dot_nt(k1_ref[...], q[:, c:c + half])
        s2 = _dot_nt(k2_ref[...], q[:, c + half:c + PEER_QDIM])
        t1 = [s1[S * r:S * (r + 1), :] for r in range(NK // S)]
        t2 = [s2[S * r:S * (r + 1), :] for r in range(NK // S)]
        v1 = _top_desc(t1)
        v2 = _top_desc(t2)
        v2_lo, v2_hi = _pack_sublanes(v2[0:S], sub), _pack_sublanes(v2[S:K], sub)
        v1_hi = _pack_sublanes(v1[S:K], sub)
        cand = [v1[0] + v2_lo, v1[0] + v2_hi] + [v1[a] + v2_lo for a in range(1, S)] + [v1_hi + v2[0]]
        vals = _top_desc(cand + [neg] * (K - len(cand)))
        tau = vals[K - 1]
        z = 1.0 + jnp.exp(vals[1] - vals[0])
        for r in range(2, K):
            z = z + jnp.exp(vals[r] - vals[0])
        scale = 0.5 / z
        for r in range(NK // S):
            rows = slice(S * r, S * (r + 1))
            th = jnp.full((S, tb), jnp.inf, F32)
            for b in range(K):
                th = jnp.where(t1[r] + v2[b] >= tau, v2[b], th)
            th_ref[h, rows, :] = th
            e1_ref[h, rows, :] = jnp.exp(t1[r] - v1[0]) * scale
            e2_ref[h, rows, :] = jnp.exp(t2[r] - v2[0])
        s2_ref[h] = s2


def _route(x, g, wq, k1, k2):
    R = x.shape[0]
    tb = 128
    H, NK = PEER_HEADS, PEER_KEYS
    tmap = pl.BlockSpec((H, None, NK, tb), lambda i: (0, i, 0, 0))
    tshape = jax.ShapeDtypeStruct((H, R // tb, NK, tb), F32)
    return pl.pallas_call(
        functools.partial(_route_kernel, tb=tb),
        grid=(R // tb,),
        in_specs=[pl.BlockSpec((tb, D_MODEL), lambda i: (i, 0)),
                  pl.BlockSpec((1, D_MODEL), lambda i: (0, 0)),
                  pl.BlockSpec((D_MODEL, H * PEER_QDIM), lambda i: (0, 0)),
                  pl.BlockSpec((NK, PEER_QDIM // 2), lambda i: (0, 0)),
                  pl.BlockSpec((NK, PEER_QDIM // 2), lambda i: (0, 0))],
        out_specs=[pl.BlockSpec((D_MODEL, tb), lambda i: (0, i)), tmap, tmap, tmap, tmap],
        out_shape=[jax.ShapeDtypeStruct((D_MODEL, R), BF16), tshape, tshape, tshape, tshape],
        compiler_params=_cparams("parallel"),
        name="peer_route",
    )(x, g, wq, k1, k2)


PEER_SUB = 8
PEER_EB = PEER_SUB * PEER_KEYS
PEER_TC = 128
PEER_OUT_ROWS = 256
PEER_ACT_ROWS = 128
GELU_C1 = math.sqrt(2.0 / math.pi)
GELU_C2 = GELU_C1 * 0.044715


def _expert_kernel(x_ref, xnt_ref, th_ref, s2_ref, e1_ref, e2_ref, eu_ref, evt_ref, out_ref, acc_ref,
                   act_even, act_odd, p_even, p_odd, *, tb, ne):
    s = pl.program_id(0)
    NK = PEER_KEYS
    proj_first = (s < 2) | ((s - 2) % ne == 0)
    proj_last = (s >= 2) & ((s - 2) % ne == ne - 1)

    @pl.when(s == 0)
    def _():
        act_odd[...] = jnp.zeros(act_odd.shape, F32)
        p_odd[...] = jnp.zeros(p_odd.shape, BF16)

    @pl.when(proj_first)
    def _():
        acc_ref[...] = jnp.zeros(acc_ref.shape, F32)

    n_piece = PEER_SUB * (tb // PEER_TC)
    n_out = D_MODEL // PEER_OUT_ROWS

    n_act = PEER_EB // PEER_ACT_ROWS

    def gate_and_project(p_new, p_old, act_new, act_old):
        for piece in range(n_piece):
            if piece < n_piece // 2 and piece % (n_piece // 2 // n_out) == 0:
                c = piece // (n_piece // 2 // n_out)
                orow = slice(c * PEER_OUT_ROWS, (c + 1) * PEER_OUT_ROWS)
                acc_ref[orow, :] += _dot(evt_ref[orow, :], p_old[...])
            if piece >= n_piece // 2 and piece % (n_piece // 2 // n_act) == 0:
                c = (piece - n_piece // 2) // (n_piece // 2 // n_act)
                erow = slice(c * PEER_ACT_ROWS, (c + 1) * PEER_ACT_ROWS)
                act = _dot(eu_ref[erow, :], xnt_ref[...])
                for t in range(tb // PEER_TC):
                    act_new[t, erow, :] = act[:, t * PEER_TC:(t + 1) * PEER_TC]
            sub, tc = divmod(piece, tb // PEER_TC)
            rows = slice(sub * NK, (sub + 1) * NK)
            lanes = slice(tc * PEER_TC, (tc + 1) * PEER_TC)
            w = jnp.zeros((NK, PEER_TC), F32)
            for h in range(PEER_HEADS):
                sel = s2_ref[h, tc] >= th_ref[h, tc, sub:sub + 1, :]
                w = w + jnp.where(sel, e1_ref[h, tc, sub:sub + 1, :] * e2_ref[h, tc], 0.0)
            a = act_old[tc, rows, :]
            wa = w * a
            p_new[rows, lanes] = (wa + wa * jnp.tanh(a * (GELU_C1 + GELU_C2 * (a * a)))).astype(BF16)

    @pl.when(s % 2 == 0)
    def _():
        gate_and_project(p_even, p_odd, act_even, act_odd)

    @pl.when(s % 2 == 1)
    def _():
        gate_and_project(p_odd, p_even, act_odd, act_even)

    @pl.when(proj_last)
    def _():
        out_ref[...] = x_ref[...] + acc_ref[...].T


def _experts(x, xn, th, s2, e1, e2, e_u, e_v):
    R = x.shape[0]
    tb = 512
    H, NK = PEER_HEADS, PEER_KEYS
    ne = e_u.shape[0] // PEER_EB
    e_vt = e_v.reshape(ne, PEER_EB, D_MODEL).transpose(0, 2, 1)
    total = (R // tb) * ne
    pair = lambda s, lag: jnp.clip(s - lag, 0, total - 1)
    nt = tb // PEER_TC
    tmap = pl.BlockSpec((H, nt, NK, PEER_TC), lambda s: (0, pair(s, 1) // ne, 0, 0))
    kmap = pl.BlockSpec((H, nt, PEER_SUB, PEER_TC), lambda s: (0, pair(s, 1) // ne, pair(s, 1) % ne, 0))
    kern = functools.partial(_expert_kernel, tb=tb, ne=ne)
    return pl.pallas_call(
        kern,
        grid=(total + 2,),
        in_specs=[pl.BlockSpec((tb, D_MODEL), lambda s: (pair(s, 2) // ne, 0)),
                  pl.BlockSpec((D_MODEL, tb), lambda s: (0, pair(s, 0) // ne)),
                  kmap, tmap, kmap, tmap,
                  pl.BlockSpec((PEER_EB, D_MODEL), lambda s: (pair(s, 0) % ne, 0)),
                  pl.BlockSpec((None, D_MODEL, PEER_EB), lambda s: (pair(s, 2) % ne, 0, 0))],
        out_specs=pl.BlockSpec((tb, D_MODEL), lambda s: (pair(s, 2) // ne, 0)),
        out_shape=jax.ShapeDtypeStruct((R, D_MODEL), F32),
        scratch_shapes=[pltpu.VMEM((D_MODEL, tb), F32),
                        pltpu.VMEM((nt, PEER_EB, PEER_TC), F32), pltpu.VMEM((nt, PEER_EB, PEER_TC), F32),
                        pltpu.VMEM((PEER_EB, tb), BF16), pltpu.VMEM((PEER_EB, tb), BF16)],
        compiler_params=_cparams("arbitrary"),
        name="peer_experts",
    )(x, xn, th, s2, e1, e2, e_u, e_vt)


def _final_norm_kernel(x_ref, g_ref, o_ref):
    o_ref[...] = _rms(x_ref[...], g_ref[...])


def _final_norm(x, g):
    R = x.shape[0]
    tb = 512
    return pl.pallas_call(
        _final_norm_kernel,
        grid=(R // tb,),
        in_specs=[pl.BlockSpec((tb, D_MODEL), lambda i: (i, 0)), pl.BlockSpec((1, D_MODEL), lambda i: (0, 0))],
        out_specs=pl.BlockSpec((tb, D_MODEL), lambda i: (i, 0)),
        out_shape=jax.ShapeDtypeStruct((R, D_MODEL), F32),
        compiler_params=_cparams("parallel"),
        name="final_norm",
    )(x, g)


def _s5_params(a_re, a_im, b_re, b_im, c_re, c_im, log_dt):
    dt = jnp.exp(log_dt)[:, None]
    mag = jnp.exp(dt * a_re)
    abar_re = mag * jnp.cos(dt * a_im)
    abar_im = mag * jnp.sin(dt * a_im)
    nr, ni = abar_re - 1.0, abar_im
    den = a_re * a_re + a_im * a_im
    coef_re = (nr * a_re + ni * a_im) / den
    coef_im = (ni * a_re - nr * a_im) / den
    bbar_re = coef_re[..., None] * b_re - coef_im[..., None] * b_im
    bbar_im = coef_re[..., None] * b_im + coef_im[..., None] * b_re
    eye = jnp.eye(SSM_GROUPS, dtype=F32)
    expand_b = lambda b: jnp.einsum('gpc,gh->gchp', b, eye).reshape(SSM_WIDTH, SSM_CH)
    expand_c = lambda c: jnp.einsum('gcp,gh->gphc', c, eye).reshape(SSM_CH, SSM_WIDTH)
    bmat = jnp.concatenate([expand_b(bbar_re), expand_b(bbar_im)], axis=1).astype(BF16)
    cmat = jnp.stack([expand_c(c_re), -expand_c(c_im)]).astype(BF16)
    abar = jnp.stack([abar_re.reshape(SSM_CH), abar_im.reshape(SSM_CH)])
    return bmat, abar, cmat


def _swap_halves(w):
    half = ROPE_DIM // 2
    return jnp.concatenate([w[..., half:], w[..., :half]], axis=-1)


def _rope_table(pos):
    half = ROPE_DIM // 2
    inv = 1.0 / (ROPE_THETA ** (jnp.arange(half, dtype=F32) / half))
    ang = pos[:, None] * inv[None, :]
    cos, sin = jnp.cos(ang), jnp.sin(ang)
    return jnp.concatenate([cos, cos, -sin, sin], axis=1)


def kernel(x_prompt, x_sample, cache_ckv, cache_kpe, state_conv, state_ssm_re, state_ssm_im, meta_tokens, norm_mix, norm_ffn, w_in, b_gate, conv_w, conv_b, w_a_out, q_norm, w_uq, kv_norm, w_ukv, w_b_out, ssm_a_re, ssm_a_im, ssm_b_re, ssm_b_im, ssm_c_re, ssm_c_im, ssm_log_dt, ssm_d, w_glu, w_o, peer_wq, peer_k1, peer_k2, peer_u, peer_v, norm_final):
    B, seq, _ = x_prompt.shape
    assert B == 1
    nb, dec_seq, _ = x_sample.shape
    ns = nb * dec_seq
    past = cache_ckv.shape[2]
    depth = w_in.shape[0]
    p_end = FRONT + seq
    R = -(-(p_end + ns) // ROW_ALIGN) * ROW_ALIGN
    s0 = R - ns
    lay = dict(ns=ns, dec_seq=dec_seq, p_end=p_end, s0=s0)

    x = jnp.concatenate([
        jnp.zeros((FRONT - N_META, D_MODEL), F32), meta_tokens, x_prompt[0],
        jnp.zeros((s0 - p_end, D_MODEL), F32), x_sample.reshape(ns, D_MODEL)], axis=0)
    pos = jnp.concatenate([
        jnp.arange(s0, dtype=F32) - (FRONT - N_META),
        jnp.tile(past + jnp.arange(dec_seq, dtype=F32), nb)])
    cs = _rope_table(pos)

    outs = [[] for _ in range(10)]
    for l in range(depth):
        wl = w_in[l]
        kp = wl[:, 2560:2624]
        w_small = jnp.concatenate([wl[:, 0:2560], wl[:, 2624:3136], kp, _swap_halves(kp)], axis=1).astype(BF16)
        w_gate = wl[:, 3136:].astype(BF16)
        wq3 = w_uq[l].reshape(Q_LORA, MLA_HEADS, QK_DIM)
        wq_aug = jnp.concatenate([wq3, _swap_halves(wq3[..., NOPE_DIM:])], axis=-1).reshape(Q_LORA, MLA_HEADS * 256)
        wq_aug = wq_aug.astype(BF16)
        wkv = w_ukv[l].astype(BF16)
        wkv3 = wkv.reshape(KV_LORA, MLA_HEADS, NOPE_DIM + V_DIM)
        w_k = wkv3[..., :NOPE_DIM].reshape(KV_LORA, MLA_HEADS * NOPE_DIM)
        w_vt = wkv3[..., NOPE_DIM:].transpose(1, 2, 0)
        bmat, abar, cmat = _s5_params(ssm_a_re[l], ssm_a_im[l], ssm_b_re[l], ssm_b_im[l], ssm_c_re[l],
                                      ssm_c_im[l], ssm_log_dt[l])
        h0 = jnp.stack([state_ssm_re[l].reshape(nb, SSM_CH), state_ssm_im[l].reshape(nb, SSM_CH)])

        proj = _in_proj(x, norm_mix[l][None], w_small)
        za, p_conv, s_conv = _conv(proj, conv_w[l], conv_b[l][None], state_conv[l].reshape(2 * nb, A_WIDTH), lay)
        q, k, vt, lat, kpe = _qkv(proj, cs, q_norm[l][None], wq_aug, kv_norm[l][None], w_k, w_vt)
        o = _flash(q, k, vt)
        lat_all = jnp.concatenate([cache_ckv[l], lat[s0:].reshape(nb, dec_seq, KV_LORA)], axis=1).astype(BF16)
        kpe_all = jnp.concatenate([cache_kpe[l], kpe[s0:].reshape(nb, dec_seq, ROPE_DIM)], axis=1).astype(BF16)
        o_s = _cached_attn(q, lat_all, kpe_all, wkv, lay)
        o = lax.dynamic_update_slice(o, o_s, (s0, 0))
        yc, p_state, s_state = _s5(proj, bmat, abar, cmat, ssm_d[l][None], h0, lay)
        x = _merge(x, norm_mix[l][None], za, o, yc, w_gate, b_gate[l][None], w_a_out[l].astype(BF16),
                   w_b_out[l].astype(BF16), w_glu[l].astype(BF16), w_o[l].astype(BF16))
        xn2, th, s2, e1, e2 = _route(x, norm_ffn[l][None], peer_wq[l].astype(BF16),
                                     peer_k1[l].astype(BF16), peer_k2[l].astype(BF16))
        x = _experts(x, xn2, th, s2, e1, e2, peer_u[l].astype(BF16), peer_v[l].astype(BF16))

        lo = FRONT - N_META
        outs[0].append(lat[lo:p_end][None])
        outs[1].append(kpe[lo:p_end][None])
        outs[2].append(p_conv[None])
        outs[3].append(p_state[0].reshape(1, SSM_GROUPS, SSM_STATE))
        outs[4].append(p_state[1].reshape(1, SSM_GROUPS, SSM_STATE))
        outs[5].append(lat[s0:].reshape(nb, dec_seq, KV_LORA))
        outs[6].append(kpe[s0:].reshape(nb, dec_seq, ROPE_DIM))
        outs[7].append(s_conv.reshape(nb, CONV_WIDTH - 1, A_WIDTH))
        outs[8].append(s_state[0].reshape(nb, SSM_GROUPS, SSM_STATE))
        outs[9].append(s_state[1].reshape(nb, SSM_GROUPS, SSM_STATE))

    y = _final_norm(x, norm_final[None])
    y_prompt = y[FRONT:p_end][None]
    y_sample = y[s0:].reshape(nb, dec_seq, D_MODEL)
    return (y_prompt, y_sample) + tuple(jnp.stack(o) for o in outs)
```

```python
import functools
import math

import jax
import jax.numpy as jnp
from jax import lax
from jax.experimental import pallas as pl
from jax.experimental.pallas import tpu as pltpu

F32 = jnp.float32
BF16 = jnp.bfloat16

D_MODEL = 2048
CHUNK = 64
N_META = 16
EPS = 1e-6
NEG_INF = -1e30
A_WIDTH = 512
CONV_WIDTH = 3
MLA_HEADS = 8
Q_LORA = 512
KV_LORA = 512
NOPE_DIM = 128
ROPE_DIM = 64
V_DIM = 128
QK_DIM = NOPE_DIM + ROPE_DIM
ROPE_THETA = 10000.0
ATTN_SCALE = 1.0 / math.sqrt(NOPE_DIM + ROPE_DIM)
SSM_GROUP = 16
SSM_GROUPS = 32
SSM_WIDTH = SSM_GROUP * SSM_GROUPS
SSM_STATE = 64
SSM_CH = SSM_GROUPS * SSM_STATE
PEER_HEADS = 8
PEER_KEYS = 128
PEER_QDIM = 256
PEER_TOPK = 16

FRONT = CHUNK
ROW_ALIGN = 512
VMEM_LIMIT = 60 * 1024 * 1024

C_AB, C_AC, C_AH, C_Q, C_KV, C_U, C_KPE, N_SMALL = 0, 512, 1024, 1536, 2048, 2560, 3072, 3200


def _cparams(*sem):
    return pltpu.CompilerParams(dimension_semantics=sem, vmem_limit_bytes=VMEM_LIMIT)


def _rms(x, g):
    return x * lax.rsqrt(jnp.mean(x * x, axis=-1, keepdims=True) + EPS) * g


def _sigmoid(x):
    return 1.0 / (1.0 + jnp.exp(-x))


def _gelu(x):
    return 0.5 * x * (1.0 + jnp.tanh(math.sqrt(2.0 / math.pi) * (x + 0.044715 * (x * x * x))))


def _dot(a, b):
    return jnp.dot(a, b, preferred_element_type=F32)


def _dot_nt(a, b):
    return lax.dot_general(a, b, (((1,), (1,)), ((), ())), preferred_element_type=F32)


def _in_proj_kernel(x_ref, g_ref, w_ref, o_ref):
    xn = _rms(x_ref[...], g_ref[...]).astype(BF16)
    o_ref[...] = _dot(xn, w_ref[...])


def _in_proj(x, g, w_small):
    R = x.shape[0]
    tb = 256
    return pl.pallas_call(
        _in_proj_kernel,
        grid=(R // tb,),
        in_specs=[pl.BlockSpec((tb, D_MODEL), lambda i: (i, 0)),
                  pl.BlockSpec((1, D_MODEL), lambda i: (0, 0)),
                  pl.BlockSpec((D_MODEL, N_SMALL), lambda i: (0, 0))],
        out_specs=pl.BlockSpec((tb, N_SMALL), lambda i: (i, 0)),
        out_shape=jax.ShapeDtypeStruct((R, N_SMALL), F32),
        compiler_params=_cparams("parallel"),
        name="in_proj",
    )(x, g, w_small)


def _conv_kernel(p_ref, w_ref, b_ref, st_ref, za_ref, pc_ref, sc_ref, zbuf, *, p_blk, p_off, n_blk, tb, dec_seq):
    i = pl.program_id(0)
    ab = p_ref[:, C_AB:C_AB + A_WIDTH]
    z = p_ref[:, C_AC:C_AC + A_WIDTH] * p_ref[:, C_AH:C_AH + A_WIDTH]
    w0, w1, w2 = w_ref[0:1, :], w_ref[1:2, :], w_ref[2:3, :]
    b = b_ref[...]

    @pl.when(i == 0)
    def _():
        zbuf[0:8, :] = jnp.zeros((8, A_WIDTH), F32)

    @pl.when(i < n_blk - 1)
    def _():
        zbuf[8:8 + tb, :] = z
        y = b + w0 * zbuf[6:6 + tb, :] + w1 * zbuf[7:7 + tb, :] + w2 * zbuf[8:8 + tb, :]
        za_ref[...] = (ab * y).astype(BF16)
        zbuf[0:8, :] = zbuf[tb:tb + 8, :]

    @pl.when(i == p_blk)
    def _():
        pc_ref[...] = z[p_off - 1:p_off + 1, :]

    @pl.when(i == n_blk - 1)
    def _():
        for s in range(tb // dec_seq):
            r0 = s * dec_seq
            zbuf[6:8, :] = st_ref[2 * s:2 * s + 2, :]
            zbuf[8:8 + dec_seq, :] = z[r0:r0 + dec_seq, :]
            y = (b + w0 * zbuf[6:6 + dec_seq, :] + w1 * zbuf[7:7 + dec_seq, :]
                 + w2 * zbuf[8:8 + dec_seq, :])
            za_ref[r0:r0 + dec_seq, :] = (ab[r0:r0 + dec_seq, :] * y).astype(BF16)
            sc_ref[2 * s:2 * s + 2, :] = z[r0 + dec_seq - 2:r0 + dec_seq, :]


def _conv(proj, conv_w, conv_b, state, lay):
    R = proj.shape[0]
    tb = lay["ns"]
    n_blk = R // tb
    last = lay["p_end"] - 1
    kern = functools.partial(_conv_kernel, p_blk=last // tb, p_off=last % tb, n_blk=n_blk, tb=tb,
                             dec_seq=lay["dec_seq"])
    assert last % tb >= 1
    nseq = tb // lay["dec_seq"]
    return pl.pallas_call(
        kern,
        grid=(n_blk,),
        in_specs=[pl.BlockSpec((tb, 3 * A_WIDTH), lambda i: (i, 0)),
                  pl.BlockSpec((CONV_WIDTH, A_WIDTH), lambda i: (0, 0)),
                  pl.BlockSpec((1, A_WIDTH), lambda i: (0, 0)),
                  pl.BlockSpec((2 * nseq, A_WIDTH), lambda i: (0, 0))],
        out_specs=[pl.BlockSpec((tb, A_WIDTH), lambda i: (i, 0)),
                   pl.BlockSpec((2, A_WIDTH), lambda i: (0, 0)),
                   pl.BlockSpec((2 * nseq, A_WIDTH), lambda i: (0, 0))],
        out_shape=[jax.ShapeDtypeStruct((R, A_WIDTH), BF16),
                   jax.ShapeDtypeStruct((2, A_WIDTH), F32),
                   jax.ShapeDtypeStruct((2 * nseq, A_WIDTH), F32)],
        scratch_shapes=[pltpu.VMEM((tb + 8, A_WIDTH), F32)],
        compiler_params=_cparams("arbitrary"),
        name="short_conv",
    )(proj, conv_w, conv_b, state)


QK_PAD = 256
Q_SCALE = ATTN_SCALE * math.log2(math.e)


V_ROWS = V_DIM + 16


def _qkv_kernel(cq_ref, ckv_ref, kp_ref, cs_ref, qn_ref, wq_ref, kn_ref, wk_ref, wvt_ref,
                q_ref, k_ref, vt_ref, lat_ref, kpe_ref, *, tb):
    cs = cs_ref[...]
    qc = _rms(cq_ref[...], qn_ref[...]).astype(BF16)
    q = _dot(qc, wq_ref[...])
    lat = _rms(ckv_ref[...], kn_ref[...])
    lat_ref[...] = lat
    lat_b = lat.astype(BF16)
    kn = _dot(lat_b, wk_ref[...])
    ones_row = jnp.where(lax.broadcasted_iota(jnp.int32, (V_ROWS - V_DIM, tb), 0) == 0, 1.0, 0.0).astype(BF16)
    lane = lax.broadcasted_iota(jnp.int32, (tb, 2 * ROPE_DIM), 1)
    row = pl.program_id(0) * tb + lax.broadcasted_iota(jnp.int32, (tb, 2 * ROPE_DIM), 0)
    q_tail = jnp.where(lane == ROPE_DIM, 1.0, 0.0)
    k_tail = jnp.where((lane == ROPE_DIM) & (row < FRONT - N_META), NEG_INF, 0.0)
    t = kp_ref[...] * cs
    kpe = t + pltpu.roll(t, ROPE_DIM, 1)
    kpe_ref[...] = kpe[:, 0:ROPE_DIM]
    k_hi = jnp.where(lane < ROPE_DIM, kpe, k_tail).astype(BF16)
    for h in range(MLA_HEADS):
        c = h * 256
        q_ref[h, :, 0:NOPE_DIM] = (q[:, c:c + NOPE_DIM] * Q_SCALE).astype(BF16)
        t = q[:, c + NOPE_DIM:c + 256] * cs
        qpe = (t + pltpu.roll(t, ROPE_DIM, 1)) * Q_SCALE
        q_ref[h, :, NOPE_DIM:QK_PAD] = jnp.where(lane < ROPE_DIM, qpe, q_tail).astype(BF16)
        k_ref[h, :, 0:NOPE_DIM] = kn[:, h * NOPE_DIM:(h + 1) * NOPE_DIM].astype(BF16)
        k_ref[h, :, NOPE_DIM:QK_PAD] = k_hi
        vt_ref[h, 0:V_DIM, :] = _dot_nt(wvt_ref[h], lat_b).astype(BF16)
        vt_ref[h, V_DIM:V_ROWS, :] = ones_row


def _qkv(proj, cs, q_norm, wq_aug, kv_norm, w_k, w_vt):
    R = proj.shape[0]
    tb = 256
    H = MLA_HEADS
    return pl.pallas_call(
        functools.partial(_qkv_kernel, tb=tb),
        grid=(R // tb,),
        in_specs=[pl.BlockSpec((tb, Q_LORA), lambda i: (i, C_Q // Q_LORA)),
                  pl.BlockSpec((tb, KV_LORA), lambda i: (i, C_KV // KV_LORA)),
                  pl.BlockSpec((tb, 2 * ROPE_DIM), lambda i: (i, C_KPE // (2 * ROPE_DIM))),
                  pl.BlockSpec((tb, 2 * ROPE_DIM), lambda i: (i, 0)),
                  pl.BlockSpec((1, Q_LORA), lambda i: (0, 0)),
                  pl.BlockSpec((Q_LORA, H * 256), lambda i: (0, 0)),
                  pl.BlockSpec((1, KV_LORA), lambda i: (0, 0)),
                  pl.BlockSpec((KV_LORA, H * NOPE_DIM), lambda i: (0, 0)),
                  pl.BlockSpec((H, V_DIM, KV_LORA), lambda i: (0, 0, 0))],
        out_specs=[pl.BlockSpec((H, tb, QK_PAD), lambda i: (0, i, 0)),
                   pl.BlockSpec((H, tb, QK_PAD), lambda i: (0, i, 0)),
                   pl.BlockSpec((H, V_ROWS, tb), lambda i: (0, 0, i)),
                   pl.BlockSpec((tb, KV_LORA), lambda i: (i, 0)),
                   pl.BlockSpec((tb, ROPE_DIM), lambda i: (i, 0))],
        out_shape=[jax.ShapeDtypeStruct((H, R, QK_PAD), BF16),
                   jax.ShapeDtypeStruct((H, R, QK_PAD), BF16),
                   jax.ShapeDtypeStruct((H, V_ROWS, R), BF16),
                   jax.ShapeDtypeStruct((R, KV_LORA), F32),
                   jax.ShapeDtypeStruct((R, ROPE_DIM), F32)],
        compiler_params=_cparams("parallel"),
        name="qkv_rope",
    )(proj, proj, proj, cs, q_norm, wq_aug, kv_norm, w_k, w_vt)


FLASH_TQ = 512
FLASH_TK = 512


def _flash_kernel(it_ref, jt_ref, q_ref, k_ref, vt_ref, o_ref, m_ref, acc_ref, *, bq):
    tq, tk = FLASH_TQ, FLASH_TK
    t = pl.program_id(1)
    i = it_ref[t]
    j = jt_ref[t]

    @pl.when(j == 0)
    def _():
        m_ref[...] = jnp.full(m_ref.shape, NEG_INF, F32)
        acc_ref[...] = jnp.zeros(acc_ref.shape, F32)

    def run(diag):
        tiles = [(a, b) for a in range(bq // tq) for b in range(bq // tk)
                 if not (diag and (b * tk) // CHUNK > (a * tq + tq - 1) // CHUNK)]

        def scores(a, b):
            return _dot_nt(k_ref[b * tk:(b + 1) * tk, :], q_ref[a * tq:(a + 1) * tq, :])

        tiles.sort(key=lambda ab: (ab[1], ab[0]))
        cols = {a: slice(a * tq, (a + 1) * tq) for a, _ in tiles}
        m = {a: m_ref[:, c] for a, c in cols.items()}
        acc = {a: acc_ref[:, c] for a, c in cols.items()}
        s_next = scores(*tiles[0])
        for n, (a, b) in enumerate(tiles):
            s = s_next
            if n + 1 < len(tiles):
                s_next = scores(*tiles[n + 1])
            if diag and (b * tk + tk - 1) // CHUNK > (a * tq) // CHUNK:
                kc = (b * tk + lax.broadcasted_iota(jnp.int32, (tk, tq), 0)) // CHUNK
                qc = (a * tq + lax.broadcasted_iota(jnp.int32, (tk, tq), 1)) // CHUNK
                s = jnp.where(kc <= qc, s, NEG_INF)
            m_new = jnp.maximum(m[a], jnp.max(s, axis=0, keepdims=True))
            alpha = jnp.exp2(m[a] - m_new)
            p = jnp.exp2(s - m_new).astype(BF16)
            acc[a] = alpha * acc[a] + _dot(vt_ref[:, b * tk:(b + 1) * tk], p)
            m[a] = m_new
        for a, c in cols.items():
            if diag:
                row = i * bq + a * tq + lax.broadcasted_iota(jnp.int32, (V_DIM, tq), 1)
                o = jnp.where(row >= FRONT - N_META, acc[a][0:V_DIM, :] / acc[a][V_DIM:V_DIM + 1, :], 0.0)
                o_ref[c, :] = o.T.astype(o_ref.dtype)
            else:
                m_ref[:, c], acc_ref[:, c] = m[a], acc[a]

    @pl.when(j < i)
    def _():
        run(False)

    @pl.when(j == i)
    def _():
        run(True)


def _flash(q, k, vt):
    H, R, _ = q.shape
    bq = next(b for b in (1536, 1024, 512) if R % b == 0)
    n = R // bq
    pairs = [(i, j) for i in range(n) for j in range(i + 1)]
    it = jnp.array([p[0] for p in pairs], jnp.int32)
    jt = jnp.array([p[1] for p in pairs], jnp.int32)
    grid_spec = pltpu.PrefetchScalarGridSpec(
        num_scalar_prefetch=2,
        grid=(H, len(pairs)),
        in_specs=[pl.BlockSpec((None, bq, QK_PAD), lambda h, t, it, jt: (h, it[t], 0)),
                  pl.BlockSpec((None, bq, QK_PAD), lambda h, t, it, jt: (h, jt[t], 0)),
                  pl.BlockSpec((None, V_ROWS, bq), lambda h, t, it, jt: (h, 0, jt[t]))],
        out_specs=pl.BlockSpec((bq, V_DIM), lambda h, t, it, jt: (it[t], h)),
        scratch_shapes=[pltpu.VMEM((1, bq), F32), pltpu.VMEM((V_ROWS, bq), F32)],
    )
    return pl.pallas_call(
        functools.partial(_flash_kernel, bq=bq),
        grid_spec=grid_spec,
        out_shape=jax.ShapeDtypeStruct((R, H * V_DIM), BF16),
        compiler_params=_cparams("parallel", "arbitrary"),
        name="prompt_attention",
    )(it, jt, q, k, vt)


def _cached_attn_kernel(q_ref, lat_ref, kpe_ref, w_ref, o_all_ref, o_ref):
    del o_all_ref
    kv = _dot(lat_ref[...], w_ref[...]).astype(BF16)
    kpe = kpe_ref[...]
    for h in range(MLA_HEADS):
        c = h * 256
        qh = q_ref[h]
        s = _dot_nt(qh[:, 0:NOPE_DIM], kv[:, c:c + NOPE_DIM]) + _dot_nt(qh[:, NOPE_DIM:QK_DIM], kpe)
        m = jnp.max(s, axis=1, keepdims=True)
        p = jnp.exp2(s - m)
        p = p / jnp.sum(p, axis=1, keepdims=True)
        o_ref[:, h * V_DIM:(h + 1) * V_DIM] = _dot(p.astype(BF16), kv[:, c + NOPE_DIM:c + 256]).astype(o_ref.dtype)


def _cached_attn(q, lat_all, kpe_all, w_ukv, o_all, lay):
    H = MLA_HEADS
    B, Lk, _ = lat_all.shape
    S = lay["dec_seq"]
    blk0 = lay["s0"] // S
    return pl.pallas_call(
        _cached_attn_kernel,
        grid=(B,),
        in_specs=[pl.BlockSpec((H, S, QK_PAD), lambda b: (0, blk0 + b, 0)),
                  pl.BlockSpec((None, Lk, KV_LORA), lambda b: (b, 0, 0)),
                  pl.BlockSpec((None, Lk, ROPE_DIM), lambda b: (b, 0, 0)),
                  pl.BlockSpec((KV_LORA, H * 256), lambda b: (0, 0)),
                  pl.BlockSpec(memory_space=pl.ANY)],
        out_specs=pl.BlockSpec((S, H * V_DIM), lambda b: (blk0 + b, 0)),
        out_shape=jax.ShapeDtypeStruct(o_all.shape, o_all.dtype),
        input_output_aliases={4: 0},
        compiler_params=_cparams("parallel"),
        name="sample_attention",
    )(q, lat_all, kpe_all, w_ukv, o_all)


def _s5_kernel(u_ref, bm_ref, a_ref, cm_ref, d_ref, h0_ref, y_ref, ps_ref, ss_ref, hre, him, carry,
               *, p_blk, p_off, n_blk, tb, dec_seq):
    i = pl.program_id(0)
    u = u_ref[...]
    hb = _dot(u.astype(BF16), bm_ref[...])
    hre[...] = hb[:, 0:SSM_CH]
    him[...] = hb[:, SSM_CH:2 * SSM_CH]
    ar = a_ref[0:1, :]
    ai = a_ref[1:2, :]

    def scan(start, n, hr, hi):
        def body(t, c):
            hr, hi = c
            r = start + t
            nr = ar * hr - ai * hi + hre[pl.ds(r, 1), :]
            ni = ar * hi + ai * hr + him[pl.ds(r, 1), :]
            hre[pl.ds(r, 1), :] = nr
            him[pl.ds(r, 1), :] = ni
            return nr, ni
        return lax.fori_loop(0, n, body, (hr, hi), unroll=2)

    @pl.when(i == 0)
    def _():
        carry[...] = jnp.zeros(carry.shape, F32)

    @pl.when(i < n_blk - 1)
    def _():
        hr, hi = scan(0, tb, carry[0:1, :], carry[1:2, :])
        carry[0:1, :] = hr
        carry[1:2, :] = hi

    @pl.when(i == p_blk)
    def _():
        ps_ref[0:1, :] = hre[p_off:p_off + 1, :]
        ps_ref[1:2, :] = him[p_off:p_off + 1, :]

    @pl.when(i == n_blk - 1)
    def _():
        for s in range(tb // dec_seq):
            hr, hi = scan(s * dec_seq, dec_seq, h0_ref[0, s:s + 1, :], h0_ref[1, s:s + 1, :])
            ss_ref[0, s:s + 1, :] = hr
            ss_ref[1, s:s + 1, :] = hi

    y = _dot(hre[...].astype(BF16), cm_ref[0]) + _dot(him[...].astype(BF16), cm_ref[1])
    y_ref[...] = _gelu(y + d_ref[...] * u).astype(BF16)


def _s5(proj, bmat, abar, cmat, d, h0, lay):
    R = proj.shape[0]
    tb = lay["ns"]
    n_blk = R // tb
    last = lay["p_end"] - 1
    nseq = tb // lay["dec_seq"]
    kern = functools.partial(_s5_kernel, p_blk=last // tb, p_off=last % tb, n_blk=n_blk, tb=tb,
                             dec_seq=lay["dec_seq"])
    return pl.pallas_call(
        kern,
        grid=(n_blk,),
        in_specs=[pl.BlockSpec((tb, SSM_WIDTH), lambda i: (i, C_U // SSM_WIDTH)),
                  pl.BlockSpec((SSM_WIDTH, 2 * SSM_CH), lambda i: (0, 0)),
                  pl.BlockSpec((2, SSM_CH), lambda i: (0, 0)),
                  pl.BlockSpec((2, SSM_CH, SSM_WIDTH), lambda i: (0, 0, 0)),
                  pl.BlockSpec((1, SSM_WIDTH), lambda i: (0, 0)),
                  pl.BlockSpec((2, nseq, SSM_CH), lambda i: (0, 0, 0))],
        out_specs=[pl.BlockSpec((tb, SSM_WIDTH), lambda i: (i, 0)),
                   pl.BlockSpec((2, SSM_CH), lambda i: (0, 0)),
                   pl.BlockSpec((2, nseq, SSM_CH), lambda i: (0, 0, 0))],
        out_shape=[jax.ShapeDtypeStruct((R, SSM_WIDTH), BF16),
                   jax.ShapeDtypeStruct((2, SSM_CH), F32),
                   jax.ShapeDtypeStruct((2, nseq, SSM_CH), F32)],
        scratch_shapes=[pltpu.VMEM((tb, SSM_CH), F32), pltpu.VMEM((tb, SSM_CH), F32),
                        pltpu.VMEM((2, SSM_CH), F32)],
        compiler_params=_cparams("arbitrary"),
        name="s5_scan",
    )(proj, bmat, abar, cmat, d, h0)


MERGE_TN = 256
MERGE_NC = D_MODEL // MERGE_TN


def _merge_kernel(x_ref, g_ref, za_ref, o_ref, yc_ref, wg0_ref, wg1_ref, wg2_ref, b0_ref, b1_ref, b2_ref,
                  wa_ref, wb_ref, wga_ref, wgb_ref, wo_ref, out_ref, xn_s, mg_s):
    c = pl.program_id(1)

    @pl.when(c == 0)
    def _():
        xn_s[...] = _rms(x_ref[...], g_ref[...]).astype(BF16)

    xn = xn_s[...]
    g0 = _sigmoid(_dot(xn, wg0_ref[...]) + b0_ref[...])
    g1 = _sigmoid(_dot(xn, wg1_ref[...]) + b1_ref[...])
    g2 = _sigmoid(_dot(xn, wg2_ref[...]) + b2_ref[...])
    yc = yc_ref[...]
    y_a = _dot(za_ref[...], wa_ref[...])
    y_b = _dot(o_ref[...], wb_ref[...])
    y_c = _dot(yc, wga_ref[...]) * _sigmoid(_dot(yc, wgb_ref[...]))
    mg_s[c] = (g0 * y_a + g1 * y_b + g2 * y_c).astype(BF16)

    @pl.when(c == MERGE_NC - 1)
    def _():
        acc = x_ref[...]
        for cc in range(MERGE_NC):
            acc = acc + _dot(mg_s[cc], wo_ref[cc * MERGE_TN:(cc + 1) * MERGE_TN, :])
        out_ref[...] = acc


def _merge(x, g, za, o, yc, w_gate, b_gate, w_a, w_b, w_glu, w_o):
    R = x.shape[0]
    tb = 512
    tn, nc = MERGE_TN, MERGE_NC
    row = lambda w: pl.BlockSpec((tb, w), lambda i, c: (i, 0))
    col = lambda k, off: pl.BlockSpec((k, tn), lambda i, c: (0, off + c))
    return pl.pallas_call(
        _merge_kernel,
        grid=(R // tb, nc),
        in_specs=[row(D_MODEL), pl.BlockSpec((1, D_MODEL), lambda i, c: (0, 0)),
                  row(A_WIDTH), row(MLA_HEADS * V_DIM), row(SSM_WIDTH),
                  col(D_MODEL, 0), col(D_MODEL, nc), col(D_MODEL, 2 * nc),
                  col(1, 0), col(1, nc), col(1, 2 * nc),
                  col(A_WIDTH, 0), col(MLA_HEADS * V_DIM, 0), col(SSM_WIDTH, 0), col(SSM_WIDTH, nc),
                  pl.BlockSpec((D_MODEL, D_MODEL), lambda i, c: (0, 0))],
        out_specs=row(D_MODEL),
        out_shape=jax.ShapeDtypeStruct((R, D_MODEL), F32),
        scratch_shapes=[pltpu.VMEM((tb, D_MODEL), BF16), pltpu.VMEM((nc, tb, tn), BF16)],
        compiler_params=_cparams("parallel", "arbitrary"),
        name="branch_merge",
    )(x, g, za, o, yc, w_gate, w_gate, w_gate, b_gate, b_gate, b_gate, w_a, w_b, w_glu, w_glu, w_o)


SUBLANES = 8


def _sort_desc(v):
    n = len(v)
    v = list(v)
    k = 2
    while k <= n:
        j = k // 2
        while j >= 1:
            for i in range(n):
                l = i ^ j
                if l > i:
                    hi, lo = jnp.maximum(v[i], v[l]), jnp.minimum(v[i], v[l])
                    v[i], v[l] = (hi, lo) if (i & k) == 0 else (lo, hi)
            j //= 2
        k *= 2
    return v


def _merge_top(a, b):
    n = len(a)
    v = [jnp.maximum(a[i], b[n - 1 - i]) for i in range(n)]
    j = n // 2
    while j >= 1:
        for i in range(n):
            l = i ^ j
            if l > i:
                v[i], v[l] = jnp.maximum(v[i], v[l]), jnp.minimum(v[i], v[l])
        j //= 2
    return v


def _top_desc(tiles):
    v = _sort_desc(tiles)
    shift = SUBLANES // 2
    while shift >= 1:
        v = _merge_top(v, [pltpu.roll(x, shift, 0) for x in v])
        shift //= 2
    return v


def _pack_sublanes(rows, sub):
    out = rows[0]
    for r in range(1, len(rows)):
        out = jnp.where(sub == r, rows[r], out)
    return out


def _route_kernel(x_ref, g_ref, wq_ref, k1_ref, k2_ref, xnt_ref, th_ref, s2_ref, e1_ref, e2_ref, *, tb):
    K, NK, S = PEER_TOPK, PEER_KEYS, SUBLANES
    xn32 = _rms(x_ref[...], g_ref[...])
    xnt_ref[...] = xn32.T.astype(BF16)
    xn = xn32.astype(BF16)
    q = _dot(xn, wq_ref[...]).astype(BF16)
    half = PEER_QDIM // 2
    sub = lax.broadcasted_iota(jnp.int32, (S, tb), 0)
    neg = jnp.full((S, tb), -jnp.inf, F32)
    for h in range(PEER_HEADS):
        c = h * PEER_QDIM
        s1 = _dot_nt(k1_ref[...], q[:, c:c + half])
        s2 = _dot_nt(k2_ref[...], q[:, c + half:c + PEER_QDIM])
        t1 = [s1[S * r:S * (r + 1), :] for r in range(NK // S)]
        t2 = [s2[S * r:S * (r + 1), :] for r in range(NK // S)]
        v1 = _top_desc(t1)
        v2 = _top_desc(t2)
        v2_lo, v2_hi = _pack_sublanes(v2[0:S], sub), _pack_sublanes(v2[S:K], sub)
        v1_hi = _pack_sublanes(v1[S:K], sub)
        cand = [v1[0] + v2_lo, v1[0] + v2_hi] + [v1[a] + v2_lo for a in range(1, S)] + [v1_hi + v2[0]]
        vals = _top_desc(cand + [neg] * (K - len(cand)))
        tau = vals[K - 1]
        z = 1.0 + jnp.exp(vals[1] - vals[0])
        for r in range(2, K):
            z = z + jnp.exp(vals[r] - vals[0])
        scale = 0.5 / z
        for r in range(NK // S):
            rows = slice(S * r, S * (r + 1))
            th = jnp.full((S, tb), jnp.inf, F32)
            for b in range(K):
                th = jnp.where(t1[r] + v2[b] >= tau, v2[b], th)
            th_ref[h, rows, :] = th
            e1_ref[h, rows, :] = jnp.exp(t1[r] - v1[0]) * scale
            e2_ref[h, rows, :] = jnp.exp(t2[r] - v2[0])
        s2_ref[h] = s2


def _route(x, g, wq, k1, k2):
    R = x.shape[0]
    tb = 128
    H, NK = PEER_HEADS, PEER_KEYS
    tmap = pl.BlockSpec((H, None, NK, tb), lambda i: (0, i, 0, 0))
    tshape = jax.ShapeDtypeStruct((H, R // tb, NK, tb), F32)
    return pl.pallas_call(
        functools.partial(_route_kernel, tb=tb),
        grid=(R // tb,),
        in_specs=[pl.BlockSpec((tb, D_MODEL), lambda i: (i, 0)),
                  pl.BlockSpec((1, D_MODEL), lambda i: (0, 0)),
                  pl.BlockSpec((D_MODEL, H * PEER_QDIM), lambda i: (0, 0)),
                  pl.BlockSpec((NK, PEER_QDIM // 2), lambda i: (0, 0)),
                  pl.BlockSpec((NK, PEER_QDIM // 2), lambda i: (0, 0))],
        out_specs=[pl.BlockSpec((D_MODEL, tb), lambda i: (0, i)), tmap, tmap, tmap, tmap],
        out_shape=[jax.ShapeDtypeStruct((D_MODEL, R), BF16), tshape, tshape, tshape, tshape],
        compiler_params=_cparams("parallel"),
        name="peer_route",
    )(x, g, wq, k1, k2)


PEER_SUB = 8
PEER_EB = PEER_SUB * PEER_KEYS
PEER_TC = 128
PEER_OUT_ROWS = 256
PEER_ACT_ROWS = 128
GELU_C1 = math.sqrt(2.0 / math.pi)
GELU_C2 = GELU_C1 * 0.044715


def _expert_kernel(x_ref, xnt_ref, th_ref, s2_ref, e1_ref, e2_ref, eu_ref, evt_ref, gf_ref, out_ref, acc_ref,
                   act_even, act_odd, p_even, p_odd, *, tb, ne, final_norm):
    s = pl.program_id(0)
    NK = PEER_KEYS
    proj_first = (s < 2) | ((s - 2) % ne == 0)
    proj_last = (s >= 2) & ((s - 2) % ne == ne - 1)

    @pl.when(s == 0)
    def _():
        act_odd[...] = jnp.zeros(act_odd.shape, F32)
        p_odd[...] = jnp.zeros(p_odd.shape, BF16)

    @pl.when(proj_first)
    def _():
        acc_ref[...] = jnp.zeros(acc_ref.shape, F32)

    n_piece = PEER_SUB * (tb // PEER_TC)
    n_out = D_MODEL // PEER_OUT_ROWS

    n_act = PEER_EB // PEER_ACT_ROWS

    def gate_and_project(p_new, p_old, act_new, act_old):
        for piece in range(n_piece):
            if piece < n_piece // 2 and piece % (n_piece // 2 // n_out) == 0:
                c = piece // (n_piece // 2 // n_out)
                orow = slice(c * PEER_OUT_ROWS, (c + 1) * PEER_OUT_ROWS)
                acc_ref[orow, :] += _dot(evt_ref[orow, :], p_old[...])
            if piece >= n_piece // 2 and piece % (n_piece // 2 // n_act) == 0:
                c = (piece - n_piece // 2) // (n_piece // 2 // n_act)
                erow = slice(c * PEER_ACT_ROWS, (c + 1) * PEER_ACT_ROWS)
                act = _dot(eu_ref[erow, :], xnt_ref[...])
                for t in range(tb // PEER_TC):
                    act_new[t, erow, :] = act[:, t * PEER_TC:(t + 1) * PEER_TC]
            sub, tc = divmod(piece, tb // PEER_TC)
            rows = slice(sub * NK, (sub + 1) * NK)
            lanes = slice(tc * PEER_TC, (tc + 1) * PEER_TC)
            w = jnp.zeros((NK, PEER_TC), F32)
            for h in range(PEER_HEADS):
                sel = s2_ref[h, tc] >= th_ref[h, tc, sub:sub + 1, :]
                w = w + jnp.where(sel, e1_ref[h, tc, sub:sub + 1, :] * e2_ref[h, tc], 0.0)
            a = act_old[tc, rows, :]
            wa = w * a
            p_new[rows, lanes] = (wa + wa * jnp.tanh(a * (GELU_C1 + GELU_C2 * (a * a)))).astype(BF16)

    @pl.when(s % 2 == 0)
    def _():
        gate_and_project(p_even, p_odd, act_even, act_odd)

    @pl.when(s % 2 == 1)
    def _():
        gate_and_project(p_odd, p_even, act_odd, act_even)

    @pl.when(proj_last)
    def _():
        y = x_ref[...] + acc_ref[...].T
        out_ref[...] = _rms(y, gf_ref[...]) if final_norm else y


def _experts(x, xn, th, s2, e1, e2, e_u, e_v, g_final, final_norm):
    R = x.shape[0]
    tb = 512
    H, NK = PEER_HEADS, PEER_KEYS
    ne = e_u.shape[0] // PEER_EB
    e_vt = e_v.reshape(ne, PEER_EB, D_MODEL).transpose(0, 2, 1)
    total = (R // tb) * ne
    pair = lambda s, lag: jnp.clip(s - lag, 0, total - 1)
    nt = tb // PEER_TC
    tmap = pl.BlockSpec((H, nt, NK, PEER_TC), lambda s: (0, pair(s, 1) // ne, 0, 0))
    kmap = pl.BlockSpec((H, nt, PEER_SUB, PEER_TC), lambda s: (0, pair(s, 1) // ne, pair(s, 1) % ne, 0))
    kern = functools.partial(_expert_kernel, tb=tb, ne=ne, final_norm=final_norm)
    return pl.pallas_call(
        kern,
        grid=(total + 2,),
        in_specs=[pl.BlockSpec((tb, D_MODEL), lambda s: (pair(s, 2) // ne, 0)),
                  pl.BlockSpec((D_MODEL, tb), lambda s: (0, pair(s, 0) // ne)),
                  kmap, tmap, kmap, tmap,
                  pl.BlockSpec((PEER_EB, D_MODEL), lambda s: (pair(s, 0) % ne, 0)),
                  pl.BlockSpec((None, D_MODEL, PEER_EB), lambda s: (pair(s, 2) % ne, 0, 0)),
                  pl.BlockSpec((1, D_MODEL), lambda s: (0, 0))],
        out_specs=pl.BlockSpec((tb, D_MODEL), lambda s: (pair(s, 2) // ne, 0)),
        out_shape=jax.ShapeDtypeStruct((R, D_MODEL), F32),
        scratch_shapes=[pltpu.VMEM((D_MODEL, tb), F32),
                        pltpu.VMEM((nt, PEER_EB, PEER_TC), F32), pltpu.VMEM((nt, PEER_EB, PEER_TC), F32),
                        pltpu.VMEM((PEER_EB, tb), BF16), pltpu.VMEM((PEER_EB, tb), BF16)],
        compiler_params=_cparams("arbitrary"),
        name="peer_experts",
    )(x, xn, th, s2, e1, e2, e_u, e_vt, g_final)


def _s5_params(a_re, a_im, b_re, b_im, c_re, c_im, log_dt):
    dt = jnp.exp(log_dt)[:, None]
    mag = jnp.exp(dt * a_re)
    abar_re = mag * jnp.cos(dt * a_im)
    abar_im = mag * jnp.sin(dt * a_im)
    nr, ni = abar_re - 1.0, abar_im
    den = a_re * a_re + a_im * a_im
    coef_re = (nr * a_re + ni * a_im) / den
    coef_im = (ni * a_re - nr * a_im) / den
    bbar_re = coef_re[..., None] * b_re - coef_im[..., None] * b_im
    bbar_im = coef_re[..., None] * b_im + coef_im[..., None] * b_re
    eye = jnp.eye(SSM_GROUPS, dtype=F32)
    expand_b = lambda b: jnp.einsum('gpc,gh->gchp', b, eye).reshape(SSM_WIDTH, SSM_CH)
    expand_c = lambda c: jnp.einsum('gcp,gh->gphc', c, eye).reshape(SSM_CH, SSM_WIDTH)
    bmat = jnp.concatenate([expand_b(bbar_re), expand_b(bbar_im)], axis=1).astype(BF16)
    cmat = jnp.stack([expand_c(c_re), -expand_c(c_im)]).astype(BF16)
    abar = jnp.stack([abar_re.reshape(SSM_CH), abar_im.reshape(SSM_CH)])
    return bmat, abar, cmat


def _swap_halves(w):
    half = ROPE_DIM // 2
    return jnp.concatenate([w[..., half:], w[..., :half]], axis=-1)


def _rope_table(pos):
    half = ROPE_DIM // 2
    inv = 1.0 / (ROPE_THETA ** (jnp.arange(half, dtype=F32) / half))
    ang = pos[:, None] * inv[None, :]
    cos, sin = jnp.cos(ang), jnp.sin(ang)
    return jnp.concatenate([cos, cos, -sin, sin], axis=1)


def kernel(x_prompt, x_sample, cache_ckv, cache_kpe, state_conv, state_ssm_re, state_ssm_im, meta_tokens, norm_mix, norm_ffn, w_in, b_gate, conv_w, conv_b, w_a_out, q_norm, w_uq, kv_norm, w_ukv, w_b_out, ssm_a_re, ssm_a_im, ssm_b_re, ssm_b_im, ssm_c_re, ssm_c_im, ssm_log_dt, ssm_d, w_glu, w_o, peer_wq, peer_k1, peer_k2, peer_u, peer_v, norm_final):
    B, seq, _ = x_prompt.shape
    assert B == 1
    nb, dec_seq, _ = x_sample.shape
    ns = nb * dec_seq
    past = cache_ckv.shape[2]
    depth = w_in.shape[0]
    p_end = FRONT + seq
    R = -(-(p_end + ns) // ROW_ALIGN) * ROW_ALIGN
    s0 = R - ns
    lay = dict(ns=ns, dec_seq=dec_seq, p_end=p_end, s0=s0)

    x = jnp.concatenate([
        jnp.zeros((FRONT - N_META, D_MODEL), F32), meta_tokens, x_prompt[0],
        jnp.zeros((s0 - p_end, D_MODEL), F32), x_sample.reshape(ns, D_MODEL)], axis=0)
    pos = jnp.concatenate([
        jnp.arange(s0, dtype=F32) - (FRONT - N_META),
        jnp.tile(past + jnp.arange(dec_seq, dtype=F32), nb)])
    cs = _rope_table(pos)

    outs = [[] for _ in range(10)]
    for l in range(depth):
        wl = w_in[l]
        kp = wl[:, 2560:2624]
        w_small = jnp.concatenate([wl[:, 0:2560], wl[:, 2624:3136], kp, _swap_halves(kp)], axis=1).astype(BF16)
        w_gate = wl[:, 3136:].astype(BF16)
        wq3 = w_uq[l].reshape(Q_LORA, MLA_HEADS, QK_DIM)
        wq_aug = jnp.concatenate([wq3, _swap_halves(wq3[..., NOPE_DIM:])], axis=-1).reshape(Q_LORA, MLA_HEADS * 256)
        wq_aug = wq_aug.astype(BF16)
        wkv = w_ukv[l].astype(BF16)
        wkv3 = wkv.reshape(KV_LORA, MLA_HEADS, NOPE_DIM + V_DIM)
        w_k = wkv3[..., :NOPE_DIM].reshape(KV_LORA, MLA_HEADS * NOPE_DIM)
        w_vt = wkv3[..., NOPE_DIM:].transpose(1, 2, 0)
        bmat, abar, cmat = _s5_params(ssm_a_re[l], ssm_a_im[l], ssm_b_re[l], ssm_b_im[l], ssm_c_re[l],
                                      ssm_c_im[l], ssm_log_dt[l])
        h0 = jnp.stack([state_ssm_re[l].reshape(nb, SSM_CH), state_ssm_im[l].reshape(nb, SSM_CH)])

        proj = _in_proj(x, norm_mix[l][None], w_small)
        za, p_conv, s_conv = _conv(proj, conv_w[l], conv_b[l][None], state_conv[l].reshape(2 * nb, A_WIDTH), lay)
        q, k, vt, lat, kpe = _qkv(proj, cs, q_norm[l][None], wq_aug, kv_norm[l][None], w_k, w_vt)
        o = _flash(q, k, vt)
        lat_all = jnp.concatenate([cache_ckv[l], lat[s0:].reshape(nb, dec_seq, KV_LORA)], axis=1).astype(BF16)
        kpe_all = jnp.concatenate([cache_kpe[l], kpe[s0:].reshape(nb, dec_seq, ROPE_DIM)], axis=1).astype(BF16)
        o = _cached_attn(q, lat_all, kpe_all, wkv, o, lay)
        yc, p_state, s_state = _s5(proj, bmat, abar, cmat, ssm_d[l][None], h0, lay)
        x = _merge(x, norm_mix[l][None], za, o, yc, w_gate, b_gate[l][None], w_a_out[l].astype(BF16),
                   w_b_out[l].astype(BF16), w_glu[l].astype(BF16), w_o[l].astype(BF16))
        xn2, th, s2, e1, e2 = _route(x, norm_ffn[l][None], peer_wq[l].astype(BF16),
                                     peer_k1[l].astype(BF16), peer_k2[l].astype(BF16))
        x = _experts(x, xn2, th, s2, e1, e2, peer_u[l].astype(BF16), peer_v[l].astype(BF16),
                     norm_final[None], final_norm=(l == depth - 1))

        lo = FRONT - N_META
        outs[0].append(lat[lo:p_end][None])
        outs[1].append(kpe[lo:p_end][None])
        outs[2].append(p_conv[None])
        outs[3].append(p_state[0].reshape(1, SSM_GROUPS, SSM_STATE))
        outs[4].append(p_state[1].reshape(1, SSM_GROUPS, SSM_STATE))
        outs[5].append(lat[s0:].reshape(nb, dec_seq, KV_LORA))
        outs[6].append(kpe[s0:].reshape(nb, dec_seq, ROPE_DIM))
        outs[7].append(s_conv.reshape(nb, CONV_WIDTH - 1, A_WIDTH))
        outs[8].append(s_state[0].reshape(nb, SSM_GROUPS, SSM_STATE))
        outs[9].append(s_state[1].reshape(nb, SSM_GROUPS, SSM_STATE))

    y_prompt = x[FRONT:p_end][None]
    y_sample = x[s0:].reshape(nb, dec_seq, D_MODEL)
    return (y_prompt, y_sample) + tuple(jnp.stack(o) for o in outs)
```

```python
import functools
import math

import jax
import jax.numpy as jnp
from jax import lax
from jax.experimental import pallas as pl
from jax.experimental.pallas import tpu as pltpu

F32 = jnp.float32
BF16 = jnp.bfloat16

D_MODEL = 2048
CHUNK = 64
N_META = 16
EPS = 1e-6
NEG_INF = -1e30
A_WIDTH = 512
CONV_WIDTH = 3
MLA_HEADS = 8
Q_LORA = 512
KV_LORA = 512
NOPE_DIM = 128
ROPE_DIM = 64
V_DIM = 128
QK_DIM = NOPE_DIM + ROPE_DIM
ROPE_THETA = 10000.0
ATTN_SCALE = 1.0 / math.sqrt(NOPE_DIM + ROPE_DIM)
SSM_GROUP = 16
SSM_GROUPS = 32
SSM_WIDTH = SSM_GROUP * SSM_GROUPS
SSM_STATE = 64
SSM_CH = SSM_GROUPS * SSM_STATE
PEER_HEADS = 8
PEER_KEYS = 128
PEER_QDIM = 256
PEER_TOPK = 16

FRONT = CHUNK
ROW_ALIGN = 512
VMEM_LIMIT = 60 * 1024 * 1024

C_AB, C_AC, C_AH, C_Q, C_KV, C_U, C_KPE, N_SMALL = 0, 512, 1024, 1536, 2048, 2560, 3072, 3200


def _cparams(*sem):
    return pltpu.CompilerParams(dimension_semantics=sem, vmem_limit_bytes=VMEM_LIMIT)


def _rms(x, g):
    return x * lax.rsqrt(jnp.mean(x * x, axis=-1, keepdims=True) + EPS) * g


def _sigmoid(x):
    return 1.0 / (1.0 + jnp.exp(-x))


def _gelu(x):
    return 0.5 * x * (1.0 + jnp.tanh(math.sqrt(2.0 / math.pi) * (x + 0.044715 * (x * x * x))))


def _dot(a, b):
    return jnp.dot(a, b, preferred_element_type=F32)


def _dot_nt(a, b):
    return lax.dot_general(a, b, (((1,), (1,)), ((), ())), preferred_element_type=F32)


def _in_proj_kernel(x_ref, g_ref, w_ref, o_ref):
    xn = _rms(x_ref[...], g_ref[...]).astype(BF16)
    o_ref[...] = _dot(xn, w_ref[...])


def _in_proj(x, g, w_small):
    R = x.shape[0]
    tb = 256
    return pl.pallas_call(
        _in_proj_kernel,
        grid=(R // tb,),
        in_specs=[pl.BlockSpec((tb, D_MODEL), lambda i: (i, 0)),
                  pl.BlockSpec((1, D_MODEL), lambda i: (0, 0)),
                  pl.BlockSpec((D_MODEL, N_SMALL), lambda i: (0, 0))],
        out_specs=pl.BlockSpec((tb, N_SMALL), lambda i: (i, 0)),
        out_shape=jax.ShapeDtypeStruct((R, N_SMALL), F32),
        compiler_params=_cparams("parallel"),
        name="in_proj",
    )(x, g, w_small)


def _conv_kernel(p_ref, w_ref, b_ref, st_ref, za_ref, pc_ref, sc_ref, zbuf, *, p_blk, p_off, n_blk, tb, dec_seq):
    i = pl.program_id(0)
    ab = p_ref[:, C_AB:C_AB + A_WIDTH]
    z = p_ref[:, C_AC:C_AC + A_WIDTH] * p_ref[:, C_AH:C_AH + A_WIDTH]
    w0, w1, w2 = w_ref[0:1, :], w_ref[1:2, :], w_ref[2:3, :]
    b = b_ref[...]

    @pl.when(i == 0)
    def _():
        zbuf[0:8, :] = jnp.zeros((8, A_WIDTH), F32)

    @pl.when(i < n_blk - 1)
    def _():
        zbuf[8:8 + tb, :] = z
        y = b + w0 * zbuf[6:6 + tb, :] + w1 * zbuf[7:7 + tb, :] + w2 * zbuf[8:8 + tb, :]
        za_ref[...] = (ab * y).astype(BF16)
        zbuf[0:8, :] = zbuf[tb:tb + 8, :]

    @pl.when(i == p_blk)
    def _():
        pc_ref[...] = z[p_off - 1:p_off + 1, :]

    @pl.when(i == n_blk - 1)
    def _():
        for s in range(tb // dec_seq):
            r0 = s * dec_seq
            zbuf[6:8, :] = st_ref[2 * s:2 * s + 2, :]
            zbuf[8:8 + dec_seq, :] = z[r0:r0 + dec_seq, :]
            y = (b + w0 * zbuf[6:6 + dec_seq, :] + w1 * zbuf[7:7 + dec_seq, :]
                 + w2 * zbuf[8:8 + dec_seq, :])
            za_ref[r0:r0 + dec_seq, :] = (ab[r0:r0 + dec_seq, :] * y).astype(BF16)
            sc_ref[2 * s:2 * s + 2, :] = z[r0 + dec_seq - 2:r0 + dec_seq, :]


def _conv(proj, conv_w, conv_b, state, lay):
    R = proj.shape[0]
    tb = lay["ns"]
    n_blk = R // tb
    last = lay["p_end"] - 1
    kern = functools.partial(_conv_kernel, p_blk=last // tb, p_off=last % tb, n_blk=n_blk, tb=tb,
                             dec_seq=lay["dec_seq"])
    assert last % tb >= 1
    nseq = tb // lay["dec_seq"]
    return pl.pallas_call(
        kern,
        grid=(n_blk,),
        in_specs=[pl.BlockSpec((tb, 3 * A_WIDTH), lambda i: (i, 0)),
                  pl.BlockSpec((CONV_WIDTH, A_WIDTH), lambda i: (0, 0)),
                  pl.BlockSpec((1, A_WIDTH), lambda i: (0, 0)),
                  pl.BlockSpec((2 * nseq, A_WIDTH), lambda i: (0, 0))],
        out_specs=[pl.BlockSpec((tb, A_WIDTH), lambda i: (i, 0)),
                   pl.BlockSpec((2, A_WIDTH), lambda i: (0, 0)),
                   pl.BlockSpec((2 * nseq, A_WIDTH), lambda i: (0, 0))],
        out_shape=[jax.ShapeDtypeStruct((R, A_WIDTH), BF16),
                   jax.ShapeDtypeStruct((2, A_WIDTH), F32),
                   jax.ShapeDtypeStruct((2 * nseq, A_WIDTH), F32)],
        scratch_shapes=[pltpu.VMEM((tb + 8, A_WIDTH), F32)],
        compiler_params=_cparams("arbitrary"),
        name="short_conv",
    )(proj, conv_w, conv_b, state)


QK_PAD = 256
Q_SCALE = ATTN_SCALE * math.log2(math.e)


V_ROWS = V_DIM + 16


def _qkv_kernel(cq_ref, ckv_ref, kp_ref, cs_ref, qn_ref, wq_ref, kn_ref, wk_ref, wvt_ref,
                q_ref, k_ref, vt_ref, lat_ref, kpe_ref, *, tb):
    cs = cs_ref[...]
    qc = _rms(cq_ref[...], qn_ref[...]).astype(BF16)
    q = _dot(qc, wq_ref[...])
    lat = _rms(ckv_ref[...], kn_ref[...])
    lat_ref[...] = lat
    lat_b = lat.astype(BF16)
    kn = _dot(lat_b, wk_ref[...])
    ones_row = jnp.where(lax.broadcasted_iota(jnp.int32, (V_ROWS - V_DIM, tb), 0) == 0, 1.0, 0.0).astype(BF16)
    lane = lax.broadcasted_iota(jnp.int32, (tb, 2 * ROPE_DIM), 1)
    row = pl.program_id(0) * tb + lax.broadcasted_iota(jnp.int32, (tb, 2 * ROPE_DIM), 0)
    q_tail = jnp.where(lane == ROPE_DIM, 1.0, 0.0)
    k_tail = jnp.where((lane == ROPE_DIM) & (row < FRONT - N_META), NEG_INF, 0.0)
    t = kp_ref[...] * cs
    kpe = t + pltpu.roll(t, ROPE_DIM, 1)
    kpe_ref[...] = kpe[:, 0:ROPE_DIM]
    k_hi = jnp.where(lane < ROPE_DIM, kpe, k_tail).astype(BF16)
    for h in range(MLA_HEADS):
        c = h * 256
        q_ref[h, :, 0:NOPE_DIM] = (q[:, c:c + NOPE_DIM] * Q_SCALE).astype(BF16)
        t = q[:, c + NOPE_DIM:c + 256] * cs
        qpe = (t + pltpu.roll(t, ROPE_DIM, 1)) * Q_SCALE
        q_ref[h, :, NOPE_DIM:QK_PAD] = jnp.where(lane < ROPE_DIM, qpe, q_tail).astype(BF16)
        k_ref[h, :, 0:NOPE_DIM] = kn[:, h * NOPE_DIM:(h + 1) * NOPE_DIM].astype(BF16)
        k_ref[h, :, NOPE_DIM:QK_PAD] = k_hi
        vt_ref[h, 0:V_DIM, :] = _dot_nt(wvt_ref[h], lat_b).astype(BF16)
        vt_ref[h, V_DIM:V_ROWS, :] = ones_row


def _qkv(proj, cs, q_norm, wq_aug, kv_norm, w_k, w_vt):
    R = proj.shape[0]
    tb = 256
    H = MLA_HEADS
    return pl.pallas_call(
        functools.partial(_qkv_kernel, tb=tb),
        grid=(R // tb,),
        in_specs=[pl.BlockSpec((tb, Q_LORA), lambda i: (i, C_Q // Q_LORA)),
                  pl.BlockSpec((tb, KV_LORA), lambda i: (i, C_KV // KV_LORA)),
                  pl.BlockSpec((tb, 2 * ROPE_DIM), lambda i: (i, C_KPE // (2 * ROPE_DIM))),
                  pl.BlockSpec((tb, 2 * ROPE_DIM), lambda i: (i, 0)),
                  pl.BlockSpec((1, Q_LORA), lambda i: (0, 0)),
                  pl.BlockSpec((Q_LORA, H * 256), lambda i: (0, 0)),
                  pl.BlockSpec((1, KV_LORA), lambda i: (0, 0)),
                  pl.BlockSpec((KV_LORA, H * NOPE_DIM), lambda i: (0, 0)),
                  pl.BlockSpec((H, V_DIM, KV_LORA), lambda i: (0, 0, 0))],
        out_specs=[pl.BlockSpec((H, tb, QK_PAD), lambda i: (0, i, 0)),
                   pl.BlockSpec((H, tb, QK_PAD), lambda i: (0, i, 0)),
                   pl.BlockSpec((H, V_ROWS, tb), lambda i: (0, 0, i)),
                   pl.BlockSpec((tb, KV_LORA), lambda i: (i, 0)),
                   pl.BlockSpec((tb, ROPE_DIM), lambda i: (i, 0))],
        out_shape=[jax.ShapeDtypeStruct((H, R, QK_PAD), BF16),
                   jax.ShapeDtypeStruct((H, R, QK_PAD), BF16),
                   jax.ShapeDtypeStruct((H, V_ROWS, R), BF16),
                   jax.ShapeDtypeStruct((R, KV_LORA), F32),
                   jax.ShapeDtypeStruct((R, ROPE_DIM), F32)],
        compiler_params=_cparams("parallel"),
        name="qkv_rope",
    )(proj, proj, proj, cs, q_norm, wq_aug, kv_norm, w_k, w_vt)


FLASH_TQ = 512
FLASH_TK = 512


def _flash_kernel(it_ref, jt_ref, q_ref, k_ref, vt_ref, o_ref, m_ref, acc_ref, *, bq):
    tq, tk = FLASH_TQ, FLASH_TK
    t = pl.program_id(1)
    i = it_ref[t]
    j = jt_ref[t]

    @pl.when(j == 0)
    def _():
        m_ref[...] = jnp.full(m_ref.shape, NEG_INF, F32)
        acc_ref[...] = jnp.zeros(acc_ref.shape, F32)

    def run(diag):
        tiles = [(a, b) for a in range(bq // tq) for b in range(bq // tk)
                 if not (diag and (b * tk) // CHUNK > (a * tq + tq - 1) // CHUNK)]

        def scores(a, b):
            return _dot_nt(k_ref[b * tk:(b + 1) * tk, :], q_ref[a * tq:(a + 1) * tq, :])

        tiles.sort(key=lambda ab: (ab[1], ab[0]))
        cols = {a: slice(a * tq, (a + 1) * tq) for a, _ in tiles}
        m = {a: m_ref[:, c] for a, c in cols.items()}
        acc = {a: acc_ref[:, c] for a, c in cols.items()}
        s_next = scores(*tiles[0])
        for n, (a, b) in enumerate(tiles):
            s = s_next
            if n + 1 < len(tiles):
                s_next = scores(*tiles[n + 1])
            if diag and (b * tk + tk - 1) // CHUNK > (a * tq) // CHUNK:
                kc = (b * tk + lax.broadcasted_iota(jnp.int32, (tk, tq), 0)) // CHUNK
                qc = (a * tq + lax.broadcasted_iota(jnp.int32, (tk, tq), 1)) // CHUNK
                s = jnp.where(kc <= qc, s, NEG_INF)
            m_new = jnp.maximum(m[a], jnp.max(s, axis=0, keepdims=True))
            alpha = jnp.exp2(m[a] - m_new)
            p = jnp.exp2(s - m_new).astype(BF16)
            acc[a] = alpha * acc[a] + _dot(vt_ref[:, b * tk:(b + 1) * tk], p)
            m[a] = m_new
        for a, c in cols.items():
            if diag:
                row = i * bq + a * tq + lax.broadcasted_iota(jnp.int32, (V_DIM, tq), 1)
                o = jnp.where(row >= FRONT - N_META, acc[a][0:V_DIM, :] / acc[a][V_DIM:V_DIM + 1, :], 0.0)
                o_ref[c, :] = o.T.astype(o_ref.dtype)
            else:
                m_ref[:, c], acc_ref[:, c] = m[a], acc[a]

    @pl.when(j < i)
    def _():
        run(False)

    @pl.when(j == i)
    def _():
        run(True)


def _flash(q, k, vt):
    H, R, _ = q.shape
    bq = next(b for b in (1536, 1024, 512) if R % b == 0)
    n = R // bq
    pairs = [(i, j) for i in range(n) for j in range(i + 1)]
    it = jnp.array([p[0] for p in pairs], jnp.int32)
    jt = jnp.array([p[1] for p in pairs], jnp.int32)
    grid_spec = pltpu.PrefetchScalarGridSpec(
        num_scalar_prefetch=2,
        grid=(H, len(pairs)),
        in_specs=[pl.BlockSpec((None, bq, QK_PAD), lambda h, t, it, jt: (h, it[t], 0)),
                  pl.BlockSpec((None, bq, QK_PAD), lambda h, t, it, jt: (h, jt[t], 0)),
                  pl.BlockSpec((None, V_ROWS, bq), lambda h, t, it, jt: (h, 0, jt[t]))],
        out_specs=pl.BlockSpec((bq, V_DIM), lambda h, t, it, jt: (it[t], h)),
        scratch_shapes=[pltpu.VMEM((1, bq), F32), pltpu.VMEM((V_ROWS, bq), F32)],
    )
    return pl.pallas_call(
        functools.partial(_flash_kernel, bq=bq),
        grid_spec=grid_spec,
        out_shape=jax.ShapeDtypeStruct((R, H * V_DIM), BF16),
        compiler_params=_cparams("parallel", "arbitrary"),
        name="prompt_attention",
    )(it, jt, q, k, vt)


def _cached_attn_kernel(q_ref, lat_ref, kpe_ref, w_ref, o_all_ref, o_ref):
    del o_all_ref
    kv = _dot(lat_ref[...], w_ref[...]).astype(BF16)
    kpe = kpe_ref[...]
    for h in range(MLA_HEADS):
        c = h * 256
        qh = q_ref[h]
        s = _dot_nt(qh[:, 0:NOPE_DIM], kv[:, c:c + NOPE_DIM]) + _dot_nt(qh[:, NOPE_DIM:QK_DIM], kpe)
        m = jnp.max(s, axis=1, keepdims=True)
        p = jnp.exp2(s - m)
        p = p / jnp.sum(p, axis=1, keepdims=True)
        o_ref[:, h * V_DIM:(h + 1) * V_DIM] = _dot(p.astype(BF16), kv[:, c + NOPE_DIM:c + 256]).astype(o_ref.dtype)


def _cached_attn(q, lat_all, kpe_all, w_ukv, o_all, lay):
    H = MLA_HEADS
    B, Lk, _ = lat_all.shape
    S = lay["dec_seq"]
    blk0 = lay["s0"] // S
    return pl.pallas_call(
        _cached_attn_kernel,
        grid=(B,),
        in_specs=[pl.BlockSpec((H, S, QK_PAD), lambda b: (0, blk0 + b, 0)),
                  pl.BlockSpec((None, Lk, KV_LORA), lambda b: (b, 0, 0)),
                  pl.BlockSpec((None, Lk, ROPE_DIM), lambda b: (b, 0, 0)),
                  pl.BlockSpec((KV_LORA, H * 256), lambda b: (0, 0)),
                  pl.BlockSpec(memory_space=pl.ANY)],
        out_specs=pl.BlockSpec((S, H * V_DIM), lambda b: (blk0 + b, 0)),
        out_shape=jax.ShapeDtypeStruct(o_all.shape, o_all.dtype),
        input_output_aliases={4: 0},
        compiler_params=_cparams("parallel"),
        name="sample_attention",
    )(q, lat_all, kpe_all, w_ukv, o_all)


def _s5_kernel(u_ref, bm_ref, a_ref, cm_ref, d_ref, h0_ref, y_ref, ps_ref, ss_ref, hre, him, carry,
               *, p_blk, p_off, n_blk, tb, dec_seq):
    i = pl.program_id(0)
    u = u_ref[...]
    hb = _dot(u.astype(BF16), bm_ref[...])
    hre[...] = hb[:, 0:SSM_CH]
    him[...] = hb[:, SSM_CH:2 * SSM_CH]
    ar = a_ref[0:1, :]
    ai = a_ref[1:2, :]

    def scan(start, n, hr, hi):
        def body(t, c):
            hr, hi = c
            r = start + t
            nr = ar * hr - ai * hi + hre[pl.ds(r, 1), :]
            ni = ar * hi + ai * hr + him[pl.ds(r, 1), :]
            hre[pl.ds(r, 1), :] = nr
            him[pl.ds(r, 1), :] = ni
            return nr, ni
        return lax.fori_loop(0, n, body, (hr, hi), unroll=2)

    @pl.when(i == 0)
    def _():
        carry[...] = jnp.zeros(carry.shape, F32)

    @pl.when(i < n_blk - 1)
    def _():
        hr, hi = scan(0, tb, carry[0:1, :], carry[1:2, :])
        carry[0:1, :] = hr
        carry[1:2, :] = hi

    @pl.when(i == p_blk)
    def _():
        ps_ref[0:1, :] = hre[p_off:p_off + 1, :]
        ps_ref[1:2, :] = him[p_off:p_off + 1, :]

    @pl.when(i == n_blk - 1)
    def _():
        for s in range(tb // dec_seq):
            hr, hi = scan(s * dec_seq, dec_seq, h0_ref[0, s:s + 1, :], h0_ref[1, s:s + 1, :])
            ss_ref[0, s:s + 1, :] = hr
            ss_ref[1, s:s + 1, :] = hi

    y = _dot(hre[...].astype(BF16), cm_ref[0]) + _dot(him[...].astype(BF16), cm_ref[1])
    y_ref[...] = _gelu(y + d_ref[...] * u).astype(BF16)


def _s5(proj, bmat, abar, cmat, d, h0, lay):
    R = proj.shape[0]
    tb = lay["ns"]
    n_blk = R // tb
    last = lay["p_end"] - 1
    nseq = tb // lay["dec_seq"]
    kern = functools.partial(_s5_kernel, p_blk=last // tb, p_off=last % tb, n_blk=n_blk, tb=tb,
                             dec_seq=lay["dec_seq"])
    return pl.pallas_call(
        kern,
        grid=(n_blk,),
        in_specs=[pl.BlockSpec((tb, SSM_WIDTH), lambda i: (i, C_U // SSM_WIDTH)),
                  pl.BlockSpec((SSM_WIDTH, 2 * SSM_CH), lambda i: (0, 0)),
                  pl.BlockSpec((2, SSM_CH), lambda i: (0, 0)),
                  pl.BlockSpec((2, SSM_CH, SSM_WIDTH), lambda i: (0, 0, 0)),
                  pl.BlockSpec((1, SSM_WIDTH), lambda i: (0, 0)),
                  pl.BlockSpec((2, nseq, SSM_CH), lambda i: (0, 0, 0))],
        out_specs=[pl.BlockSpec((tb, SSM_WIDTH), lambda i: (i, 0)),
                   pl.BlockSpec((2, SSM_CH), lambda i: (0, 0)),
                   pl.BlockSpec((2, nseq, SSM_CH), lambda i: (0, 0, 0))],
        out_shape=[jax.ShapeDtypeStruct((R, SSM_WIDTH), BF16),
                   jax.ShapeDtypeStruct((2, SSM_CH), F32),
                   jax.ShapeDtypeStruct((2, nseq, SSM_CH), F32)],
        scratch_shapes=[pltpu.VMEM((tb, SSM_CH), F32), pltpu.VMEM((tb, SSM_CH), F32),
                        pltpu.VMEM((2, SSM_CH), F32)],
        compiler_params=_cparams("arbitrary"),
        name="s5_scan",
    )(proj, bmat, abar, cmat, d, h0)


MERGE_TN = 256
MERGE_NC = D_MODEL // MERGE_TN


def _merge_kernel(x_ref, g_ref, za_ref, o_ref, yc_ref, wg0_ref, wg1_ref, wg2_ref, b0_ref, b1_ref, b2_ref,
                  wa_ref, wb_ref, wga_ref, wgb_ref, wo_ref, out_ref, xn_s, mg_s):
    c = pl.program_id(1)

    @pl.when(c == 0)
    def _():
        xn_s[...] = _rms(x_ref[...], g_ref[...]).astype(BF16)

    xn = xn_s[...]
    g0 = _sigmoid(_dot(xn, wg0_ref[...]) + b0_ref[...])
    g1 = _sigmoid(_dot(xn, wg1_ref[...]) + b1_ref[...])
    g2 = _sigmoid(_dot(xn, wg2_ref[...]) + b2_ref[...])
    yc = yc_ref[...]
    y_a = _dot(za_ref[...], wa_ref[...])
    y_b = _dot(o_ref[...], wb_ref[...])
    y_c = _dot(yc, wga_ref[...]) * _sigmoid(_dot(yc, wgb_ref[...]))
    mg_s[c] = (g0 * y_a + g1 * y_b + g2 * y_c).astype(BF16)

    @pl.when(c == MERGE_NC - 1)
    def _():
        acc = x_ref[...]
        for cc in range(MERGE_NC):
            acc = acc + _dot(mg_s[cc], wo_ref[cc * MERGE_TN:(cc + 1) * MERGE_TN, :])
        out_ref[...] = acc


def _merge(x, g, za, o, yc, w_gate, b_gate, w_a, w_b, w_glu, w_o):
    R = x.shape[0]
    tb = 512
    tn, nc = MERGE_TN, MERGE_NC
    row = lambda w: pl.BlockSpec((tb, w), lambda i, c: (i, 0))
    col = lambda k, off: pl.BlockSpec((k, tn), lambda i, c: (0, off + c))
    return pl.pallas_call(
        _merge_kernel,
        grid=(R // tb, nc),
        in_specs=[row(D_MODEL), pl.BlockSpec((1, D_MODEL), lambda i, c: (0, 0)),
                  row(A_WIDTH), row(MLA_HEADS * V_DIM), row(SSM_WIDTH),
                  col(D_MODEL, 0), col(D_MODEL, nc), col(D_MODEL, 2 * nc),
                  col(1, 0), col(1, nc), col(1, 2 * nc),
                  col(A_WIDTH, 0), col(MLA_HEADS * V_DIM, 0), col(SSM_WIDTH, 0), col(SSM_WIDTH, nc),
                  pl.BlockSpec((D_MODEL, D_MODEL), lambda i, c: (0, 0))],
        out_specs=row(D_MODEL),
        out_shape=jax.ShapeDtypeStruct((R, D_MODEL), F32),
        scratch_shapes=[pltpu.VMEM((tb, D_MODEL), BF16), pltpu.VMEM((nc, tb, tn), BF16)],
        compiler_params=_cparams("parallel", "arbitrary"),
        name="branch_merge",
    )(x, g, za, o, yc, w_gate, w_gate, w_gate, b_gate, b_gate, b_gate, w_a, w_b, w_glu, w_glu, w_o)


SUBLANES = 8


def _sort_desc(v):
    n = len(v)
    v = list(v)
    k = 2
    while k <= n:
        j = k // 2
        while j >= 1:
            for i in range(n):
                l = i ^ j
                if l > i:
                    hi, lo = jnp.maximum(v[i], v[l]), jnp.minimum(v[i], v[l])
                    v[i], v[l] = (hi, lo) if (i & k) == 0 else (lo, hi)
            j //= 2
        k *= 2
    return v


def _merge_top(a, b):
    n = len(a)
    v = [jnp.maximum(a[i], b[n - 1 - i]) for i in range(n)]
    j = n // 2
    while j >= 1:
        for i in range(n):
            l = i ^ j
            if l > i:
                v[i], v[l] = jnp.maximum(v[i], v[l]), jnp.minimum(v[i], v[l])
        j //= 2
    return v


def _top_desc(tiles):
    v = _sort_desc(tiles)
    shift = SUBLANES // 2
    while shift >= 1:
        v = _merge_top(v, [pltpu.roll(x, shift, 0) for x in v])
        shift //= 2
    return v


def _pack_sublanes(rows, sub):
    out = rows[0]
    for r in range(1, len(rows)):
        out = jnp.where(sub == r, rows[r], out)
    return out


def _route_kernel(x_ref, g_ref, wq_ref, k1_ref, k2_ref, xnt_ref, th_ref, s2_ref, e1_ref, e2_ref, *, tb):
    K, NK, S = PEER_TOPK, PEER_KEYS, SUBLANES
    xn32 = _rms(x_ref[...], g_ref[...])
    xnt_ref[...] = xn32.T.astype(BF16)
    xn = xn32.astype(BF16)
    q = _dot(xn, wq_ref[...]).astype(BF16)
    half = PEER_QDIM // 2
    sub = lax.broadcasted_iota(jnp.int32, (S, tb), 0)
    neg = jnp.full((S, tb), -jnp.inf, F32)
    for h in range(PEER_HEADS):
        c = h * PEER_QDIM
        s1 = _dot_nt(k1_ref[...], q[:, c:c + half])
        s2 = _dot_nt(k2_ref[...], q[:, c + half:c + PEER_QDIM])
        t1 = [s1[S * r:S * (r + 1), :] for r in range(NK // S)]
        t2 = [s2[S * r:S * (r + 1), :] for r in range(NK // S)]
        v1 = _top_desc(t1)
        v2 = _top_desc(t2)
        v2_lo, v2_hi = _pack_sublanes(v2[0:S], sub), _pack_sublanes(v2[S:K], sub)
        v1_hi = _pack_sublanes(v1[S:K], sub)
        cand = [v1[0] + v2_lo, v1[0] + v2_hi] + [v1[a] + v2_lo for a in range(1, S)] + [v1_hi + v2[0]]
        vals = _top_desc(cand + [neg] * (K - len(cand)))
        tau = vals[K - 1]
        z = 1.0 + jnp.exp(vals[1] - vals[0])
        for r in range(2, K):
            z = z + jnp.exp(vals[r] - vals[0])
        scale = 0.5 / z
        for r in range(NK // S):
            rows = slice(S * r, S * (r + 1))
            th = jnp.full((S, tb), jnp.inf, F32)
            for b in range(K):
                th = jnp.where(t1[r] + v2[b] >= tau, v2[b], th)
            th_ref[h, rows, :] = th
            e1_ref[h, rows, :] = jnp.exp(t1[r] - v1[0]) * scale
            e2_ref[h, rows, :] = jnp.exp(t2[r] - v2[0])
        s2_ref[h] = s2


def _route(x, g, wq, k1, k2):
    R = x.shape[0]
    tb = 128
    H, NK = PEER_HEADS, PEER_KEYS
    tmap = pl.BlockSpec((H, None, NK, tb), lambda i: (0, i, 0, 0))
    tshape = jax.ShapeDtypeStruct((H, R // tb, NK, tb), F32)
    return pl.pallas_call(
        functools.partial(_route_kernel, tb=tb),
        grid=(R // tb,),
        in_specs=[pl.BlockSpec((tb, D_MODEL), lambda i: (i, 0)),
                  pl.BlockSpec((1, D_MODEL), lambda i: (0, 0)),
                  pl.BlockSpec((D_MODEL, H * PEER_QDIM), lambda i: (0, 0)),
                  pl.BlockSpec((NK, PEER_QDIM // 2), lambda i: (0, 0)),
                  pl.BlockSpec((NK, PEER_QDIM // 2), lambda i: (0, 0))],
        out_specs=[pl.BlockSpec((D_MODEL, tb), lambda i: (0, i)), tmap, tmap, tmap, tmap],
        out_shape=[jax.ShapeDtypeStruct((D_MODEL, R), BF16), tshape, tshape, tshape, tshape],
        compiler_params=_cparams("parallel"),
        name="peer_route",
    )(x, g, wq, k1, k2)


PEER_SUB = 8
PEER_EB = PEER_SUB * PEER_KEYS
PEER_TC = 128
PEER_OUT_ROWS = 128
PEER_ACT_ROWS = 128
GELU_C1 = math.sqrt(2.0 / math.pi)
GELU_C2 = GELU_C1 * 0.044715


def _expert_kernel(x_ref, xnt_ref, th_ref, s2_ref, e1_ref, e2_ref, eu_ref, evt_ref, gf_ref, out_ref, acc_ref,
                   act_even, act_odd, p_even, p_odd, *, tb, ne, final_norm):
    s = pl.program_id(0)
    NK = PEER_KEYS
    proj_first = (s < 2) | ((s - 2) % ne == 0)
    proj_last = (s >= 2) & ((s - 2) % ne == ne - 1)

    @pl.when(s == 0)
    def _():
        act_odd[...] = jnp.zeros(act_odd.shape, F32)
        p_odd[...] = jnp.zeros(p_odd.shape, BF16)

    @pl.when(proj_first)
    def _():
        acc_ref[...] = jnp.zeros(acc_ref.shape, F32)

    n_piece = PEER_SUB * (tb // PEER_TC)
    n_out = D_MODEL // PEER_OUT_ROWS

    n_act = PEER_EB // PEER_ACT_ROWS

    def gate_and_project(p_new, p_old, act_new, act_old):
        for piece in range(n_piece):
            if piece < n_piece // 2 and piece % (n_piece // 2 // n_out) == 0:
                c = piece // (n_piece // 2 // n_out)
                orow = slice(c * PEER_OUT_ROWS, (c + 1) * PEER_OUT_ROWS)
                acc_ref[orow, :] += _dot(evt_ref[orow, :], p_old[...])
            if piece >= n_piece // 2 and piece % (n_piece // 2 // n_act) == 0:
                c = (piece - n_piece // 2) // (n_piece // 2 // n_act)
                erow = slice(c * PEER_ACT_ROWS, (c + 1) * PEER_ACT_ROWS)
                act = _dot(eu_ref[erow, :], xnt_ref[...])
                for t in range(tb // PEER_TC):
                    act_new[t, erow, :] = act[:, t * PEER_TC:(t + 1) * PEER_TC]
            sub, tc = divmod(piece, tb // PEER_TC)
            rows = slice(sub * NK, (sub + 1) * NK)
            lanes = slice(tc * PEER_TC, (tc + 1) * PEER_TC)
            w = jnp.zeros((NK, PEER_TC), F32)
            for h in range(PEER_HEADS):
                sel = s2_ref[h, tc] >= th_ref[h, tc, sub:sub + 1, :]
                w = w + jnp.where(sel, e1_ref[h, tc, sub:sub + 1, :] * e2_ref[h, tc], 0.0)
            a = act_old[tc, rows, :]
            wa = w * a
            p_new[rows, lanes] = (wa + wa * jnp.tanh(a * (GELU_C1 + GELU_C2 * (a * a)))).astype(BF16)

    @pl.when(s % 2 == 0)
    def _():
        gate_and_project(p_even, p_odd, act_even, act_odd)

    @pl.when(s % 2 == 1)
    def _():
        gate_and_project(p_odd, p_even, act_odd, act_even)

    @pl.when(proj_last)
    def _():
        y = x_ref[...] + acc_ref[...].T
        out_ref[...] = _rms(y, gf_ref[...]) if final_norm else y


def _experts(x, xn, th, s2, e1, e2, e_u, e_v, g_final, final_norm):
    R = x.shape[0]
    tb = 512
    H, NK = PEER_HEADS, PEER_KEYS
    ne = e_u.shape[0] // PEER_EB
    e_vt = e_v.reshape(ne, PEER_EB, D_MODEL).transpose(0, 2, 1)
    total = (R // tb) * ne
    pair = lambda s, lag: jnp.clip(s - lag, 0, total - 1)
    nt = tb // PEER_TC
    tmap = pl.BlockSpec((H, nt, NK, PEER_TC), lambda s: (0, pair(s, 1) // ne, 0, 0))
    kmap = pl.BlockSpec((H, nt, PEER_SUB, PEER_TC), lambda s: (0, pair(s, 1) // ne, pair(s, 1) % ne, 0))
    kern = functools.partial(_expert_kernel, tb=tb, ne=ne, final_norm=final_norm)
    return pl.pallas_call(
        kern,
        grid=(total + 2,),
        in_specs=[pl.BlockSpec((tb, D_MODEL), lambda s: (pair(s, 2) // ne, 0)),
                  pl.BlockSpec((D_MODEL, tb), lambda s: (0, pair(s, 0) // ne)),
                  kmap, tmap, kmap, tmap,
                  pl.BlockSpec((PEER_EB, D_MODEL), lambda s: (pair(s, 0) % ne, 0)),
                  pl.BlockSpec((None, D_MODEL, PEER_EB), lambda s: (pair(s, 2) % ne, 0, 0)),
                  pl.BlockSpec((1, D_MODEL), lambda s: (0, 0))],
        out_specs=pl.BlockSpec((tb, D_MODEL), lambda s: (pair(s, 2) // ne, 0)),
        out_shape=jax.ShapeDtypeStruct((R, D_MODEL), F32),
        scratch_shapes=[pltpu.VMEM((D_MODEL, tb), F32),
                        pltpu.VMEM((nt, PEER_EB, PEER_TC), F32), pltpu.VMEM((nt, PEER_EB, PEER_TC), F32),
                        pltpu.VMEM((PEER_EB, tb), BF16), pltpu.VMEM((PEER_EB, tb), BF16)],
        compiler_params=_cparams("arbitrary"),
        name="peer_experts",
    )(x, xn, th, s2, e1, e2, e_u, e_vt, g_final)


def _s5_params(a_re, a_im, b_re, b_im, c_re, c_im, log_dt):
    dt = jnp.exp(log_dt)[:, None]
    mag = jnp.exp(dt * a_re)
    abar_re = mag * jnp.cos(dt * a_im)
    abar_im = mag * jnp.sin(dt * a_im)
    nr, ni = abar_re - 1.0, abar_im
    den = a_re * a_re + a_im * a_im
    coef_re = (nr * a_re + ni * a_im) / den
    coef_im = (ni * a_re - nr * a_im) / den
    bbar_re = coef_re[..., None] * b_re - coef_im[..., None] * b_im
    bbar_im = coef_re[..., None] * b_im + coef_im[..., None] * b_re
    eye = jnp.eye(SSM_GROUPS, dtype=F32)
    expand_b = lambda b: jnp.einsum('gpc,gh->gchp', b, eye).reshape(SSM_WIDTH, SSM_CH)
    expand_c = lambda c: jnp.einsum('gcp,gh->gphc', c, eye).reshape(SSM_CH, SSM_WIDTH)
    bmat = jnp.concatenate([expand_b(bbar_re), expand_b(bbar_im)], axis=1).astype(BF16)
    cmat = jnp.stack([expand_c(c_re), -expand_c(c_im)]).astype(BF16)
    abar = jnp.stack([abar_re.reshape(SSM_CH), abar_im.reshape(SSM_CH)])
    return bmat, abar, cmat


def _swap_halves(w):
    half = ROPE_DIM // 2
    return jnp.concatenate([w[..., half:], w[..., :half]], axis=-1)


def _rope_table(pos):
    half = ROPE_DIM // 2
    inv = 1.0 / (ROPE_THETA ** (jnp.arange(half, dtype=F32) / half))
    ang = pos[:, None] * inv[None, :]
    cos, sin = jnp.cos(ang), jnp.sin(ang)
    return jnp.concatenate([cos, cos, -sin, sin], axis=1)


def kernel(x_prompt, x_sample, cache_ckv, cache_kpe, state_conv, state_ssm_re, state_ssm_im, meta_tokens, norm_mix, norm_ffn, w_in, b_gate, conv_w, conv_b, w_a_out, q_norm, w_uq, kv_norm, w_ukv, w_b_out, ssm_a_re, ssm_a_im, ssm_b_re, ssm_b_im, ssm_c_re, ssm_c_im, ssm_log_dt, ssm_d, w_glu, w_o, peer_wq, peer_k1, peer_k2, peer_u, peer_v, norm_final):
    B, seq, _ = x_prompt.shape
    assert B == 1
    nb, dec_seq, _ = x_sample.shape
    ns = nb * dec_seq
    past = cache_ckv.shape[2]
    depth = w_in.shape[0]
    p_end = FRONT + seq
    R = -(-(p_end + ns) // ROW_ALIGN) * ROW_ALIGN
    s0 = R - ns
    lay = dict(ns=ns, dec_seq=dec_seq, p_end=p_end, s0=s0)

    x = jnp.concatenate([
        jnp.zeros((FRONT - N_META, D_MODEL), F32), meta_tokens, x_prompt[0],
        jnp.zeros((s0 - p_end, D_MODEL), F32), x_sample.reshape(ns, D_MODEL)], axis=0)
    pos = jnp.concatenate([
        jnp.arange(s0, dtype=F32) - (FRONT - N_META),
        jnp.tile(past + jnp.arange(dec_seq, dtype=F32), nb)])
    cs = _rope_table(pos)

    outs = [[] for _ in range(10)]
    for l in range(depth):
        wl = w_in[l]
        kp = wl[:, 2560:2624]
        w_small = jnp.concatenate([wl[:, 0:2560], wl[:, 2624:3136], kp, _swap_halves(kp)], axis=1).astype(BF16)
        w_gate = wl[:, 3136:].astype(BF16)
        wq3 = w_uq[l].reshape(Q_LORA, MLA_HEADS, QK_DIM)
        wq_aug = jnp.concatenate([wq3, _swap_halves(wq3[..., NOPE_DIM:])], axis=-1).reshape(Q_LORA, MLA_HEADS * 256)
        wq_aug = wq_aug.astype(BF16)
        wkv = w_ukv[l].astype(BF16)
        wkv3 = wkv.reshape(KV_LORA, MLA_HEADS, NOPE_DIM + V_DIM)
        w_k = wkv3[..., :NOPE_DIM].reshape(KV_LORA, MLA_HEADS * NOPE_DIM)
        w_vt = wkv3[..., NOPE_DIM:].transpose(1, 2, 0)
        bmat, abar, cmat = _s5_params(ssm_a_re[l], ssm_a_im[l], ssm_b_re[l], ssm_b_im[l], ssm_c_re[l],
                                      ssm_c_im[l], ssm_log_dt[l])
        h0 = jnp.stack([state_ssm_re[l].reshape(nb, SSM_CH), state_ssm_im[l].reshape(nb, SSM_CH)])

        proj = _in_proj(x, norm_mix[l][None], w_small)
        za, p_conv, s_conv = _conv(proj, conv_w[l], conv_b[l][None], state_conv[l].reshape(2 * nb, A_WIDTH), lay)
        q, k, vt, lat, kpe = _qkv(proj, cs, q_norm[l][None], wq_aug, kv_norm[l][None], w_k, w_vt)
        o = _flash(q, k, vt)
        lat_all = jnp.concatenate([cache_ckv[l], lat[s0:].reshape(nb, dec_seq, KV_LORA)], axis=1).astype(BF16)
        kpe_all = jnp.concatenate([cache_kpe[l], kpe[s0:].reshape(nb, dec_seq, ROPE_DIM)], axis=1).astype(BF16)
        o = _cached_attn(q, lat_all, kpe_all, wkv, o, lay)
        yc, p_state, s_state = _s5(proj, bmat, abar, cmat, ssm_d[l][None], h0, lay)
        x = _merge(x, norm_mix[l][None], za, o, yc, w_gate, b_gate[l][None], w_a_out[l].astype(BF16),
                   w_b_out[l].astype(BF16), w_glu[l].astype(BF16), w_o[l].astype(BF16))
        xn2, th, s2, e1, e2 = _route(x, norm_ffn[l][None], peer_wq[l].astype(BF16),
                                     peer_k1[l].astype(BF16), peer_k2[l].astype(BF16))
        x = _experts(x, xn2, th, s2, e1, e2, peer_u[l].astype(BF16), peer_v[l].astype(BF16),
                     norm_final[None], final_norm=(l == depth - 1))

        lo = FRONT - N_META
        outs[0].append(lat[lo:p_end][None])
        outs[1].append(kpe[lo:p_end][None])
        outs[2].append(p_conv[None])
        outs[3].append(p_state[0].reshape(1, SSM_GROUPS, SSM_STATE))
        outs[4].append(p_state[1].reshape(1, SSM_GROUPS, SSM_STATE))
        outs[5].append(lat[s0:].reshape(nb, dec_seq, KV_LORA))
        outs[6].append(kpe[s0:].reshape(nb, dec_seq, ROPE_DIM))
        outs[7].append(s_conv.reshape(nb, CONV_WIDTH - 1, A_WIDTH))
        outs[8].append(s_state[0].reshape(nb, SSM_GROUPS, SSM_STATE))
        outs[9].append(s_state[1].reshape(nb, SSM_GROUPS, SSM_STATE))

    y_prompt = x[FRONT:p_end][None]
    y_sample = x[s0:].reshape(nb, dec_seq, D_MODEL)
    return (y_prompt, y_sample) + tuple(jnp.stack(o) for o in outs)
```

```python
import functools
import math

import jax
import jax.numpy as jnp
from jax import lax
from jax.experimental import pallas as pl
from jax.experimental.pallas import tpu as pltpu

F32 = jnp.float32
BF16 = jnp.bfloat16

D_MODEL = 2048
CHUNK = 64
N_META = 16
EPS = 1e-6
NEG_INF = -1e30
A_WIDTH = 512
CONV_WIDTH = 3
MLA_HEADS = 8
Q_LORA = 512
KV_LORA = 512
NOPE_DIM = 128
ROPE_DIM = 64
V_DIM = 128
QK_DIM = NOPE_DIM + ROPE_DIM
ROPE_THETA = 10000.0
ATTN_SCALE = 1.0 / math.sqrt(NOPE_DIM + ROPE_DIM)
SSM_GROUP = 16
SSM_GROUPS = 32
SSM_WIDTH = SSM_GROUP * SSM_GROUPS
SSM_STATE = 64
SSM_CH = SSM_GROUPS * SSM_STATE
PEER_HEADS = 8
PEER_KEYS = 128
PEER_QDIM = 256
PEER_TOPK = 16

FRONT = CHUNK
ROW_ALIGN = 512
VMEM_LIMIT = 60 * 1024 * 1024

C_AB, C_AC, C_AH, C_Q, C_KV, C_U, C_KPE, N_SMALL = 0, 512, 1024, 1536, 2048, 2560, 3072, 3200


def _cparams(*sem):
    return pltpu.CompilerParams(dimension_semantics=sem, vmem_limit_bytes=VMEM_LIMIT)


def _rms(x, g):
    return x * lax.rsqrt(jnp.mean(x * x, axis=-1, keepdims=True) + EPS) * g


def _sigmoid(x):
    return 1.0 / (1.0 + jnp.exp(-x))


def _gelu(x):
    return 0.5 * x * (1.0 + jnp.tanh(math.sqrt(2.0 / math.pi) * (x + 0.044715 * (x * x * x))))


def _dot(a, b):
    return jnp.dot(a, b, preferred_element_type=F32)


def _dot_nt(a, b):
    return lax.dot_general(a, b, (((1,), (1,)), ((), ())), preferred_element_type=F32)


def _in_proj_kernel(x_ref, g_ref, w_ref, o_ref):
    xn = _rms(x_ref[...], g_ref[...]).astype(BF16)
    o_ref[...] = _dot(xn, w_ref[...])


def _in_proj(x, g, w_small):
    R = x.shape[0]
    tb = 256
    return pl.pallas_call(
        _in_proj_kernel,
        grid=(R // tb,),
        in_specs=[pl.BlockSpec((tb, D_MODEL), lambda i: (i, 0)),
                  pl.BlockSpec((1, D_MODEL), lambda i: (0, 0)),
                  pl.BlockSpec((D_MODEL, N_SMALL), lambda i: (0, 0))],
        out_specs=pl.BlockSpec((tb, N_SMALL), lambda i: (i, 0)),
        out_shape=jax.ShapeDtypeStruct((R, N_SMALL), F32),
        compiler_params=_cparams("parallel"),
        name="in_proj",
    )(x, g, w_small)


def _conv_kernel(p_ref, w_ref, b_ref, st_ref, za_ref, pc_ref, sc_ref, zbuf, *, p_blk, p_off, n_blk, tb, dec_seq):
    i = pl.program_id(0)
    ab = p_ref[:, C_AB:C_AB + A_WIDTH]
    z = p_ref[:, C_AC:C_AC + A_WIDTH] * p_ref[:, C_AH:C_AH + A_WIDTH]
    w0, w1, w2 = w_ref[0:1, :], w_ref[1:2, :], w_ref[2:3, :]
    b = b_ref[...]

    @pl.when(i == 0)
    def _():
        zbuf[0:8, :] = jnp.zeros((8, A_WIDTH), F32)

    @pl.when(i < n_blk - 1)
    def _():
        zbuf[8:8 + tb, :] = z
        y = b + w0 * zbuf[6:6 + tb, :] + w1 * zbuf[7:7 + tb, :] + w2 * zbuf[8:8 + tb, :]
        za_ref[...] = (ab * y).astype(BF16)
        zbuf[0:8, :] = zbuf[tb:tb + 8, :]

    @pl.when(i == p_blk)
    def _():
        pc_ref[...] = z[p_off - 1:p_off + 1, :]

    @pl.when(i == n_blk - 1)
    def _():
        for s in range(tb // dec_seq):
            r0 = s * dec_seq
            zbuf[6:8, :] = st_ref[2 * s:2 * s + 2, :]
            zbuf[8:8 + dec_seq, :] = z[r0:r0 + dec_seq, :]
            y = (b + w0 * zbuf[6:6 + dec_seq, :] + w1 * zbuf[7:7 + dec_seq, :]
                 + w2 * zbuf[8:8 + dec_seq, :])
            za_ref[r0:r0 + dec_seq, :] = (ab[r0:r0 + dec_seq, :] * y).astype(BF16)
            sc_ref[2 * s:2 * s + 2, :] = z[r0 + dec_seq - 2:r0 + dec_seq, :]


def _conv(proj, conv_w, conv_b, state, lay):
    R = proj.shape[0]
    tb = lay["ns"]
    n_blk = R // tb
    last = lay["p_end"] - 1
    kern = functools.partial(_conv_kernel, p_blk=last // tb, p_off=last % tb, n_blk=n_blk, tb=tb,
                             dec_seq=lay["dec_seq"])
    assert last % tb >= 1
    nseq = tb // lay["dec_seq"]
    return pl.pallas_call(
        kern,
        grid=(n_blk,),
        in_specs=[pl.BlockSpec((tb, 3 * A_WIDTH), lambda i: (i, 0)),
                  pl.BlockSpec((CONV_WIDTH, A_WIDTH), lambda i: (0, 0)),
                  pl.BlockSpec((1, A_WIDTH), lambda i: (0, 0)),
                  pl.BlockSpec((2 * nseq, A_WIDTH), lambda i: (0, 0))],
        out_specs=[pl.BlockSpec((tb, A_WIDTH), lambda i: (i, 0)),
                   pl.BlockSpec((2, A_WIDTH), lambda i: (0, 0)),
                   pl.BlockSpec((2 * nseq, A_WIDTH), lambda i: (0, 0))],
        out_shape=[jax.ShapeDtypeStruct((R, A_WIDTH), BF16),
                   jax.ShapeDtypeStruct((2, A_WIDTH), F32),
                   jax.ShapeDtypeStruct((2 * nseq, A_WIDTH), F32)],
        scratch_shapes=[pltpu.VMEM((tb + 8, A_WIDTH), F32)],
        compiler_params=_cparams("arbitrary"),
        name="short_conv",
    )(proj, conv_w, conv_b, state)


QK_PAD = 256
Q_SCALE = ATTN_SCALE * math.log2(math.e)


V_ROWS = V_DIM + 16


def _qkv_kernel(cq_ref, ckv_ref, kp_ref, cs_ref, qn_ref, wq_ref, kn_ref, wk_ref, wvt_ref,
                q_ref, k_ref, vt_ref, lat_ref, kpe_ref, *, tb):
    cs = cs_ref[...]
    qc = _rms(cq_ref[...], qn_ref[...]).astype(BF16)
    q = _dot(qc, wq_ref[...])
    lat = _rms(ckv_ref[...], kn_ref[...])
    lat_ref[...] = lat
    lat_b = lat.astype(BF16)
    kn = _dot(lat_b, wk_ref[...])
    ones_row = jnp.where(lax.broadcasted_iota(jnp.int32, (V_ROWS - V_DIM, tb), 0) == 0, 1.0, 0.0).astype(BF16)
    lane = lax.broadcasted_iota(jnp.int32, (tb, 2 * ROPE_DIM), 1)
    row = pl.program_id(0) * tb + lax.broadcasted_iota(jnp.int32, (tb, 2 * ROPE_DIM), 0)
    q_tail = jnp.where(lane == ROPE_DIM, 1.0, 0.0)
    k_tail = jnp.where((lane == ROPE_DIM) & (row < FRONT - N_META), NEG_INF, 0.0)
    t = kp_ref[...] * cs
    kpe = t + pltpu.roll(t, ROPE_DIM, 1)
    kpe_ref[...] = kpe[:, 0:ROPE_DIM]
    k_hi = jnp.where(lane < ROPE_DIM, kpe, k_tail).astype(BF16)
    for h in range(MLA_HEADS):
        c = h * 256
        q_ref[h, :, 0:NOPE_DIM] = (q[:, c:c + NOPE_DIM] * Q_SCALE).astype(BF16)
        t = q[:, c + NOPE_DIM:c + 256] * cs
        qpe = (t + pltpu.roll(t, ROPE_DIM, 1)) * Q_SCALE
        q_ref[h, :, NOPE_DIM:QK_PAD] = jnp.where(lane < ROPE_DIM, qpe, q_tail).astype(BF16)
        k_ref[h, :, 0:NOPE_DIM] = kn[:, h * NOPE_DIM:(h + 1) * NOPE_DIM].astype(BF16)
        k_ref[h, :, NOPE_DIM:QK_PAD] = k_hi
        vt_ref[h, 0:V_DIM, :] = _dot_nt(wvt_ref[h], lat_b).astype(BF16)
        vt_ref[h, V_DIM:V_ROWS, :] = ones_row


def _qkv(proj, cs, q_norm, wq_aug, kv_norm, w_k, w_vt):
    R = proj.shape[0]
    tb = 256
    H = MLA_HEADS
    return pl.pallas_call(
        functools.partial(_qkv_kernel, tb=tb),
        grid=(R // tb,),
        in_specs=[pl.BlockSpec((tb, Q_LORA), lambda i: (i, C_Q // Q_LORA)),
                  pl.BlockSpec((tb, KV_LORA), lambda i: (i, C_KV // KV_LORA)),
                  pl.BlockSpec((tb, 2 * ROPE_DIM), lambda i: (i, C_KPE // (2 * ROPE_DIM))),
                  pl.BlockSpec((tb, 2 * ROPE_DIM), lambda i: (i, 0)),
                  pl.BlockSpec((1, Q_LORA), lambda i: (0, 0)),
                  pl.BlockSpec((Q_LORA, H * 256), lambda i: (0, 0)),
                  pl.BlockSpec((1, KV_LORA), lambda i: (0, 0)),
                  pl.BlockSpec((KV_LORA, H * NOPE_DIM), lambda i: (0, 0)),
                  pl.BlockSpec((H, V_DIM, KV_LORA), lambda i: (0, 0, 0))],
        out_specs=[pl.BlockSpec((H, tb, QK_PAD), lambda i: (0, i, 0)),
                   pl.BlockSpec((H, tb, QK_PAD), lambda i: (0, i, 0)),
                   pl.BlockSpec((H, V_ROWS, tb), lambda i: (0, 0, i)),
                   pl.BlockSpec((tb, KV_LORA), lambda i: (i, 0)),
                   pl.BlockSpec((tb, ROPE_DIM), lambda i: (i, 0))],
        out_shape=[jax.ShapeDtypeStruct((H, R, QK_PAD), BF16),
                   jax.ShapeDtypeStruct((H, R, QK_PAD), BF16),
                   jax.ShapeDtypeStruct((H, V_ROWS, R), BF16),
                   jax.ShapeDtypeStruct((R, KV_LORA), F32),
                   jax.ShapeDtypeStruct((R, ROPE_DIM), F32)],
        compiler_params=_cparams("parallel"),
        name="qkv_rope",
    )(proj, proj, proj, cs, q_norm, wq_aug, kv_norm, w_k, w_vt)


FLASH_TQ = 512
FLASH_TK = 512


def _flash_kernel(it_ref, jt_ref, q_ref, k_ref, vt_ref, o_ref, m_ref, acc_ref, *, bq):
    tq, tk = FLASH_TQ, FLASH_TK
    t = pl.program_id(1)
    i = it_ref[t]
    j = jt_ref[t]

    @pl.when(j == 0)
    def _():
        m_ref[...] = jnp.full(m_ref.shape, NEG_INF, F32)
        acc_ref[...] = jnp.zeros(acc_ref.shape, F32)

    def run(diag):
        tiles = [(a, b) for a in range(bq // tq) for b in range(bq // tk)
                 if not (diag and (b * tk) // CHUNK > (a * tq + tq - 1) // CHUNK)]

        def scores(a, b):
            return _dot_nt(k_ref[b * tk:(b + 1) * tk, :], q_ref[a * tq:(a + 1) * tq, :])

        tiles.sort(key=lambda ab: (ab[1], ab[0]))
        cols = {a: slice(a * tq, (a + 1) * tq) for a, _ in tiles}
        m = {a: m_ref[:, c] for a, c in cols.items()}
        acc = {a: acc_ref[:, c] for a, c in cols.items()}
        s_next = scores(*tiles[0])
        for n, (a, b) in enumerate(tiles):
            s = s_next
            if n + 1 < len(tiles):
                s_next = scores(*tiles[n + 1])
            if diag and (b * tk + tk - 1) // CHUNK > (a * tq) // CHUNK:
                kc = (b * tk + lax.broadcasted_iota(jnp.int32, (tk, tq), 0)) // CHUNK
                qc = (a * tq + lax.broadcasted_iota(jnp.int32, (tk, tq), 1)) // CHUNK
                s = jnp.where(kc <= qc, s, NEG_INF)
            m_new = jnp.maximum(m[a], jnp.max(s, axis=0, keepdims=True))
            alpha = jnp.exp2(m[a] - m_new)
            p = jnp.exp2(s - m_new).astype(BF16)
            acc[a] = alpha * acc[a] + _dot(vt_ref[:, b * tk:(b + 1) * tk], p)
            m[a] = m_new
        for a, c in cols.items():
            if diag:
                row = i * bq + a * tq + lax.broadcasted_iota(jnp.int32, (V_DIM, tq), 1)
                o = jnp.where(row >= FRONT - N_META, acc[a][0:V_DIM, :] / acc[a][V_DIM:V_DIM + 1, :], 0.0)
                o_ref[c, :] = o.T.astype(o_ref.dtype)
            else:
                m_ref[:, c], acc_ref[:, c] = m[a], acc[a]

    @pl.when(j < i)
    def _():
        run(False)

    @pl.when(j == i)
    def _():
        run(True)


def _flash(q, k, vt):
    H, R, _ = q.shape
    bq = next(b for b in (1536, 1024, 512) if R % b == 0)
    n = R // bq
    pairs = [(i, j) for i in range(n) for j in range(i + 1)]
    it = jnp.array([p[0] for p in pairs], jnp.int32)
    jt = jnp.array([p[1] for p in pairs], jnp.int32)
    grid_spec = pltpu.PrefetchScalarGridSpec(
        num_scalar_prefetch=2,
        grid=(H, len(pairs)),
        in_specs=[pl.BlockSpec((None, bq, QK_PAD), lambda h, t, it, jt: (h, it[t], 0)),
                  pl.BlockSpec((None, bq, QK_PAD), lambda h, t, it, jt: (h, jt[t], 0)),
                  pl.BlockSpec((None, V_ROWS, bq), lambda h, t, it, jt: (h, 0, jt[t]))],
        out_specs=pl.BlockSpec((bq, V_DIM), lambda h, t, it, jt: (it[t], h)),
        scratch_shapes=[pltpu.VMEM((1, bq), F32), pltpu.VMEM((V_ROWS, bq), F32)],
    )
    return pl.pallas_call(
        functools.partial(_flash_kernel, bq=bq),
        grid_spec=grid_spec,
        out_shape=jax.ShapeDtypeStruct((R, H * V_DIM), BF16),
        compiler_params=_cparams("parallel", "arbitrary"),
        name="prompt_attention",
    )(it, jt, q, k, vt)


def _cached_attn_kernel(q_ref, lat_ref, kpe_ref, w_ref, o_all_ref, o_ref):
    del o_all_ref
    kv = _dot(lat_ref[...], w_ref[...]).astype(BF16)
    kpe = kpe_ref[...]
    for h in range(MLA_HEADS):
        c = h * 256
        qh = q_ref[h]
        s = _dot_nt(qh[:, 0:NOPE_DIM], kv[:, c:c + NOPE_DIM]) + _dot_nt(qh[:, NOPE_DIM:QK_DIM], kpe)
        m = jnp.max(s, axis=1, keepdims=True)
        p = jnp.exp2(s - m)
        p = p / jnp.sum(p, axis=1, keepdims=True)
        o_ref[:, h * V_DIM:(h + 1) * V_DIM] = _dot(p.astype(BF16), kv[:, c + NOPE_DIM:c + 256]).astype(o_ref.dtype)


def _cached_attn(q, lat_all, kpe_all, w_ukv, o_all, lay):
    H = MLA_HEADS
    B, Lk, _ = lat_all.shape
    S = lay["dec_seq"]
    blk0 = lay["s0"] // S
    return pl.pallas_call(
        _cached_attn_kernel,
        grid=(B,),
        in_specs=[pl.BlockSpec((H, S, QK_PAD), lambda b: (0, blk0 + b, 0)),
                  pl.BlockSpec((None, Lk, KV_LORA), lambda b: (b, 0, 0)),
                  pl.BlockSpec((None, Lk, ROPE_DIM), lambda b: (b, 0, 0)),
                  pl.BlockSpec((KV_LORA, H * 256), lambda b: (0, 0)),
                  pl.BlockSpec(memory_space=pl.ANY)],
        out_specs=pl.BlockSpec((S, H * V_DIM), lambda b: (blk0 + b, 0)),
        out_shape=jax.ShapeDtypeStruct(o_all.shape, o_all.dtype),
        input_output_aliases={4: 0},
        compiler_params=_cparams("parallel"),
        name="sample_attention",
    )(q, lat_all, kpe_all, w_ukv, o_all)


def _s5_kernel(u_ref, bm_ref, a_ref, cm_ref, d_ref, h0_ref, y_ref, ps_ref, ss_ref, hre, him, carry,
               *, p_blk, p_off, n_blk, tb, dec_seq):
    i = pl.program_id(0)
    u = u_ref[...]
    hb = _dot(u.astype(BF16), bm_ref[...])
    hre[...] = hb[:, 0:SSM_CH]
    him[...] = hb[:, SSM_CH:2 * SSM_CH]
    ar = a_ref[0:1, :]
    ai = a_ref[1:2, :]

    def scan(start, n, hr, hi):
        def body(t, c):
            hr, hi = c
            r = start + t
            nr = ar * hr - ai * hi + hre[pl.ds(r, 1), :]
            ni = ar * hi + ai * hr + him[pl.ds(r, 1), :]
            hre[pl.ds(r, 1), :] = nr
            him[pl.ds(r, 1), :] = ni
            return nr, ni
        return lax.fori_loop(0, n, body, (hr, hi), unroll=2)

    @pl.when(i == 0)
    def _():
        carry[...] = jnp.zeros(carry.shape, F32)

    @pl.when(i < n_blk - 1)
    def _():
        hr, hi = scan(0, tb, carry[0:1, :], carry[1:2, :])
        carry[0:1, :] = hr
        carry[1:2, :] = hi

    @pl.when(i == p_blk)
    def _():
        ps_ref[0:1, :] = hre[p_off:p_off + 1, :]
        ps_ref[1:2, :] = him[p_off:p_off + 1, :]

    @pl.when(i == n_blk - 1)
    def _():
        for s in range(tb // dec_seq):
            hr, hi = scan(s * dec_seq, dec_seq, h0_ref[0, s:s + 1, :], h0_ref[1, s:s + 1, :])
            ss_ref[0, s:s + 1, :] = hr
            ss_ref[1, s:s + 1, :] = hi

    y = _dot(hre[...].astype(BF16), cm_ref[0]) + _dot(him[...].astype(BF16), cm_ref[1])
    y_ref[...] = _gelu(y + d_ref[...] * u).astype(BF16)


def _s5(proj, bmat, abar, cmat, d, h0, lay):
    R = proj.shape[0]
    tb = lay["ns"]
    n_blk = R // tb
    last = lay["p_end"] - 1
    nseq = tb // lay["dec_seq"]
    kern = functools.partial(_s5_kernel, p_blk=last // tb, p_off=last % tb, n_blk=n_blk, tb=tb,
                             dec_seq=lay["dec_seq"])
    return pl.pallas_call(
        kern,
        grid=(n_blk,),
        in_specs=[pl.BlockSpec((tb, SSM_WIDTH), lambda i: (i, C_U // SSM_WIDTH)),
                  pl.BlockSpec((SSM_WIDTH, 2 * SSM_CH), lambda i: (0, 0)),
                  pl.BlockSpec((2, SSM_CH), lambda i: (0, 0)),
                  pl.BlockSpec((2, SSM_CH, SSM_WIDTH), lambda i: (0, 0, 0)),
                  pl.BlockSpec((1, SSM_WIDTH), lambda i: (0, 0)),
                  pl.BlockSpec((2, nseq, SSM_CH), lambda i: (0, 0, 0))],
        out_specs=[pl.BlockSpec((tb, SSM_WIDTH), lambda i: (i, 0)),
                   pl.BlockSpec((2, SSM_CH), lambda i: (0, 0)),
                   pl.BlockSpec((2, nseq, SSM_CH), lambda i: (0, 0, 0))],
        out_shape=[jax.ShapeDtypeStruct((R, SSM_WIDTH), BF16),
                   jax.ShapeDtypeStruct((2, SSM_CH), F32),
                   jax.ShapeDtypeStruct((2, nseq, SSM_CH), F32)],
        scratch_shapes=[pltpu.VMEM((tb, SSM_CH), F32), pltpu.VMEM((tb, SSM_CH), F32),
                        pltpu.VMEM((2, SSM_CH), F32)],
        compiler_params=_cparams("arbitrary"),
        name="s5_scan",
    )(proj, bmat, abar, cmat, d, h0)


MERGE_TN = 256
MERGE_NC = D_MODEL // MERGE_TN


def _merge_kernel(x_ref, g_ref, za_ref, o_ref, yc_ref, wg0_ref, wg1_ref, wg2_ref, b0_ref, b1_ref, b2_ref,
                  wa_ref, wb_ref, wga_ref, wgb_ref, wo_ref, out_ref, xn_s, mg_s):
    c = pl.program_id(1)

    @pl.when(c == 0)
    def _():
        xn_s[...] = _rms(x_ref[...], g_ref[...]).astype(BF16)

    xn = xn_s[...]
    g0 = _sigmoid(_dot(xn, wg0_ref[...]) + b0_ref[...])
    g1 = _sigmoid(_dot(xn, wg1_ref[...]) + b1_ref[...])
    g2 = _sigmoid(_dot(xn, wg2_ref[...]) + b2_ref[...])
    yc = yc_ref[...]
    y_a = _dot(za_ref[...], wa_ref[...])
    y_b = _dot(o_ref[...], wb_ref[...])
    y_c = _dot(yc, wga_ref[...]) * _sigmoid(_dot(yc, wgb_ref[...]))
    mg_s[c] = (g0 * y_a + g1 * y_b + g2 * y_c).astype(BF16)

    @pl.when(c == MERGE_NC - 1)
    def _():
        acc = x_ref[...]
        for cc in range(MERGE_NC):
            acc = acc + _dot(mg_s[cc], wo_ref[cc * MERGE_TN:(cc + 1) * MERGE_TN, :])
        out_ref[...] = acc


def _merge(x, g, za, o, yc, w_gate, b_gate, w_a, w_b, w_glu, w_o):
    R = x.shape[0]
    tb = 512
    tn, nc = MERGE_TN, MERGE_NC
    row = lambda w: pl.BlockSpec((tb, w), lambda i, c: (i, 0))
    col = lambda k, off: pl.BlockSpec((k, tn), lambda i, c: (0, off + c))
    return pl.pallas_call(
        _merge_kernel,
        grid=(R // tb, nc),
        in_specs=[row(D_MODEL), pl.BlockSpec((1, D_MODEL), lambda i, c: (0, 0)),
                  row(A_WIDTH), row(MLA_HEADS * V_DIM), row(SSM_WIDTH),
                  col(D_MODEL, 0), col(D_MODEL, nc), col(D_MODEL, 2 * nc),
                  col(1, 0), col(1, nc), col(1, 2 * nc),
                  col(A_WIDTH, 0), col(MLA_HEADS * V_DIM, 0), col(SSM_WIDTH, 0), col(SSM_WIDTH, nc),
                  pl.BlockSpec((D_MODEL, D_MODEL), lambda i, c: (0, 0))],
        out_specs=row(D_MODEL),
        out_shape=jax.ShapeDtypeStruct((R, D_MODEL), F32),
        scratch_shapes=[pltpu.VMEM((tb, D_MODEL), BF16), pltpu.VMEM((nc, tb, tn), BF16)],
        compiler_params=_cparams("parallel", "arbitrary"),
        name="branch_merge",
    )(x, g, za, o, yc, w_gate, w_gate, w_gate, b_gate, b_gate, b_gate, w_a, w_b, w_glu, w_glu, w_o)


SUBLANES = 8


def _sort_desc(v):
    n = len(v)
    v = list(v)
    k = 2
    while k <= n:
        j = k // 2
        while j >= 1:
            for i in range(n):
                l = i ^ j
                if l > i:
                    hi, lo = jnp.maximum(v[i], v[l]), jnp.minimum(v[i], v[l])
                    v[i], v[l] = (hi, lo) if (i & k) == 0 else (lo, hi)
            j //= 2
        k *= 2
    return v


def _merge_top(a, b):
    n = len(a)
    v = [jnp.maximum(a[i], b[n - 1 - i]) for i in range(n)]
    j = n // 2
    while j >= 1:
        for i in range(n):
            l = i ^ j
            if l > i:
                v[i], v[l] = jnp.maximum(v[i], v[l]), jnp.minimum(v[i], v[l])
        j //= 2
    return v


def _top_desc(tiles):
    v = _sort_desc(tiles)
    shift = SUBLANES // 2
    while shift >= 1:
        v = _merge_top(v, [pltpu.roll(x, shift, 0) for x in v])
        shift //= 2
    return v


def _pack_sublanes(rows, sub):
    out = rows[0]
    for r in range(1, len(rows)):
        out = jnp.where(sub == r, rows[r], out)
    return out


def _route_kernel(x_ref, g_ref, wq_ref, k1_ref, k2_ref, xnt_ref, th_ref, s2_ref, e1_ref, e2_ref, *, tb):
    K, NK, S = PEER_TOPK, PEER_KEYS, SUBLANES
    xn32 = _rms(x_ref[...], g_ref[...])
    xnt_ref[...] = xn32.T.astype(BF16)
    xn = xn32.astype(BF16)
    q = _dot(xn, wq_ref[...]).astype(BF16)
    half = PEER_QDIM // 2
    sub = lax.broadcasted_iota(jnp.int32, (S, tb), 0)
    neg = jnp.full((S, tb), -jnp.inf, F32)
    for h in range(PEER_HEADS):
        c = h * PEER_QDIM
        s1 = _dot_nt(k1_ref[...], q[:, c:c + half])
        s2 = _dot_nt(k2_ref[...], q[:, c + half:c + PEER_QDIM])
        t1 = [s1[S * r:S * (r + 1), :] for r in range(NK // S)]
        t2 = [s2[S * r:S * (r + 1), :] for r in range(NK // S)]
        v1 = _top_desc(t1)
        v2 = _top_desc(t2)
        v2_lo, v2_hi = _pack_sublanes(v2[0:S], sub), _pack_sublanes(v2[S:K], sub)
        v1_hi = _pack_sublanes(v1[S:K], sub)
        cand = [v1[0] + v2_lo, v1[0] + v2_hi] + [v1[a] + v2_lo for a in range(1, S)] + [v1_hi + v2[0]]
        vals = _top_desc(cand + [neg] * (K - len(cand)))
        tau = vals[K - 1]
        z = 1.0 + jnp.exp(vals[1] - vals[0])
        for r in range(2, K):
            z = z + jnp.exp(vals[r] - vals[0])
        scale = 0.5 / z
        for r in range(NK // S):
            rows = slice(S * r, S * (r + 1))
            th = jnp.full((S, tb), jnp.inf, F32)
            for b in range(K):
                th = jnp.where(t1[r] + v2[b] >= tau, v2[b], th)
            th_ref[h, rows, :] = th
            e1_ref[h, rows, :] = jnp.exp(t1[r] - v1[0]) * scale
            e2_ref[h, rows, :] = jnp.exp(t2[r] - v2[0])
        s2_ref[h] = s2


def _route(x, g, wq, k1, k2):
    R = x.shape[0]
    tb = 128
    H, NK = PEER_HEADS, PEER_KEYS
    tmap = pl.BlockSpec((H, None, NK, tb), lambda i: (0, i, 0, 0))
    tshape = jax.ShapeDtypeStruct((H, R // tb, NK, tb), F32)
    return pl.pallas_call(
        functools.partial(_route_kernel, tb=tb),
        grid=(R // tb,),
        in_specs=[pl.BlockSpec((tb, D_MODEL), lambda i: (i, 0)),
                  pl.BlockSpec((1, D_MODEL), lambda i: (0, 0)),
                  pl.BlockSpec((D_MODEL, H * PEER_QDIM), lambda i: (0, 0)),
                  pl.BlockSpec((NK, PEER_QDIM // 2), lambda i: (0, 0)),
                  pl.BlockSpec((NK, PEER_QDIM // 2), lambda i: (0, 0))],
        out_specs=[pl.BlockSpec((D_MODEL, tb), lambda i: (0, i)), tmap, tmap, tmap, tmap],
        out_shape=[jax.ShapeDtypeStruct((D_MODEL, R), BF16), tshape, tshape, tshape, tshape],
        compiler_params=_cparams("parallel"),
        name="peer_route",
    )(x, g, wq, k1, k2)


PEER_SUB = 8
PEER_EB = PEER_SUB * PEER_KEYS
PEER_TC = 128
PEER_OUT_ROWS = 512
PEER_ACT_ROWS = 256
GELU_C1 = math.sqrt(2.0 / math.pi)
GELU_C2 = GELU_C1 * 0.044715


def _expert_kernel(x_ref, xnt_ref, th_ref, s2_ref, e1_ref, e2_ref, eu_ref, evt_ref, gf_ref, out_ref, acc_ref,
                   act_even, act_odd, p_even, p_odd, *, tb, ne, final_norm):
    s = pl.program_id(0)
    NK = PEER_KEYS
    proj_first = (s < 2) | ((s - 2) % ne == 0)
    proj_last = (s >= 2) & ((s - 2) % ne == ne - 1)

    @pl.when(s == 0)
    def _():
        act_odd[...] = jnp.zeros(act_odd.shape, F32)
        p_odd[...] = jnp.zeros(p_odd.shape, BF16)

    @pl.when(proj_first)
    def _():
        acc_ref[...] = jnp.zeros(acc_ref.shape, F32)

    n_piece = PEER_SUB * (tb // PEER_TC)
    n_out = D_MODEL // PEER_OUT_ROWS

    n_act = PEER_EB // PEER_ACT_ROWS

    def gate_and_project(p_new, p_old, act_new, act_old):
        for piece in range(n_piece):
            if piece < n_piece // 2 and piece % (n_piece // 2 // n_out) == 0:
                c = piece // (n_piece // 2 // n_out)
                orow = slice(c * PEER_OUT_ROWS, (c + 1) * PEER_OUT_ROWS)
                acc_ref[orow, :] += _dot(evt_ref[orow, :], p_old[...])
            if piece >= n_piece // 2 and piece % (n_piece // 2 // n_act) == 0:
                c = (piece - n_piece // 2) // (n_piece // 2 // n_act)
                erow = slice(c * PEER_ACT_ROWS, (c + 1) * PEER_ACT_ROWS)
                act = _dot(eu_ref[erow, :], xnt_ref[...])
                for t in range(tb // PEER_TC):
                    act_new[t, erow, :] = act[:, t * PEER_TC:(t + 1) * PEER_TC]
            sub, tc = divmod(piece, tb // PEER_TC)
            rows = slice(sub * NK, (sub + 1) * NK)
            lanes = slice(tc * PEER_TC, (tc + 1) * PEER_TC)
            w = jnp.zeros((NK, PEER_TC), F32)
            for h in range(PEER_HEADS):
                sel = s2_ref[h, tc] >= th_ref[h, tc, sub:sub + 1, :]
                w = w + jnp.where(sel, e1_ref[h, tc, sub:sub + 1, :] * e2_ref[h, tc], 0.0)
            a = act_old[tc, rows, :]
            wa = w * a
            p_new[rows, lanes] = (wa + wa * jnp.tanh(a * (GELU_C1 + GELU_C2 * (a * a)))).astype(BF16)

    @pl.when(s % 2 == 0)
    def _():
        gate_and_project(p_even, p_odd, act_even, act_odd)

    @pl.when(s % 2 == 1)
    def _():
        gate_and_project(p_odd, p_even, act_odd, act_even)

    @pl.when(proj_last)
    def _():
        y = x_ref[...] + acc_ref[...].T
        out_ref[...] = _rms(y, gf_ref[...]) if final_norm else y


def _experts(x, xn, th, s2, e1, e2, e_u, e_v, g_final, final_norm):
    R = x.shape[0]
    tb = 512
    H, NK = PEER_HEADS, PEER_KEYS
    ne = e_u.shape[0] // PEER_EB
    e_vt = e_v.reshape(ne, PEER_EB, D_MODEL).transpose(0, 2, 1)
    total = (R // tb) * ne
    pair = lambda s, lag: jnp.clip(s - lag, 0, total - 1)
    nt = tb // PEER_TC
    tmap = pl.BlockSpec((H, nt, NK, PEER_TC), lambda s: (0, pair(s, 1) // ne, 0, 0))
    kmap = pl.BlockSpec((H, nt, PEER_SUB, PEER_TC), lambda s: (0, pair(s, 1) // ne, pair(s, 1) % ne, 0))
    kern = functools.partial(_expert_kernel, tb=tb, ne=ne, final_norm=final_norm)
    return pl.pallas_call(
        kern,
        grid=(total + 2,),
        in_specs=[pl.BlockSpec((tb, D_MODEL), lambda s: (pair(s, 2) // ne, 0)),
                  pl.BlockSpec((D_MODEL, tb), lambda s: (0, pair(s, 0) // ne)),
                  kmap, tmap, kmap, tmap,
                  pl.BlockSpec((PEER_EB, D_MODEL), lambda s: (pair(s, 0) % ne, 0)),
                  pl.BlockSpec((None, D_MODEL, PEER_EB), lambda s: (pair(s, 2) % ne, 0, 0)),
                  pl.BlockSpec((1, D_MODEL), lambda s: (0, 0))],
        out_specs=pl.BlockSpec((tb, D_MODEL), lambda s: (pair(s, 2) // ne, 0)),
        out_shape=jax.ShapeDtypeStruct((R, D_MODEL), F32),
        scratch_shapes=[pltpu.VMEM((D_MODEL, tb), F32),
                        pltpu.VMEM((nt, PEER_EB, PEER_TC), F32), pltpu.VMEM((nt, PEER_EB, PEER_TC), F32),
                        pltpu.VMEM((PEER_EB, tb), BF16), pltpu.VMEM((PEER_EB, tb), BF16)],
        compiler_params=_cparams("arbitrary"),
        name="peer_experts",
    )(x, xn, th, s2, e1, e2, e_u, e_vt, g_final)


def _s5_params(a_re, a_im, b_re, b_im, c_re, c_im, log_dt):
    dt = jnp.exp(log_dt)[:, None]
    mag = jnp.exp(dt * a_re)
    abar_re = mag * jnp.cos(dt * a_im)
    abar_im = mag * jnp.sin(dt * a_im)
    nr, ni = abar_re - 1.0, abar_im
    den = a_re * a_re + a_im * a_im
    coef_re = (nr * a_re + ni * a_im) / den
    coef_im = (ni * a_re - nr * a_im) / den
    bbar_re = coef_re[..., None] * b_re - coef_im[..., None] * b_im
    bbar_im = coef_re[..., None] * b_im + coef_im[..., None] * b_re
    eye = jnp.eye(SSM_GROUPS, dtype=F32)
    expand_b = lambda b: jnp.einsum('gpc,gh->gchp', b, eye).reshape(SSM_WIDTH, SSM_CH)
    expand_c = lambda c: jnp.einsum('gcp,gh->gphc', c, eye).reshape(SSM_CH, SSM_WIDTH)
    bmat = jnp.concatenate([expand_b(bbar_re), expand_b(bbar_im)], axis=1).astype(BF16)
    cmat = jnp.stack([expand_c(c_re), -expand_c(c_im)]).astype(BF16)
    abar = jnp.stack([abar_re.reshape(SSM_CH), abar_im.reshape(SSM_CH)])
    return bmat, abar, cmat


def _swap_halves(w):
    half = ROPE_DIM // 2
    return jnp.concatenate([w[..., half:], w[..., :half]], axis=-1)


def _rope_table(pos):
    half = ROPE_DIM // 2
    inv = 1.0 / (ROPE_THETA ** (jnp.arange(half, dtype=F32) / half))
    ang = pos[:, None] * inv[None, :]
    cos, sin = jnp.cos(ang), jnp.sin(ang)
    return jnp.concatenate([cos, cos, -sin, sin], axis=1)


def kernel(x_prompt, x_sample, cache_ckv, cache_kpe, state_conv, state_ssm_re, state_ssm_im, meta_tokens, norm_mix, norm_ffn, w_in, b_gate, conv_w, conv_b, w_a_out, q_norm, w_uq, kv_norm, w_ukv, w_b_out, ssm_a_re, ssm_a_im, ssm_b_re, ssm_b_im, ssm_c_re, ssm_c_im, ssm_log_dt, ssm_d, w_glu, w_o, peer_wq, peer_k1, peer_k2, peer_u, peer_v, norm_final):
    B, seq, _ = x_prompt.shape
    assert B == 1
    nb, dec_seq, _ = x_sample.shape
    ns = nb * dec_seq
    past = cache_ckv.shape[2]
    depth = w_in.shape[0]
    p_end = FRONT + seq
    R = -(-(p_end + ns) // ROW_ALIGN) * ROW_ALIGN
    s0 = R - ns
    lay = dict(ns=ns, dec_seq=dec_seq, p_end=p_end, s0=s0)

    x = jnp.concatenate([
        jnp.zeros((FRONT - N_META, D_MODEL), F32), meta_tokens, x_prompt[0],
        jnp.zeros((s0 - p_end, D_MODEL), F32), x_sample.reshape(ns, D_MODEL)], axis=0)
    pos = jnp.concatenate([
        jnp.arange(s0, dtype=F32) - (FRONT - N_META),
        jnp.tile(past + jnp.arange(dec_seq, dtype=F32), nb)])
    cs = _rope_table(pos)

    outs = [[] for _ in range(10)]
    for l in range(depth):
        wl = w_in[l]
        kp = wl[:, 2560:2624]
        w_small = jnp.concatenate([wl[:, 0:2560], wl[:, 2624:3136], kp, _swap_halves(kp)], axis=1).astype(BF16)
        w_gate = wl[:, 3136:].astype(BF16)
        wq3 = w_uq[l].reshape(Q_LORA, MLA_HEADS, QK_DIM)
        wq_aug = jnp.concatenate([wq3, _swap_halves(wq3[..., NOPE_DIM:])], axis=-1).reshape(Q_LORA, MLA_HEADS * 256)
        wq_aug = wq_aug.astype(BF16)
        wkv = w_ukv[l].astype(BF16)
        wkv3 = wkv.reshape(KV_LORA, MLA_HEADS, NOPE_DIM + V_DIM)
        w_k = wkv3[..., :NOPE_DIM].reshape(KV_LORA, MLA_HEADS * NOPE_DIM)
        w_vt = wkv3[..., NOPE_DIM:].transpose(1, 2, 0)
        bmat, abar, cmat = _s5_params(ssm_a_re[l], ssm_a_im[l], ssm_b_re[l], ssm_b_im[l], ssm_c_re[l],
                                      ssm_c_im[l], ssm_log_dt[l])
        h0 = jnp.stack([state_ssm_re[l].reshape(nb, SSM_CH), state_ssm_im[l].reshape(nb, SSM_CH)])

        proj = _in_proj(x, norm_mix[l][None], w_small)
        za, p_conv, s_conv = _conv(proj, conv_w[l], conv_b[l][None], state_conv[l].reshape(2 * nb, A_WIDTH), lay)
        q, k, vt, lat, kpe = _qkv(proj, cs, q_norm[l][None], wq_aug, kv_norm[l][None], w_k, w_vt)
        o = _flash(q, k, vt)
        lat_all = jnp.concatenate([cache_ckv[l], lat[s0:].reshape(nb, dec_seq, KV_LORA)], axis=1).astype(BF16)
        kpe_all = jnp.concatenate([cache_kpe[l], kpe[s0:].reshape(nb, dec_seq, ROPE_DIM)], axis=1).astype(BF16)
        o = _cached_attn(q, lat_all, kpe_all, wkv, o, lay)
        yc, p_state, s_state = _s5(proj, bmat, abar, cmat, ssm_d[l][None], h0, lay)
        x = _merge(x, norm_mix[l][None], za, o, yc, w_gate, b_gate[l][None], w_a_out[l].astype(BF16),
                   w_b_out[l].astype(BF16), w_glu[l].astype(BF16), w_o[l].astype(BF16))
        xn2, th, s2, e1, e2 = _route(x, norm_ffn[l][None], peer_wq[l].astype(BF16),
                                     peer_k1[l].astype(BF16), peer_k2[l].astype(BF16))
        x = _experts(x, xn2, th, s2, e1, e2, peer_u[l].astype(BF16), peer_v[l].astype(BF16),
                     norm_final[None], final_norm=(l == depth - 1))

        lo = FRONT - N_META
        outs[0].append(lat[lo:p_end][None])
        outs[1].append(kpe[lo:p_end][None])
        outs[2].append(p_conv[None])
        outs[3].append(p_state[0].reshape(1, SSM_GROUPS, SSM_STATE))
        outs[4].append(p_state[1].reshape(1, SSM_GROUPS, SSM_STATE))
        outs[5].append(lat[s0:].reshape(nb, dec_seq, KV_LORA))
        outs[6].append(kpe[s0:].reshape(nb, dec_seq, ROPE_DIM))
        outs[7].append(s_conv.reshape(nb, CONV_WIDTH - 1, A_WIDTH))
        outs[8].append(s_state[0].reshape(nb, SSM_GROUPS, SSM_STATE))
        outs[9].append(s_state[1].reshape(nb, SSM_GROUPS, SSM_STATE))

    y_prompt = x[FRONT:p_end][None]
    y_sample = x[s0:].reshape(nb, dec_seq, D_MODEL)
    return (y_prompt, y_sample) + tuple(jnp.stack(o) for o in outs)
```

```python
import functools
import math

import jax
import jax.numpy as jnp
from jax import lax
from jax.experimental import pallas as pl
from jax.experimental.pallas import tpu as pltpu

F32 = jnp.float32
BF16 = jnp.bfloat16

D_MODEL = 2048
CHUNK = 64
N_META = 16
EPS = 1e-6
NEG_INF = -1e30
A_WIDTH = 512
CONV_WIDTH = 3
MLA_HEADS = 8
Q_LORA = 512
KV_LORA = 512
NOPE_DIM = 128
ROPE_DIM = 64
V_DIM = 128
QK_DIM = NOPE_DIM + ROPE_DIM
ROPE_THETA = 10000.0
ATTN_SCALE = 1.0 / math.sqrt(NOPE_DIM + ROPE_DIM)
SSM_GROUP = 16
SSM_GROUPS = 32
SSM_WIDTH = SSM_GROUP * SSM_GROUPS
SSM_STATE = 64
SSM_CH = SSM_GROUPS * SSM_STATE
PEER_HEADS = 8
PEER_KEYS = 128
PEER_QDIM = 256
PEER_TOPK = 16

FRONT = CHUNK
ROW_ALIGN = 512
VMEM_LIMIT = 60 * 1024 * 1024

C_AB, C_AC, C_AH, C_Q, C_KV, C_U, C_KPE, N_SMALL = 0, 512, 1024, 1536, 2048, 2560, 3072, 3200


def _cparams(*sem):
    return pltpu.CompilerParams(dimension_semantics=sem, vmem_limit_bytes=VMEM_LIMIT)


def _rms(x, g):
    return x * lax.rsqrt(jnp.mean(x * x, axis=-1, keepdims=True) + EPS) * g


def _sigmoid(x):
    return 1.0 / (1.0 + jnp.exp(-x))


def _gelu(x):
    return 0.5 * x * (1.0 + jnp.tanh(math.sqrt(2.0 / math.pi) * (x + 0.044715 * (x * x * x))))


def _dot(a, b):
    return jnp.dot(a, b, preferred_element_type=F32)


def _dot_nt(a, b):
    return lax.dot_general(a, b, (((1,), (1,)), ((), ())), preferred_element_type=F32)


def _in_proj_kernel(x_ref, g_ref, w_ref, o_ref):
    xn = _rms(x_ref[...], g_ref[...]).astype(BF16)
    o_ref[...] = _dot(xn, w_ref[...])


def _in_proj(x, g, w_small):
    R = x.shape[0]
    tb = 256
    return pl.pallas_call(
        _in_proj_kernel,
        grid=(R // tb,),
        in_specs=[pl.BlockSpec((tb, D_MODEL), lambda i: (i, 0)),
                  pl.BlockSpec((1, D_MODEL), lambda i: (0, 0)),
                  pl.BlockSpec((D_MODEL, N_SMALL), lambda i: (0, 0))],
        out_specs=pl.BlockSpec((tb, N_SMALL), lambda i: (i, 0)),
        out_shape=jax.ShapeDtypeStruct((R, N_SMALL), F32),
        compiler_params=_cparams("parallel"),
        name="in_proj",
    )(x, g, w_small)


def _conv_kernel(p_ref, w_ref, b_ref, st_ref, za_ref, pc_ref, sc_ref, zbuf, *, p_blk, p_off, n_blk, tb, dec_seq):
    i = pl.program_id(0)
    ab = p_ref[:, C_AB:C_AB + A_WIDTH]
    z = p_ref[:, C_AC:C_AC + A_WIDTH] * p_ref[:, C_AH:C_AH + A_WIDTH]
    w0, w1, w2 = w_ref[0:1, :], w_ref[1:2, :], w_ref[2:3, :]
    b = b_ref[...]

    @pl.when(i == 0)
    def _():
        zbuf[0:8, :] = jnp.zeros((8, A_WIDTH), F32)

    @pl.when(i < n_blk - 1)
    def _():
        zbuf[8:8 + tb, :] = z
        y = b + w0 * zbuf[6:6 + tb, :] + w1 * zbuf[7:7 + tb, :] + w2 * zbuf[8:8 + tb, :]
        za_ref[...] = (ab * y).astype(BF16)
        zbuf[0:8, :] = zbuf[tb:tb + 8, :]

    @pl.when(i == p_blk)
    def _():
        pc_ref[...] = z[p_off - 1:p_off + 1, :]

    @pl.when(i == n_blk - 1)
    def _():
        for s in range(tb // dec_seq):
            r0 = s * dec_seq
            zbuf[6:8, :] = st_ref[2 * s:2 * s + 2, :]
            zbuf[8:8 + dec_seq, :] = z[r0:r0 + dec_seq, :]
            y = (b + w0 * zbuf[6:6 + dec_seq, :] + w1 * zbuf[7:7 + dec_seq, :]
                 + w2 * zbuf[8:8 + dec_seq, :])
            za_ref[r0:r0 + dec_seq, :] = (ab[r0:r0 + dec_seq, :] * y).astype(BF16)
            sc_ref[2 * s:2 * s + 2, :] = z[r0 + dec_seq - 2:r0 + dec_seq, :]


def _conv(proj, conv_w, conv_b, state, lay):
    R = proj.shape[0]
    tb = lay["ns"]
    n_blk = R // tb
    last = lay["p_end"] - 1
    kern = functools.partial(_conv_kernel, p_blk=last // tb, p_off=last % tb, n_blk=n_blk, tb=tb,
                             dec_seq=lay["dec_seq"])
    assert last % tb >= 1
    nseq = tb // lay["dec_seq"]
    return pl.pallas_call(
        kern,
        grid=(n_blk,),
        in_specs=[pl.BlockSpec((tb, 3 * A_WIDTH), lambda i: (i, 0)),
                  pl.BlockSpec((CONV_WIDTH, A_WIDTH), lambda i: (0, 0)),
                  pl.BlockSpec((1, A_WIDTH), lambda i: (0, 0)),
                  pl.BlockSpec((2 * nseq, A_WIDTH), lambda i: (0, 0))],
        out_specs=[pl.BlockSpec((tb, A_WIDTH), lambda i: (i, 0)),
                   pl.BlockSpec((2, A_WIDTH), lambda i: (0, 0)),
                   pl.BlockSpec((2 * nseq, A_WIDTH), lambda i: (0, 0))],
        out_shape=[jax.ShapeDtypeStruct((R, A_WIDTH), BF16),
                   jax.ShapeDtypeStruct((2, A_WIDTH), F32),
                   jax.ShapeDtypeStruct((2 * nseq, A_WIDTH), F32)],
        scratch_shapes=[pltpu.VMEM((tb + 8, A_WIDTH), F32)],
        compiler_params=_cparams("arbitrary"),
        name="short_conv",
    )(proj, conv_w, conv_b, state)


QK_PAD = 256
Q_SCALE = ATTN_SCALE * math.log2(math.e)


V_ROWS = V_DIM + 16


def _qkv_kernel(cq_ref, ckv_ref, kp_ref, cs_ref, qn_ref, wq_ref, kn_ref, wk_ref, wvt_ref,
                q_ref, k_ref, vt_ref, lat_ref, kpe_ref, *, tb):
    cs = cs_ref[...]
    qc = _rms(cq_ref[...], qn_ref[...]).astype(BF16)
    q = _dot(qc, wq_ref[...])
    lat = _rms(ckv_ref[...], kn_ref[...])
    lat_ref[...] = lat
    lat_b = lat.astype(BF16)
    kn = _dot(lat_b, wk_ref[...])
    ones_row = jnp.where(lax.broadcasted_iota(jnp.int32, (V_ROWS - V_DIM, tb), 0) == 0, 1.0, 0.0).astype(BF16)
    lane = lax.broadcasted_iota(jnp.int32, (tb, 2 * ROPE_DIM), 1)
    row = pl.program_id(0) * tb + lax.broadcasted_iota(jnp.int32, (tb, 2 * ROPE_DIM), 0)
    q_tail = jnp.where(lane == ROPE_DIM, 1.0, 0.0)
    k_tail = jnp.where((lane == ROPE_DIM) & (row < FRONT - N_META), NEG_INF, 0.0)
    t = kp_ref[...] * cs
    kpe = t + pltpu.roll(t, ROPE_DIM, 1)
    kpe_ref[...] = kpe[:, 0:ROPE_DIM]
    k_hi = jnp.where(lane < ROPE_DIM, kpe, k_tail).astype(BF16)
    for h in range(MLA_HEADS):
        c = h * 256
        q_ref[h, :, 0:NOPE_DIM] = (q[:, c:c + NOPE_DIM] * Q_SCALE).astype(BF16)
        t = q[:, c + NOPE_DIM:c + 256] * cs
        qpe = (t + pltpu.roll(t, ROPE_DIM, 1)) * Q_SCALE
        q_ref[h, :, NOPE_DIM:QK_PAD] = jnp.where(lane < ROPE_DIM, qpe, q_tail).astype(BF16)
        k_ref[h, :, 0:NOPE_DIM] = kn[:, h * NOPE_DIM:(h + 1) * NOPE_DIM].astype(BF16)
        k_ref[h, :, NOPE_DIM:QK_PAD] = k_hi
        vt_ref[h, 0:V_DIM, :] = _dot_nt(wvt_ref[h], lat_b).astype(BF16)
        vt_ref[h, V_DIM:V_ROWS, :] = ones_row


def _qkv(proj, cs, q_norm, wq_aug, kv_norm, w_k, w_vt):
    R = proj.shape[0]
    tb = 256
    H = MLA_HEADS
    return pl.pallas_call(
        functools.partial(_qkv_kernel, tb=tb),
        grid=(R // tb,),
        in_specs=[pl.BlockSpec((tb, Q_LORA), lambda i: (i, C_Q // Q_LORA)),
                  pl.BlockSpec((tb, KV_LORA), lambda i: (i, C_KV // KV_LORA)),
                  pl.BlockSpec((tb, 2 * ROPE_DIM), lambda i: (i, C_KPE // (2 * ROPE_DIM))),
                  pl.BlockSpec((tb, 2 * ROPE_DIM), lambda i: (i, 0)),
                  pl.BlockSpec((1, Q_LORA), lambda i: (0, 0)),
                  pl.BlockSpec((Q_LORA, H * 256), lambda i: (0, 0)),
                  pl.BlockSpec((1, KV_LORA), lambda i: (0, 0)),
                  pl.BlockSpec((KV_LORA, H * NOPE_DIM), lambda i: (0, 0)),
                  pl.BlockSpec((H, V_DIM, KV_LORA), lambda i: (0, 0, 0))],
        out_specs=[pl.BlockSpec((H, tb, QK_PAD), lambda i: (0, i, 0)),
                   pl.BlockSpec((H, tb, QK_PAD), lambda i: (0, i, 0)),
                   pl.BlockSpec((H, V_ROWS, tb), lambda i: (0, 0, i)),
                   pl.BlockSpec((tb, KV_LORA), lambda i: (i, 0)),
                   pl.BlockSpec((tb, ROPE_DIM), lambda i: (i, 0))],
        out_shape=[jax.ShapeDtypeStruct((H, R, QK_PAD), BF16),
                   jax.ShapeDtypeStruct((H, R, QK_PAD), BF16),
                   jax.ShapeDtypeStruct((H, V_ROWS, R), BF16),
                   jax.ShapeDtypeStruct((R, KV_LORA), F32),
                   jax.ShapeDtypeStruct((R, ROPE_DIM), F32)],
        compiler_params=_cparams("parallel"),
        name="qkv_rope",
    )(proj, proj, proj, cs, q_norm, wq_aug, kv_norm, w_k, w_vt)


FLASH_TQ = 512
FLASH_TK = 512


def _flash_kernel(it_ref, jt_ref, q_ref, k_ref, vt_ref, o_ref, m_ref, acc_ref, *, bq):
    tq, tk = FLASH_TQ, FLASH_TK
    t = pl.program_id(1)
    i = it_ref[t]
    j = jt_ref[t]

    @pl.when(j == 0)
    def _():
        m_ref[...] = jnp.full(m_ref.shape, NEG_INF, F32)
        acc_ref[...] = jnp.zeros(acc_ref.shape, F32)

    def run(diag):
        tiles = [(a, b) for a in range(bq // tq) for b in range(bq // tk)
                 if not (diag and (b * tk) // CHUNK > (a * tq + tq - 1) // CHUNK)]

        def scores(a, b):
            return _dot_nt(k_ref[b * tk:(b + 1) * tk, :], q_ref[a * tq:(a + 1) * tq, :])

        tiles.sort(key=lambda ab: (ab[1], ab[0]))
        cols = {a: slice(a * tq, (a + 1) * tq) for a, _ in tiles}
        m = {a: m_ref[:, c] for a, c in cols.items()}
        acc = {a: acc_ref[:, c] for a, c in cols.items()}
        def accumulate(a, b, alpha, p):
            acc[a] = alpha * acc[a] + _dot(vt_ref[:, b * tk:(b + 1) * tk], p)

        s_next = scores(*tiles[0])
        pending = None
        for n, (a, b) in enumerate(tiles):
            s = s_next
            if n + 1 < len(tiles):
                s_next = scores(*tiles[n + 1])
            if pending is not None:
                accumulate(*pending)
            if diag and (b * tk + tk - 1) // CHUNK > (a * tq) // CHUNK:
                kc = (b * tk + lax.broadcasted_iota(jnp.int32, (tk, tq), 0)) // CHUNK
                qc = (a * tq + lax.broadcasted_iota(jnp.int32, (tk, tq), 1)) // CHUNK
                s = jnp.where(kc <= qc, s, NEG_INF)
            m_new = jnp.maximum(m[a], jnp.max(s, axis=0, keepdims=True))
            alpha = jnp.exp2(m[a] - m_new)
            pending = (a, b, alpha, jnp.exp2(s - m_new).astype(BF16))
            m[a] = m_new
        accumulate(*pending)
        for a, c in cols.items():
            if diag:
                row = i * bq + a * tq + lax.broadcasted_iota(jnp.int32, (V_DIM, tq), 1)
                o = jnp.where(row >= FRONT - N_META, acc[a][0:V_DIM, :] / acc[a][V_DIM:V_DIM + 1, :], 0.0)
                o_ref[c, :] = o.T.astype(o_ref.dtype)
            else:
                m_ref[:, c], acc_ref[:, c] = m[a], acc[a]

    @pl.when(j < i)
    def _():
        run(False)

    @pl.when(j == i)
    def _():
        run(True)


def _flash(q, k, vt):
    H, R, _ = q.shape
    bq = next(b for b in (1536, 1024, 512) if R % b == 0)
    n = R // bq
    pairs = [(i, j) for i in range(n) for j in range(i + 1)]
    it = jnp.array([p[0] for p in pairs], jnp.int32)
    jt = jnp.array([p[1] for p in pairs], jnp.int32)
    grid_spec = pltpu.PrefetchScalarGridSpec(
        num_scalar_prefetch=2,
        grid=(H, len(pairs)),
        in_specs=[pl.BlockSpec((None, bq, QK_PAD), lambda h, t, it, jt: (h, it[t], 0)),
                  pl.BlockSpec((None, bq, QK_PAD), lambda h, t, it, jt: (h, jt[t], 0)),
                  pl.BlockSpec((None, V_ROWS, bq), lambda h, t, it, jt: (h, 0, jt[t]))],
        out_specs=pl.BlockSpec((bq, V_DIM), lambda h, t, it, jt: (it[t], h)),
        scratch_shapes=[pltpu.VMEM((1, bq), F32), pltpu.VMEM((V_ROWS, bq), F32)],
    )
    return pl.pallas_call(
        functools.partial(_flash_kernel, bq=bq),
        grid_spec=grid_spec,
        out_shape=jax.ShapeDtypeStruct((R, H * V_DIM), BF16),
        compiler_params=_cparams("parallel", "arbitrary"),
        name="prompt_attention",
    )(it, jt, q, k, vt)


def _cached_attn_kernel(q_ref, lat_ref, kpe_ref, w_ref, o_all_ref, o_ref):
    del o_all_ref
    kv = _dot(lat_ref[...], w_ref[...]).astype(BF16)
    kpe = kpe_ref[...]
    for h in range(MLA_HEADS):
        c = h * 256
        qh = q_ref[h]
        s = _dot_nt(qh[:, 0:NOPE_DIM], kv[:, c:c + NOPE_DIM]) + _dot_nt(qh[:, NOPE_DIM:QK_DIM], kpe)
        m = jnp.max(s, axis=1, keepdims=True)
        p = jnp.exp2(s - m)
        p = p / jnp.sum(p, axis=1, keepdims=True)
        o_ref[:, h * V_DIM:(h + 1) * V_DIM] = _dot(p.astype(BF16), kv[:, c + NOPE_DIM:c + 256]).astype(o_ref.dtype)


def _cached_attn(q, lat_all, kpe_all, w_ukv, o_all, lay):
    H = MLA_HEADS
    B, Lk, _ = lat_all.shape
    S = lay["dec_seq"]
    blk0 = lay["s0"] // S
    return pl.pallas_call(
        _cached_attn_kernel,
        grid=(B,),
        in_specs=[pl.BlockSpec((H, S, QK_PAD), lambda b: (0, blk0 + b, 0)),
                  pl.BlockSpec((None, Lk, KV_LORA), lambda b: (b, 0, 0)),
                  pl.BlockSpec((None, Lk, ROPE_DIM), lambda b: (b, 0, 0)),
                  pl.BlockSpec((KV_LORA, H * 256), lambda b: (0, 0)),
                  pl.BlockSpec(memory_space=pl.ANY)],
        out_specs=pl.BlockSpec((S, H * V_DIM), lambda b: (blk0 + b, 0)),
        out_shape=jax.ShapeDtypeStruct(o_all.shape, o_all.dtype),
        input_output_aliases={4: 0},
        compiler_params=_cparams("parallel"),
        name="sample_attention",
    )(q, lat_all, kpe_all, w_ukv, o_all)


def _s5_kernel(u_ref, bm_ref, a_ref, cm_ref, d_ref, h0_ref, y_ref, ps_ref, ss_ref, hre, him, carry,
               *, p_blk, p_off, n_blk, tb, dec_seq):
    i = pl.program_id(0)
    u = u_ref[...]
    hb = _dot(u.astype(BF16), bm_ref[...])
    hre[...] = hb[:, 0:SSM_CH]
    him[...] = hb[:, SSM_CH:2 * SSM_CH]
    ar = a_ref[0:1, :]
    ai = a_ref[1:2, :]

    def scan(start, n, hr, hi):
        def body(t, c):
            hr, hi = c
            r = start + t
            nr = ar * hr - ai * hi + hre[pl.ds(r, 1), :]
            ni = ar * hi + ai * hr + him[pl.ds(r, 1), :]
            hre[pl.ds(r, 1), :] = nr
            him[pl.ds(r, 1), :] = ni
            return nr, ni
        return lax.fori_loop(0, n, body, (hr, hi), unroll=2)

    @pl.when(i == 0)
    def _():
        carry[...] = jnp.zeros(carry.shape, F32)

    @pl.when(i < n_blk - 1)
    def _():
        hr, hi = scan(0, tb, carry[0:1, :], carry[1:2, :])
        carry[0:1, :] = hr
        carry[1:2, :] = hi

    @pl.when(i == p_blk)
    def _():
        ps_ref[0:1, :] = hre[p_off:p_off + 1, :]
        ps_ref[1:2, :] = him[p_off:p_off + 1, :]

    @pl.when(i == n_blk - 1)
    def _():
        for s in range(tb // dec_seq):
            hr, hi = scan(s * dec_seq, dec_seq, h0_ref[0, s:s + 1, :], h0_ref[1, s:s + 1, :])
            ss_ref[0, s:s + 1, :] = hr
            ss_ref[1, s:s + 1, :] = hi

    y = _dot(hre[...].astype(BF16), cm_ref[0]) + _dot(him[...].astype(BF16), cm_ref[1])
    y_ref[...] = _gelu(y + d_ref[...] * u).astype(BF16)


def _s5(proj, bmat, abar, cmat, d, h0, lay):
    R = proj.shape[0]
    tb = lay["ns"]
    n_blk = R // tb
    last = lay["p_end"] - 1
    nseq = tb // lay["dec_seq"]
    kern = functools.partial(_s5_kernel, p_blk=last // tb, p_off=last % tb, n_blk=n_blk, tb=tb,
                             dec_seq=lay["dec_seq"])
    return pl.pallas_call(
        kern,
        grid=(n_blk,),
        in_specs=[pl.BlockSpec((tb, SSM_WIDTH), lambda i: (i, C_U // SSM_WIDTH)),
                  pl.BlockSpec((SSM_WIDTH, 2 * SSM_CH), lambda i: (0, 0)),
                  pl.BlockSpec((2, SSM_CH), lambda i: (0, 0)),
                  pl.BlockSpec((2, SSM_CH, SSM_WIDTH), lambda i: (0, 0, 0)),
                  pl.BlockSpec((1, SSM_WIDTH), lambda i: (0, 0)),
                  pl.BlockSpec((2, nseq, SSM_CH), lambda i: (0, 0, 0))],
        out_specs=[pl.BlockSpec((tb, SSM_WIDTH), lambda i: (i, 0)),
                   pl.BlockSpec((2, SSM_CH), lambda i: (0, 0)),
                   pl.BlockSpec((2, nseq, SSM_CH), lambda i: (0, 0, 0))],
        out_shape=[jax.ShapeDtypeStruct((R, SSM_WIDTH), BF16),
                   jax.ShapeDtypeStruct((2, SSM_CH), F32),
                   jax.ShapeDtypeStruct((2, nseq, SSM_CH), F32)],
        scratch_shapes=[pltpu.VMEM((tb, SSM_CH), F32), pltpu.VMEM((tb, SSM_CH), F32),
                        pltpu.VMEM((2, SSM_CH), F32)],
        compiler_params=_cparams("arbitrary"),
        name="s5_scan",
    )(proj, bmat, abar, cmat, d, h0)


MERGE_TN = 256
MERGE_NC = D_MODEL // MERGE_TN


def _merge_kernel(x_ref, g_ref, za_ref, o_ref, yc_ref, wg0_ref, wg1_ref, wg2_ref, b0_ref, b1_ref, b2_ref,
                  wa_ref, wb_ref, wga_ref, wgb_ref, wo_ref, out_ref, xn_s, mg_s):
    c = pl.program_id(1)

    @pl.when(c == 0)
    def _():
        xn_s[...] = _rms(x_ref[...], g_ref[...]).astype(BF16)

    xn = xn_s[...]
    g0 = _sigmoid(_dot(xn, wg0_ref[...]) + b0_ref[...])
    g1 = _sigmoid(_dot(xn, wg1_ref[...]) + b1_ref[...])
    g2 = _sigmoid(_dot(xn, wg2_ref[...]) + b2_ref[...])
    yc = yc_ref[...]
    y_a = _dot(za_ref[...], wa_ref[...])
    y_b = _dot(o_ref[...], wb_ref[...])
    y_c = _dot(yc, wga_ref[...]) * _sigmoid(_dot(yc, wgb_ref[...]))
    mg_s[c] = (g0 * y_a + g1 * y_b + g2 * y_c).astype(BF16)

    @pl.when(c == MERGE_NC - 1)
    def _():
        acc = x_ref[...]
        for cc in range(MERGE_NC):
            acc = acc + _dot(mg_s[cc], wo_ref[cc * MERGE_TN:(cc + 1) * MERGE_TN, :])
        out_ref[...] = acc


def _merge(x, g, za, o, yc, w_gate, b_gate, w_a, w_b, w_glu, w_o):
    R = x.shape[0]
    tb = 512
    tn, nc = MERGE_TN, MERGE_NC
    row = lambda w: pl.BlockSpec((tb, w), lambda i, c: (i, 0))
    col = lambda k, off: pl.BlockSpec((k, tn), lambda i, c: (0, off + c))
    return pl.pallas_call(
        _merge_kernel,
        grid=(R // tb, nc),
        in_specs=[row(D_MODEL), pl.BlockSpec((1, D_MODEL), lambda i, c: (0, 0)),
                  row(A_WIDTH), row(MLA_HEADS * V_DIM), row(SSM_WIDTH),
                  col(D_MODEL, 0), col(D_MODEL, nc), col(D_MODEL, 2 * nc),
                  col(1, 0), col(1, nc), col(1, 2 * nc),
                  col(A_WIDTH, 0), col(MLA_HEADS * V_DIM, 0), col(SSM_WIDTH, 0), col(SSM_WIDTH, nc),
                  pl.BlockSpec((D_MODEL, D_MODEL), lambda i, c: (0, 0))],
        out_specs=row(D_MODEL),
        out_shape=jax.ShapeDtypeStruct((R, D_MODEL), F32),
        scratch_shapes=[pltpu.VMEM((tb, D_MODEL), BF16), pltpu.VMEM((nc, tb, tn), BF16)],
        compiler_params=_cparams("parallel", "arbitrary"),
        name="branch_merge",
    )(x, g, za, o, yc, w_gate, w_gate, w_gate, b_gate, b_gate, b_gate, w_a, w_b, w_glu, w_glu, w_o)


SUBLANES = 8


def _sort_desc(v):
    n = len(v)
    v = list(v)
    k = 2
    while k <= n:
        j = k // 2
        while j >= 1:
            for i in range(n):
                l = i ^ j
                if l > i:
                    hi, lo = jnp.maximum(v[i], v[l]), jnp.minimum(v[i], v[l])
                    v[i], v[l] = (hi, lo) if (i & k) == 0 else (lo, hi)
            j //= 2
        k *= 2
    return v


def _merge_top(a, b):
    n = len(a)
    v = [jnp.maximum(a[i], b[n - 1 - i]) for i in range(n)]
    j = n // 2
    while j >= 1:
        for i in range(n):
            l = i ^ j
            if l > i:
                v[i], v[l] = jnp.maximum(v[i], v[l]), jnp.minimum(v[i], v[l])
        j //= 2
    return v


def _top_desc(tiles):
    v = _sort_desc(tiles)
    shift = SUBLANES // 2
    while shift >= 1:
        v = _merge_top(v, [pltpu.roll(x, shift, 0) for x in v])
        shift //= 2
    return v


def _pack_sublanes(rows, sub):
    out = rows[0]
    for r in range(1, len(rows)):
        out = jnp.where(sub == r, rows[r], out)
    return out


def _route_kernel(x_ref, g_ref, wq_ref, k1_ref, k2_ref, xnt_ref, th_ref, s2_ref, e1_ref, e2_ref, *, tb):
    K, NK, S = PEER_TOPK, PEER_KEYS, SUBLANES
    xn32 = _rms(x_ref[...], g_ref[...])
    xnt_ref[...] = xn32.T.astype(BF16)
    xn = xn32.astype(BF16)
    q = _dot(xn, wq_ref[...]).astype(BF16)
    half = PEER_QDIM // 2
    sub = lax.broadcasted_iota(jnp.int32, (S, tb), 0)
    neg = jnp.full((S, tb), -jnp.inf, F32)
    for h in range(PEER_HEADS):
        c = h * PEER_QDIM
        s1 = _dot_nt(k1_ref[...], q[:, c:c + half])
        s2 = _dot_nt(k2_ref[...], q[:, c + half:c + PEER_QDIM])
        t1 = [s1[S * r:S * (r + 1), :] for r in range(NK // S)]
        t2 = [s2[S * r:S * (r + 1), :] for r in range(NK // S)]
        v1 = _top_desc(t1)
        v2 = _top_desc(t2)
        v2_lo, v2_hi = _pack_sublanes(v2[0:S], sub), _pack_sublanes(v2[S:K], sub)
        v1_hi = _pack_sublanes(v1[S:K], sub)
        cand = [v1[0] + v2_lo, v1[0] + v2_hi] + [v1[a] + v2_lo for a in range(1, S)] + [v1_hi + v2[0]]
        vals = _top_desc(cand + [neg] * (K - len(cand)))
        tau = vals[K - 1]
        z = 1.0 + jnp.exp(vals[1] - vals[0])
        for r in range(2, K):
            z = z + jnp.exp(vals[r] - vals[0])
        scale = 0.5 / z
        for r in range(NK // S):
            rows = slice(S * r, S * (r + 1))
            th = jnp.full((S, tb), jnp.inf, F32)
            for b in range(K):
                th = jnp.where(t1[r] + v2[b] >= tau, v2[b], th)
            th_ref[h, rows, :] = th
            e1_ref[h, rows, :] = jnp.exp(t1[r] - v1[0]) * scale
            e2_ref[h, rows, :] = jnp.exp(t2[r] - v2[0])
        s2_ref[h] = s2


def _route(x, g, wq, k1, k2):
    R = x.shape[0]
    tb = 128
    H, NK = PEER_HEADS, PEER_KEYS
    tmap = pl.BlockSpec((H, None, NK, tb), lambda i: (0, i, 0, 0))
    tshape = jax.ShapeDtypeStruct((H, R // tb, NK, tb), F32)
    return pl.pallas_call(
        functools.partial(_route_kernel, tb=tb),
        grid=(R // tb,),
        in_specs=[pl.BlockSpec((tb, D_MODEL), lambda i: (i, 0)),
                  pl.BlockSpec((1, D_MODEL), lambda i: (0, 0)),
                  pl.BlockSpec((D_MODEL, H * PEER_QDIM), lambda i: (0, 0)),
                  pl.BlockSpec((NK, PEER_QDIM // 2), lambda i: (0, 0)),
                  pl.BlockSpec((NK, PEER_QDIM // 2), lambda i: (0, 0))],
        out_specs=[pl.BlockSpec((D_MODEL, tb), lambda i: (0, i)), tmap, tmap, tmap, tmap],
        out_shape=[jax.ShapeDtypeStruct((D_MODEL, R), BF16), tshape, tshape, tshape, tshape],
        compiler_params=_cparams("parallel"),
        name="peer_route",
    )(x, g, wq, k1, k2)


PEER_SUB = 8
PEER_EB = PEER_SUB * PEER_KEYS
PEER_TC = 128
PEER_OUT_ROWS = 256
PEER_ACT_ROWS = 128
GELU_C1 = math.sqrt(2.0 / math.pi)
GELU_C2 = GELU_C1 * 0.044715


def _expert_kernel(x_ref, xnt_ref, th_ref, s2_ref, e1_ref, e2_ref, eu_ref, evt_ref, gf_ref, out_ref, acc_ref,
                   act_even, act_odd, p_even, p_odd, *, tb, ne, final_norm):
    s = pl.program_id(0)
    NK = PEER_KEYS
    proj_first = (s < 2) | ((s - 2) % ne == 0)
    proj_last = (s >= 2) & ((s - 2) % ne == ne - 1)

    @pl.when(s == 0)
    def _():
        act_odd[...] = jnp.zeros(act_odd.shape, F32)
        p_odd[...] = jnp.zeros(p_odd.shape, BF16)

    @pl.when(proj_first)
    def _():
        acc_ref[...] = jnp.zeros(acc_ref.shape, F32)

    n_piece = PEER_SUB * (tb // PEER_TC)
    n_out = D_MODEL // PEER_OUT_ROWS

    n_act = PEER_EB // PEER_ACT_ROWS

    def gate_and_project(p_new, p_old, act_new, act_old):
        for piece in range(n_piece):
            if piece < n_piece // 2 and piece % (n_piece // 2 // n_out) == 0:
                c = piece // (n_piece // 2 // n_out)
                orow = slice(c * PEER_OUT_ROWS, (c + 1) * PEER_OUT_ROWS)
                acc_ref[orow, :] += _dot(evt_ref[orow, :], p_old[...])
            if piece >= n_piece // 2 and piece % (n_piece // 2 // n_act) == 0:
                c = (piece - n_piece // 2) // (n_piece // 2 // n_act)
                erow = slice(c * PEER_ACT_ROWS, (c + 1) * PEER_ACT_ROWS)
                act = _dot(eu_ref[erow, :], xnt_ref[...])
                for t in range(tb // PEER_TC):
                    act_new[t, erow, :] = act[:, t * PEER_TC:(t + 1) * PEER_TC]
            sub, tc = divmod(piece, tb // PEER_TC)
            rows = slice(sub * NK, (sub + 1) * NK)
            lanes = slice(tc * PEER_TC, (tc + 1) * PEER_TC)
            w = jnp.zeros((NK, PEER_TC), F32)
            for h in range(PEER_HEADS):
                sel = s2_ref[h, tc] >= th_ref[h, tc, sub:sub + 1, :]
                w = w + jnp.where(sel, e1_ref[h, tc, sub:sub + 1, :] * e2_ref[h, tc], 0.0)
            a = act_old[tc, rows, :]
            wa = w * a
            p_new[rows, lanes] = (wa + wa * jnp.tanh(a * (GELU_C1 + GELU_C2 * (a * a)))).astype(BF16)

    @pl.when(s % 2 == 0)
    def _():
        gate_and_project(p_even, p_odd, act_even, act_odd)

    @pl.when(s % 2 == 1)
    def _():
        gate_and_project(p_odd, p_even, act_odd, act_even)

    @pl.when(proj_last)
    def _():
        y = x_ref[...] + acc_ref[...].T
        out_ref[...] = _rms(y, gf_ref[...]) if final_norm else y


def _experts(x, xn, th, s2, e1, e2, e_u, e_v, g_final, final_norm):
    R = x.shape[0]
    tb = 512
    H, NK = PEER_HEADS, PEER_KEYS
    ne = e_u.shape[0] // PEER_EB
    e_vt = e_v.reshape(ne, PEER_EB, D_MODEL).transpose(0, 2, 1)
    total = (R // tb) * ne
    pair = lambda s, lag: jnp.clip(s - lag, 0, total - 1)
    nt = tb // PEER_TC
    tmap = pl.BlockSpec((H, nt, NK, PEER_TC), lambda s: (0, pair(s, 1) // ne, 0, 0))
    kmap = pl.BlockSpec((H, nt, PEER_SUB, PEER_TC), lambda s: (0, pair(s, 1) // ne, pair(s, 1) % ne, 0))
    kern = functools.partial(_expert_kernel, tb=tb, ne=ne, final_norm=final_norm)
    return pl.pallas_call(
        kern,
        grid=(total + 2,),
        in_specs=[pl.BlockSpec((tb, D_MODEL), lambda s: (pair(s, 2) // ne, 0)),
                  pl.BlockSpec((D_MODEL, tb), lambda s: (0, pair(s, 0) // ne)),
                  kmap, tmap, kmap, tmap,
                  pl.BlockSpec((PEER_EB, D_MODEL), lambda s: (pair(s, 0) % ne, 0)),
                  pl.BlockSpec((None, D_MODEL, PEER_EB), lambda s: (pair(s, 2) % ne, 0, 0)),
                  pl.BlockSpec((1, D_MODEL), lambda s: (0, 0))],
        out_specs=pl.BlockSpec((tb, D_MODEL), lambda s: (pair(s, 2) // ne, 0)),
        out_shape=jax.ShapeDtypeStruct((R, D_MODEL), F32),
        scratch_shapes=[pltpu.VMEM((D_MODEL, tb), F32),
                        pltpu.VMEM((nt, PEER_EB, PEER_TC), F32), pltpu.VMEM((nt, PEER_EB, PEER_TC), F32),
                        pltpu.VMEM((PEER_EB, tb), BF16), pltpu.VMEM((PEER_EB, tb), BF16)],
        compiler_params=_cparams("arbitrary"),
        name="peer_experts",
    )(x, xn, th, s2, e1, e2, e_u, e_vt, g_final)


def _s5_params(a_re, a_im, b_re, b_im, c_re, c_im, log_dt):
    dt = jnp.exp(log_dt)[:, None]
    mag = jnp.exp(dt * a_re)
    abar_re = mag * jnp.cos(dt * a_im)
    abar_im = mag * jnp.sin(dt * a_im)
    nr, ni = abar_re - 1.0, abar_im
    den = a_re * a_re + a_im * a_im
    coef_re = (nr * a_re + ni * a_im) / den
    coef_im = (ni * a_re - nr * a_im) / den
    bbar_re = coef_re[..., None] * b_re - coef_im[..., None] * b_im
    bbar_im = coef_re[..., None] * b_im + coef_im[..., None] * b_re
    eye = jnp.eye(SSM_GROUPS, dtype=F32)
    expand_b = lambda b: jnp.einsum('gpc,gh->gchp', b, eye).reshape(SSM_WIDTH, SSM_CH)
    expand_c = lambda c: jnp.einsum('gcp,gh->gphc', c, eye).reshape(SSM_CH, SSM_WIDTH)
    bmat = jnp.concatenate([expand_b(bbar_re), expand_b(bbar_im)], axis=1).astype(BF16)
    cmat = jnp.stack([expand_c(c_re), -expand_c(c_im)]).astype(BF16)
    abar = jnp.stack([abar_re.reshape(SSM_CH), abar_im.reshape(SSM_CH)])
    return bmat, abar, cmat


def _swap_halves(w):
    half = ROPE_DIM // 2
    return jnp.concatenate([w[..., half:], w[..., :half]], axis=-1)


def _rope_table(pos):
    half = ROPE_DIM // 2
    inv = 1.0 / (ROPE_THETA ** (jnp.arange(half, dtype=F32) / half))
    ang = pos[:, None] * inv[None, :]
    cos, sin = jnp.cos(ang), jnp.sin(ang)
    return jnp.concatenate([cos, cos, -sin, sin], axis=1)


def kernel(x_prompt, x_sample, cache_ckv, cache_kpe, state_conv, state_ssm_re, state_ssm_im, meta_tokens, norm_mix, norm_ffn, w_in, b_gate, conv_w, conv_b, w_a_out, q_norm, w_uq, kv_norm, w_ukv, w_b_out, ssm_a_re, ssm_a_im, ssm_b_re, ssm_b_im, ssm_c_re, ssm_c_im, ssm_log_dt, ssm_d, w_glu, w_o, peer_wq, peer_k1, peer_k2, peer_u, peer_v, norm_final):
    B, seq, _ = x_prompt.shape
    assert B == 1
    nb, dec_seq, _ = x_sample.shape
    ns = nb * dec_seq
    past = cache_ckv.shape[2]
    depth = w_in.shape[0]
    p_end = FRONT + seq
    R = -(-(p_end + ns) // ROW_ALIGN) * ROW_ALIGN
    s0 = R - ns
    lay = dict(ns=ns, dec_seq=dec_seq, p_end=p_end, s0=s0)

    x = jnp.concatenate([
        jnp.zeros((FRONT - N_META, D_MODEL), F32), meta_tokens, x_prompt[0],
        jnp.zeros((s0 - p_end, D_MODEL), F32), x_sample.reshape(ns, D_MODEL)], axis=0)
    pos = jnp.concatenate([
        jnp.arange(s0, dtype=F32) - (FRONT - N_META),
        jnp.tile(past + jnp.arange(dec_seq, dtype=F32), nb)])
    cs = _rope_table(pos)

    outs = [[] for _ in range(10)]
    for l in range(depth):
        wl = w_in[l]
        kp = wl[:, 2560:2624]
        w_small = jnp.concatenate([wl[:, 0:2560], wl[:, 2624:3136], kp, _swap_halves(kp)], axis=1).astype(BF16)
        w_gate = wl[:, 3136:].astype(BF16)
        wq3 = w_uq[l].reshape(Q_LORA, MLA_HEADS, QK_DIM)
        wq_aug = jnp.concatenate([wq3, _swap_halves(wq3[..., NOPE_DIM:])], axis=-1).reshape(Q_LORA, MLA_HEADS * 256)
        wq_aug = wq_aug.astype(BF16)
        wkv = w_ukv[l].astype(BF16)
        wkv3 = wkv.reshape(KV_LORA, MLA_HEADS, NOPE_DIM + V_DIM)
        w_k = wkv3[..., :NOPE_DIM].reshape(KV_LORA, MLA_HEADS * NOPE_DIM)
        w_vt = wkv3[..., NOPE_DIM:].transpose(1, 2, 0)
        bmat, abar, cmat = _s5_params(ssm_a_re[l], ssm_a_im[l], ssm_b_re[l], ssm_b_im[l], ssm_c_re[l],
                                      ssm_c_im[l], ssm_log_dt[l])
        h0 = jnp.stack([state_ssm_re[l].reshape(nb, SSM_CH), state_ssm_im[l].reshape(nb, SSM_CH)])

        proj = _in_proj(x, norm_mix[l][None], w_small)
        za, p_conv, s_conv = _conv(proj, conv_w[l], conv_b[l][None], state_conv[l].reshape(2 * nb, A_WIDTH), lay)
        q, k, vt, lat, kpe = _qkv(proj, cs, q_norm[l][None], wq_aug, kv_norm[l][None], w_k, w_vt)
        o = _flash(q, k, vt)
        lat_all = jnp.concatenate([cache_ckv[l], lat[s0:].reshape(nb, dec_seq, KV_LORA)], axis=1).astype(BF16)
        kpe_all = jnp.concatenate([cache_kpe[l], kpe[s0:].reshape(nb, dec_seq, ROPE_DIM)], axis=1).astype(BF16)
        o = _cached_attn(q, lat_all, kpe_all, wkv, o, lay)
        yc, p_state, s_state = _s5(proj, bmat, abar, cmat, ssm_d[l][None], h0, lay)
        x = _merge(x, norm_mix[l][None], za, o, yc, w_gate, b_gate[l][None], w_a_out[l].astype(BF16),
                   w_b_out[l].astype(BF16), w_glu[l].astype(BF16), w_o[l].astype(BF16))
        xn2, th, s2, e1, e2 = _route(x, norm_ffn[l][None], peer_wq[l].astype(BF16),
                                     peer_k1[l].astype(BF16), peer_k2[l].astype(BF16))
        x = _experts(x, xn2, th, s2, e1, e2, peer_u[l].astype(BF16), peer_v[l].astype(BF16),
                     norm_final[None], final_norm=(l == depth - 1))

        lo = FRONT - N_META
        outs[0].append(lat[lo:p_end][None])
        outs[1].append(kpe[lo:p_end][None])
        outs[2].append(p_conv[None])
        outs[3].append(p_state[0].reshape(1, SSM_GROUPS, SSM_STATE))
        outs[4].append(p_state[1].reshape(1, SSM_GROUPS, SSM_STATE))
        outs[5].append(lat[s0:].reshape(nb, dec_seq, KV_LORA))
        outs[6].append(kpe[s0:].reshape(nb, dec_seq, ROPE_DIM))
        outs[7].append(s_conv.reshape(nb, CONV_WIDTH - 1, A_WIDTH))
        outs[8].append(s_state[0].reshape(nb, SSM_GROUPS, SSM_STATE))
        outs[9].append(s_state[1].reshape(nb, SSM_GROUPS, SSM_STATE))

    y_prompt = x[FRONT:p_end][None]
    y_sample = x[s0:].reshape(nb, dec_seq, D_MODEL)
    return (y_prompt, y_sample) + tuple(jnp.stack(o) for o in outs)
```

```python
import functools
import math

import jax
import jax.numpy as jnp
from jax import lax
from jax.experimental import pallas as pl
from jax.experimental.pallas import tpu as pltpu

F32 = jnp.float32
BF16 = jnp.bfloat16

D_MODEL = 2048
CHUNK = 64
N_META = 16
EPS = 1e-6
NEG_INF = -1e30
A_WIDTH = 512
CONV_WIDTH = 3
MLA_HEADS = 8
Q_LORA = 512
KV_LORA = 512
NOPE_DIM = 128
ROPE_DIM = 64
V_DIM = 128
QK_DIM = NOPE_DIM + ROPE_DIM
ROPE_THETA = 10000.0
ATTN_SCALE = 1.0 / math.sqrt(NOPE_DIM + ROPE_DIM)
SSM_GROUP = 16
SSM_GROUPS = 32
SSM_WIDTH = SSM_GROUP * SSM_GROUPS
SSM_STATE = 64
SSM_CH = SSM_GROUPS * SSM_STATE
PEER_HEADS = 8
PEER_KEYS = 128
PEER_QDIM = 256
PEER_TOPK = 16

FRONT = CHUNK
ROW_ALIGN = 512
VMEM_LIMIT = 60 * 1024 * 1024

C_AB, C_AC, C_AH, C_Q, C_KV, C_U, C_KPE, N_SMALL = 0, 512, 1024, 1536, 2048, 2560, 3072, 3200


def _cparams(*sem):
    return pltpu.CompilerParams(dimension_semantics=sem, vmem_limit_bytes=VMEM_LIMIT)


def _rms(x, g):
    return x * lax.rsqrt(jnp.mean(x * x, axis=-1, keepdims=True) + EPS) * g


def _sigmoid(x):
    return 1.0 / (1.0 + jnp.exp(-x))


def _gelu(x):
    return 0.5 * x * (1.0 + jnp.tanh(math.sqrt(2.0 / math.pi) * (x + 0.044715 * (x * x * x))))


def _dot(a, b):
    return jnp.dot(a, b, preferred_element_type=F32)


def _dot_nt(a, b):
    return lax.dot_general(a, b, (((1,), (1,)), ((), ())), preferred_element_type=F32)


def _in_proj_kernel(x_ref, g_ref, w_ref, o_ref):
    xn = _rms(x_ref[...], g_ref[...]).astype(BF16)
    o_ref[...] = _dot(xn, w_ref[...])


def _in_proj(x, g, w_small):
    R = x.shape[0]
    tb = 256
    return pl.pallas_call(
        _in_proj_kernel,
        grid=(R // tb,),
        in_specs=[pl.BlockSpec((tb, D_MODEL), lambda i: (i, 0)),
                  pl.BlockSpec((1, D_MODEL), lambda i: (0, 0)),
                  pl.BlockSpec((D_MODEL, N_SMALL), lambda i: (0, 0))],
        out_specs=pl.BlockSpec((tb, N_SMALL), lambda i: (i, 0)),
        out_shape=jax.ShapeDtypeStruct((R, N_SMALL), F32),
        compiler_params=_cparams("parallel"),
        name="in_proj",
    )(x, g, w_small)


def _conv_kernel(p_ref, w_ref, b_ref, st_ref, za_ref, pc_ref, sc_ref, zbuf, *, p_blk, p_off, n_blk, tb, dec_seq):
    i = pl.program_id(0)
    ab = p_ref[:, C_AB:C_AB + A_WIDTH]
    z = p_ref[:, C_AC:C_AC + A_WIDTH] * p_ref[:, C_AH:C_AH + A_WIDTH]
    w0, w1, w2 = w_ref[0:1, :], w_ref[1:2, :], w_ref[2:3, :]
    b = b_ref[...]

    @pl.when(i == 0)
    def _():
        zbuf[0:8, :] = jnp.zeros((8, A_WIDTH), F32)

    @pl.when(i < n_blk - 1)
    def _():
        zbuf[8:8 + tb, :] = z
        y = b + w0 * zbuf[6:6 + tb, :] + w1 * zbuf[7:7 + tb, :] + w2 * zbuf[8:8 + tb, :]
        za_ref[...] = (ab * y).astype(BF16)
        zbuf[0:8, :] = zbuf[tb:tb + 8, :]

    @pl.when(i == p_blk)
    def _():
        pc_ref[...] = z[p_off - 1:p_off + 1, :]

    @pl.when(i == n_blk - 1)
    def _():
        for s in range(tb // dec_seq):
            r0 = s * dec_seq
            zbuf[6:8, :] = st_ref[2 * s:2 * s + 2, :]
            zbuf[8:8 + dec_seq, :] = z[r0:r0 + dec_seq, :]
            y = (b + w0 * zbuf[6:6 + dec_seq, :] + w1 * zbuf[7:7 + dec_seq, :]
                 + w2 * zbuf[8:8 + dec_seq, :])
            za_ref[r0:r0 + dec_seq, :] = (ab[r0:r0 + dec_seq, :] * y).astype(BF16)
            sc_ref[2 * s:2 * s + 2, :] = z[r0 + dec_seq - 2:r0 + dec_seq, :]


def _conv(proj, conv_w, conv_b, state, lay):
    R = proj.shape[0]
    tb = lay["ns"]
    n_blk = R // tb
    last = lay["p_end"] - 1
    kern = functools.partial(_conv_kernel, p_blk=last // tb, p_off=last % tb, n_blk=n_blk, tb=tb,
                             dec_seq=lay["dec_seq"])
    assert last % tb >= 1
    nseq = tb // lay["dec_seq"]
    return pl.pallas_call(
        kern,
        grid=(n_blk,),
        in_specs=[pl.BlockSpec((tb, 3 * A_WIDTH), lambda i: (i, 0)),
                  pl.BlockSpec((CONV_WIDTH, A_WIDTH), lambda i: (0, 0)),
                  pl.BlockSpec((1, A_WIDTH), lambda i: (0, 0)),
                  pl.BlockSpec((2 * nseq, A_WIDTH), lambda i: (0, 0))],
        out_specs=[pl.BlockSpec((tb, A_WIDTH), lambda i: (i, 0)),
                   pl.BlockSpec((2, A_WIDTH), lambda i: (0, 0)),
                   pl.BlockSpec((2 * nseq, A_WIDTH), lambda i: (0, 0))],
        out_shape=[jax.ShapeDtypeStruct((R, A_WIDTH), BF16),
                   jax.ShapeDtypeStruct((2, A_WIDTH), F32),
                   jax.ShapeDtypeStruct((2 * nseq, A_WIDTH), F32)],
        scratch_shapes=[pltpu.VMEM((tb + 8, A_WIDTH), F32)],
        compiler_params=_cparams("arbitrary"),
        name="short_conv",
    )(proj, conv_w, conv_b, state)


QK_PAD = 256
Q_SCALE = ATTN_SCALE * math.log2(math.e)


V_ROWS = V_DIM + 16


def _qkv_kernel(cq_ref, ckv_ref, kp_ref, cs_ref, qn_ref, wq_ref, kn_ref, wk_ref, wvt_ref,
                q_ref, k_ref, vt_ref, lat_ref, kpe_ref, *, tb):
    cs = cs_ref[...]
    qc = _rms(cq_ref[...], qn_ref[...]).astype(BF16)
    q = _dot(qc, wq_ref[...])
    lat = _rms(ckv_ref[...], kn_ref[...])
    lat_ref[...] = lat
    lat_b = lat.astype(BF16)
    kn = _dot(lat_b, wk_ref[...])
    ones_row = jnp.where(lax.broadcasted_iota(jnp.int32, (V_ROWS - V_DIM, tb), 0) == 0, 1.0, 0.0).astype(BF16)
    lane = lax.broadcasted_iota(jnp.int32, (tb, 2 * ROPE_DIM), 1)
    row = pl.program_id(0) * tb + lax.broadcasted_iota(jnp.int32, (tb, 2 * ROPE_DIM), 0)
    q_tail = jnp.where(lane == ROPE_DIM, 1.0, 0.0)
    k_tail = jnp.where((lane == ROPE_DIM) & (row < FRONT - N_META), NEG_INF, 0.0)
    t = kp_ref[...] * cs
    kpe = t + pltpu.roll(t, ROPE_DIM, 1)
    kpe_ref[...] = kpe[:, 0:ROPE_DIM]
    k_hi = jnp.where(lane < ROPE_DIM, kpe, k_tail).astype(BF16)
    for h in range(MLA_HEADS):
        c = h * 256
        q_ref[h, :, 0:NOPE_DIM] = (q[:, c:c + NOPE_DIM] * Q_SCALE).astype(BF16)
        t = q[:, c + NOPE_DIM:c + 256] * cs
        qpe = (t + pltpu.roll(t, ROPE_DIM, 1)) * Q_SCALE
        q_ref[h, :, NOPE_DIM:QK_PAD] = jnp.where(lane < ROPE_DIM, qpe, q_tail).astype(BF16)
        k_ref[h, :, 0:NOPE_DIM] = kn[:, h * NOPE_DIM:(h + 1) * NOPE_DIM].astype(BF16)
        k_ref[h, :, NOPE_DIM:QK_PAD] = k_hi
        vt_ref[h, 0:V_DIM, :] = _dot_nt(wvt_ref[h], lat_b).astype(BF16)
        vt_ref[h, V_DIM:V_ROWS, :] = ones_row


def _qkv(proj, cs, q_norm, wq_aug, kv_norm, w_k, w_vt):
    R = proj.shape[0]
    tb = 256
    H = MLA_HEADS
    return pl.pallas_call(
        functools.partial(_qkv_kernel, tb=tb),
        grid=(R // tb,),
        in_specs=[pl.BlockSpec((tb, Q_LORA), lambda i: (i, C_Q // Q_LORA)),
                  pl.BlockSpec((tb, KV_LORA), lambda i: (i, C_KV // KV_LORA)),
                  pl.BlockSpec((tb, 2 * ROPE_DIM), lambda i: (i, C_KPE // (2 * ROPE_DIM))),
                  pl.BlockSpec((tb, 2 * ROPE_DIM), lambda i: (i, 0)),
                  pl.BlockSpec((1, Q_LORA), lambda i: (0, 0)),
                  pl.BlockSpec((Q_LORA, H * 256), lambda i: (0, 0)),
                  pl.BlockSpec((1, KV_LORA), lambda i: (0, 0)),
                  pl.BlockSpec((KV_LORA, H * NOPE_DIM), lambda i: (0, 0)),
                  pl.BlockSpec((H, V_DIM, KV_LORA), lambda i: (0, 0, 0))],
        out_specs=[pl.BlockSpec((H, tb, QK_PAD), lambda i: (0, i, 0)),
                   pl.BlockSpec((H, tb, QK_PAD), lambda i: (0, i, 0)),
                   pl.BlockSpec((H, V_ROWS, tb), lambda i: (0, 0, i)),
                   pl.BlockSpec((tb, KV_LORA), lambda i: (i, 0)),
                   pl.BlockSpec((tb, ROPE_DIM), lambda i: (i, 0))],
        out_shape=[jax.ShapeDtypeStruct((H, R, QK_PAD), BF16),
                   jax.ShapeDtypeStruct((H, R, QK_PAD), BF16),
                   jax.ShapeDtypeStruct((H, V_ROWS, R), BF16),
                   jax.ShapeDtypeStruct((R, KV_LORA), F32),
                   jax.ShapeDtypeStruct((R, ROPE_DIM), F32)],
        compiler_params=_cparams("parallel"),
        name="qkv_rope",
    )(proj, proj, proj, cs, q_norm, wq_aug, kv_norm, w_k, w_vt)


FLASH_TQ = 512
FLASH_TK = 512
FLASH_AHEAD = 2


def _flash_kernel(it_ref, jt_ref, q_ref, k_ref, vt_ref, o_ref, m_ref, acc_ref, *, bq):
    tq, tk = FLASH_TQ, FLASH_TK
    t = pl.program_id(1)
    i = it_ref[t]
    j = jt_ref[t]

    @pl.when(j == 0)
    def _():
        m_ref[...] = jnp.full(m_ref.shape, NEG_INF, F32)
        acc_ref[...] = jnp.zeros(acc_ref.shape, F32)

    def run(diag):
        tiles = [(a, b) for a in range(bq // tq) for b in range(bq // tk)
                 if not (diag and (b * tk) // CHUNK > (a * tq + tq - 1) // CHUNK)]

        def scores(a, b):
            return _dot_nt(k_ref[b * tk:(b + 1) * tk, :], q_ref[a * tq:(a + 1) * tq, :])

        tiles.sort(key=lambda ab: (ab[1], ab[0]))
        cols = {a: slice(a * tq, (a + 1) * tq) for a, _ in tiles}
        m = {a: m_ref[:, c] for a, c in cols.items()}
        acc = {a: acc_ref[:, c] for a, c in cols.items()}
        def accumulate(a, b, alpha, p):
            acc[a] = alpha * acc[a] + _dot(vt_ref[:, b * tk:(b + 1) * tk], p)

        ahead = [scores(*t) for t in tiles[:FLASH_AHEAD]]
        pending = None
        for n, (a, b) in enumerate(tiles):
            s = ahead.pop(0)
            if n + FLASH_AHEAD < len(tiles):
                ahead.append(scores(*tiles[n + FLASH_AHEAD]))
            if pending is not None:
                accumulate(*pending)
            if diag and (b * tk + tk - 1) // CHUNK > (a * tq) // CHUNK:
                kc = (b * tk + lax.broadcasted_iota(jnp.int32, (tk, tq), 0)) // CHUNK
                qc = (a * tq + lax.broadcasted_iota(jnp.int32, (tk, tq), 1)) // CHUNK
                s = jnp.where(kc <= qc, s, NEG_INF)
            m_new = jnp.maximum(m[a], jnp.max(s, axis=0, keepdims=True))
            alpha = jnp.exp2(m[a] - m_new)
            pending = (a, b, alpha, jnp.exp2(s - m_new).astype(BF16))
            m[a] = m_new
        accumulate(*pending)
        for a, c in cols.items():
            if diag:
                row = i * bq + a * tq + lax.broadcasted_iota(jnp.int32, (V_DIM, tq), 1)
                o = jnp.where(row >= FRONT - N_META, acc[a][0:V_DIM, :] / acc[a][V_DIM:V_DIM + 1, :], 0.0)
                o_ref[c, :] = o.T.astype(o_ref.dtype)
            else:
                m_ref[:, c], acc_ref[:, c] = m[a], acc[a]

    @pl.when(j < i)
    def _():
        run(False)

    @pl.when(j == i)
    def _():
        run(True)


def _flash(q, k, vt):
    H, R, _ = q.shape
    bq = next(b for b in (1536, 1024, 512) if R % b == 0)
    n = R // bq
    pairs = [(i, j) for i in range(n) for j in range(i + 1)]
    it = jnp.array([p[0] for p in pairs], jnp.int32)
    jt = jnp.array([p[1] for p in pairs], jnp.int32)
    grid_spec = pltpu.PrefetchScalarGridSpec(
        num_scalar_prefetch=2,
        grid=(H, len(pairs)),
        in_specs=[pl.BlockSpec((None, bq, QK_PAD), lambda h, t, it, jt: (h, it[t], 0)),
                  pl.BlockSpec((None, bq, QK_PAD), lambda h, t, it, jt: (h, jt[t], 0)),
                  pl.BlockSpec((None, V_ROWS, bq), lambda h, t, it, jt: (h, 0, jt[t]))],
        out_specs=pl.BlockSpec((bq, V_DIM), lambda h, t, it, jt: (it[t], h)),
        scratch_shapes=[pltpu.VMEM((1, bq), F32), pltpu.VMEM((V_ROWS, bq), F32)],
    )
    return pl.pallas_call(
        functools.partial(_flash_kernel, bq=bq),
        grid_spec=grid_spec,
        out_shape=jax.ShapeDtypeStruct((R, H * V_DIM), BF16),
        compiler_params=_cparams("parallel", "arbitrary"),
        name="prompt_attention",
    )(it, jt, q, k, vt)


def _cached_attn_kernel(q_ref, lat_ref, kpe_ref, w_ref, o_all_ref, o_ref):
    del o_all_ref
    kv = _dot(lat_ref[...], w_ref[...]).astype(BF16)
    kpe = kpe_ref[...]
    for h in range(MLA_HEADS):
        c = h * 256
        qh = q_ref[h]
        s = _dot_nt(qh[:, 0:NOPE_DIM], kv[:, c:c + NOPE_DIM]) + _dot_nt(qh[:, NOPE_DIM:QK_DIM], kpe)
        m = jnp.max(s, axis=1, keepdims=True)
        p = jnp.exp2(s - m)
        p = p / jnp.sum(p, axis=1, keepdims=True)
        o_ref[:, h * V_DIM:(h + 1) * V_DIM] = _dot(p.astype(BF16), kv[:, c + NOPE_DIM:c + 256]).astype(o_ref.dtype)


def _cached_attn(q, lat_all, kpe_all, w_ukv, o_all, lay):
    H = MLA_HEADS
    B, Lk, _ = lat_all.shape
    S = lay["dec_seq"]
    blk0 = lay["s0"] // S
    return pl.pallas_call(
        _cached_attn_kernel,
        grid=(B,),
        in_specs=[pl.BlockSpec((H, S, QK_PAD), lambda b: (0, blk0 + b, 0)),
                  pl.BlockSpec((None, Lk, KV_LORA), lambda b: (b, 0, 0)),
                  pl.BlockSpec((None, Lk, ROPE_DIM), lambda b: (b, 0, 0)),
                  pl.BlockSpec((KV_LORA, H * 256), lambda b: (0, 0)),
                  pl.BlockSpec(memory_space=pl.ANY)],
        out_specs=pl.BlockSpec((S, H * V_DIM), lambda b: (blk0 + b, 0)),
        out_shape=jax.ShapeDtypeStruct(o_all.shape, o_all.dtype),
        input_output_aliases={4: 0},
        compiler_params=_cparams("parallel"),
        name="sample_attention",
    )(q, lat_all, kpe_all, w_ukv, o_all)


def _s5_kernel(u_ref, bm_ref, a_ref, cm_ref, d_ref, h0_ref, y_ref, ps_ref, ss_ref, hre, him, carry,
               *, p_blk, p_off, n_blk, tb, dec_seq):
    i = pl.program_id(0)
    u = u_ref[...]
    hb = _dot(u.astype(BF16), bm_ref[...])
    hre[...] = hb[:, 0:SSM_CH]
    him[...] = hb[:, SSM_CH:2 * SSM_CH]
    ar = a_ref[0:1, :]
    ai = a_ref[1:2, :]

    def scan(start, n, hr, hi):
        def body(t, c):
            hr, hi = c
            r = start + t
            nr = ar * hr - ai * hi + hre[pl.ds(r, 1), :]
            ni = ar * hi + ai * hr + him[pl.ds(r, 1), :]
            hre[pl.ds(r, 1), :] = nr
            him[pl.ds(r, 1), :] = ni
            return nr, ni
        return lax.fori_loop(0, n, body, (hr, hi), unroll=2)

    @pl.when(i == 0)
    def _():
        carry[...] = jnp.zeros(carry.shape, F32)

    @pl.when(i < n_blk - 1)
    def _():
        hr, hi = scan(0, tb, carry[0:1, :], carry[1:2, :])
        carry[0:1, :] = hr
        carry[1:2, :] = hi

    @pl.when(i == p_blk)
    def _():
        ps_ref[0:1, :] = hre[p_off:p_off + 1, :]
        ps_ref[1:2, :] = him[p_off:p_off + 1, :]

    @pl.when(i == n_blk - 1)
    def _():
        for s in range(tb // dec_seq):
            hr, hi = scan(s * dec_seq, dec_seq, h0_ref[0, s:s + 1, :], h0_ref[1, s:s + 1, :])
            ss_ref[0, s:s + 1, :] = hr
            ss_ref[1, s:s + 1, :] = hi

    y = _dot(hre[...].astype(BF16), cm_ref[0]) + _dot(him[...].astype(BF16), cm_ref[1])
    y_ref[...] = _gelu(y + d_ref[...] * u).astype(BF16)


def _s5(proj, bmat, abar, cmat, d, h0, lay):
    R = proj.shape[0]
    tb = lay["ns"]
    n_blk = R // tb
    last = lay["p_end"] - 1
    nseq = tb // lay["dec_seq"]
    kern = functools.partial(_s5_kernel, p_blk=last // tb, p_off=last % tb, n_blk=n_blk, tb=tb,
                             dec_seq=lay["dec_seq"])
    return pl.pallas_call(
        kern,
        grid=(n_blk,),
        in_specs=[pl.BlockSpec((tb, SSM_WIDTH), lambda i: (i, C_U // SSM_WIDTH)),
                  pl.BlockSpec((SSM_WIDTH, 2 * SSM_CH), lambda i: (0, 0)),
                  pl.BlockSpec((2, SSM_CH), lambda i: (0, 0)),
                  pl.BlockSpec((2, SSM_CH, SSM_WIDTH), lambda i: (0, 0, 0)),
                  pl.BlockSpec((1, SSM_WIDTH), lambda i: (0, 0)),
                  pl.BlockSpec((2, nseq, SSM_CH), lambda i: (0, 0, 0))],
        out_specs=[pl.BlockSpec((tb, SSM_WIDTH), lambda i: (i, 0)),
                   pl.BlockSpec((2, SSM_CH), lambda i: (0, 0)),
                   pl.BlockSpec((2, nseq, SSM_CH), lambda i: (0, 0, 0))],
        out_shape=[jax.ShapeDtypeStruct((R, SSM_WIDTH), BF16),
                   jax.ShapeDtypeStruct((2, SSM_CH), F32),
                   jax.ShapeDtypeStruct((2, nseq, SSM_CH), F32)],
        scratch_shapes=[pltpu.VMEM((tb, SSM_CH), F32), pltpu.VMEM((tb, SSM_CH), F32),
                        pltpu.VMEM((2, SSM_CH), F32)],
        compiler_params=_cparams("arbitrary"),
        name="s5_scan",
    )(proj, bmat, abar, cmat, d, h0)


MERGE_TN = 256
MERGE_NC = D_MODEL // MERGE_TN


def _merge_kernel(x_ref, g_ref, za_ref, o_ref, yc_ref, wg0_ref, wg1_ref, wg2_ref, b0_ref, b1_ref, b2_ref,
                  wa_ref, wb_ref, wga_ref, wgb_ref, wo_ref, out_ref, xn_s, mg_s):
    c = pl.program_id(1)

    @pl.when(c == 0)
    def _():
        xn_s[...] = _rms(x_ref[...], g_ref[...]).astype(BF16)

    xn = xn_s[...]
    g0 = _sigmoid(_dot(xn, wg0_ref[...]) + b0_ref[...])
    g1 = _sigmoid(_dot(xn, wg1_ref[...]) + b1_ref[...])
    g2 = _sigmoid(_dot(xn, wg2_ref[...]) + b2_ref[...])
    yc = yc_ref[...]
    y_a = _dot(za_ref[...], wa_ref[...])
    y_b = _dot(o_ref[...], wb_ref[...])
    y_c = _dot(yc, wga_ref[...]) * _sigmoid(_dot(yc, wgb_ref[...]))
    mg_s[c] = (g0 * y_a + g1 * y_b + g2 * y_c).astype(BF16)

    @pl.when(c == MERGE_NC - 1)
    def _():
        acc = x_ref[...]
        for cc in range(MERGE_NC):
            acc = acc + _dot(mg_s[cc], wo_ref[cc * MERGE_TN:(cc + 1) * MERGE_TN, :])
        out_ref[...] = acc


def _merge(x, g, za, o, yc, w_gate, b_gate, w_a, w_b, w_glu, w_o):
    R = x.shape[0]
    tb = 512
    tn, nc = MERGE_TN, MERGE_NC
    row = lambda w: pl.BlockSpec((tb, w), lambda i, c: (i, 0))
    col = lambda k, off: pl.BlockSpec((k, tn), lambda i, c: (0, off + c))
    return pl.pallas_call(
        _merge_kernel,
        grid=(R // tb, nc),
        in_specs=[row(D_MODEL), pl.BlockSpec((1, D_MODEL), lambda i, c: (0, 0)),
                  row(A_WIDTH), row(MLA_HEADS * V_DIM), row(SSM_WIDTH),
                  col(D_MODEL, 0), col(D_MODEL, nc), col(D_MODEL, 2 * nc),
                  col(1, 0), col(1, nc), col(1, 2 * nc),
                  col(A_WIDTH, 0), col(MLA_HEADS * V_DIM, 0), col(SSM_WIDTH, 0), col(SSM_WIDTH, nc),
                  pl.BlockSpec((D_MODEL, D_MODEL), lambda i, c: (0, 0))],
        out_specs=row(D_MODEL),
        out_shape=jax.ShapeDtypeStruct((R, D_MODEL), F32),
        scratch_shapes=[pltpu.VMEM((tb, D_MODEL), BF16), pltpu.VMEM((nc, tb, tn), BF16)],
        compiler_params=_cparams("parallel", "arbitrary"),
        name="branch_merge",
    )(x, g, za, o, yc, w_gate, w_gate, w_gate, b_gate, b_gate, b_gate, w_a, w_b, w_glu, w_glu, w_o)


SUBLANES = 8


def _sort_desc(v):
    n = len(v)
    v = list(v)
    k = 2
    while k <= n:
        j = k // 2
        while j >= 1:
            for i in range(n):
                l = i ^ j
                if l > i:
                    hi, lo = jnp.maximum(v[i], v[l]), jnp.minimum(v[i], v[l])
                    v[i], v[l] = (hi, lo) if (i & k) == 0 else (lo, hi)
            j //= 2
        k *= 2
    return v


def _merge_top(a, b):
    n = len(a)
    v = [jnp.maximum(a[i], b[n - 1 - i]) for i in range(n)]
    j = n // 2
    while j >= 1:
        for i in range(n):
            l = i ^ j
            if l > i:
                v[i], v[l] = jnp.maximum(v[i], v[l]), jnp.minimum(v[i], v[l])
        j //= 2
    return v


def _top_desc(tiles):
    v = _sort_desc(tiles)
    shift = SUBLANES // 2
    while shift >= 1:
        v = _merge_top(v, [pltpu.roll(x, shift, 0) for x in v])
        shift //= 2
    return v


def _pack_sublanes(rows, sub):
    out = rows[0]
    for r in range(1, len(rows)):
        out = jnp.where(sub == r, rows[r], out)
    return out


def _route_kernel(x_ref, g_ref, wq_ref, k1_ref, k2_ref, xnt_ref, th_ref, s2_ref, e1_ref, e2_ref, *, tb):
    K, NK, S = PEER_TOPK, PEER_KEYS, SUBLANES
    xn32 = _rms(x_ref[...], g_ref[...])
    xnt_ref[...] = xn32.T.astype(BF16)
    xn = xn32.astype(BF16)
    q = _dot(xn, wq_ref[...]).astype(BF16)
    half = PEER_QDIM // 2
    sub = lax.broadcasted_iota(jnp.int32, (S, tb), 0)
    neg = jnp.full((S, tb), -jnp.inf, F32)
    for h in range(PEER_HEADS):
        c = h * PEER_QDIM
        s1 = _dot_nt(k1_ref[...], q[:, c:c + half])
        s2 = _dot_nt(k2_ref[...], q[:, c + half:c + PEER_QDIM])
        t1 = [s1[S * r:S * (r + 1), :] for r in range(NK // S)]
        t2 = [s2[S * r:S * (r + 1), :] for r in range(NK // S)]
        v1 = _top_desc(t1)
        v2 = _top_desc(t2)
        v2_lo, v2_hi = _pack_sublanes(v2[0:S], sub), _pack_sublanes(v2[S:K], sub)
        v1_hi = _pack_sublanes(v1[S:K], sub)
        cand = [v1[0] + v2_lo, v1[0] + v2_hi] + [v1[a] + v2_lo for a in range(1, S)] + [v1_hi + v2[0]]
        vals = _top_desc(cand + [neg] * (K - len(cand)))
        tau = vals[K - 1]
        z = 1.0 + jnp.exp(vals[1] - vals[0])
        for r in range(2, K):
            z = z + jnp.exp(vals[r] - vals[0])
        scale = 0.5 / z
        for r in range(NK // S):
            rows = slice(S * r, S * (r + 1))
            th = jnp.full((S, tb), jnp.inf, F32)
            for b in range(K):
                th = jnp.where(t1[r] + v2[b] >= tau, v2[b], th)
            th_ref[h, rows, :] = th
            e1_ref[h, rows, :] = jnp.exp(t1[r] - v1[0]) * scale
            e2_ref[h, rows, :] = jnp.exp(t2[r] - v2[0])
        s2_ref[h] = s2


def _route(x, g, wq, k1, k2):
    R = x.shape[0]
    tb = 128
    H, NK = PEER_HEADS, PEER_KEYS
    tmap = pl.BlockSpec((H, None, NK, tb), lambda i: (0, i, 0, 0))
    tshape = jax.ShapeDtypeStruct((H, R // tb, NK, tb), F32)
    return pl.pallas_call(
        functools.partial(_route_kernel, tb=tb),
        grid=(R // tb,),
        in_specs=[pl.BlockSpec((tb, D_MODEL), lambda i: (i, 0)),
                  pl.BlockSpec((1, D_MODEL), lambda i: (0, 0)),
                  pl.BlockSpec((D_MODEL, H * PEER_QDIM), lambda i: (0, 0)),
                  pl.BlockSpec((NK, PEER_QDIM // 2), lambda i: (0, 0)),
                  pl.BlockSpec((NK, PEER_QDIM // 2), lambda i: (0, 0))],
        out_specs=[pl.BlockSpec((D_MODEL, tb), lambda i: (0, i)), tmap, tmap, tmap, tmap],
        out_shape=[jax.ShapeDtypeStruct((D_MODEL, R), BF16), tshape, tshape, tshape, tshape],
        compiler_params=_cparams("parallel"),
        name="peer_route",
    )(x, g, wq, k1, k2)


PEER_SUB = 8
PEER_EB = PEER_SUB * PEER_KEYS
PEER_TC = 128
PEER_OUT_ROWS = 256
PEER_ACT_ROWS = 128
GELU_C1 = math.sqrt(2.0 / math.pi)
GELU_C2 = GELU_C1 * 0.044715


def _expert_kernel(x_ref, xnt_ref, th_ref, s2_ref, e1_ref, e2_ref, eu_ref, evt_ref, gf_ref, out_ref, acc_ref,
                   act_even, act_odd, p_even, p_odd, *, tb, ne, final_norm):
    s = pl.program_id(0)
    NK = PEER_KEYS
    proj_first = (s < 2) | ((s - 2) % ne == 0)
    proj_last = (s >= 2) & ((s - 2) % ne == ne - 1)

    @pl.when(s == 0)
    def _():
        act_odd[...] = jnp.zeros(act_odd.shape, F32)
        p_odd[...] = jnp.zeros(p_odd.shape, BF16)

    @pl.when(proj_first)
    def _():
        acc_ref[...] = jnp.zeros(acc_ref.shape, F32)

    n_piece = PEER_SUB * (tb // PEER_TC)
    n_out = D_MODEL // PEER_OUT_ROWS

    n_act = PEER_EB // PEER_ACT_ROWS

    def gate_and_project(p_new, p_old, act_new, act_old):
        for piece in range(n_piece):
            if piece < n_piece // 2 and piece % (n_piece // 2 // n_out) == 0:
                c = piece // (n_piece // 2 // n_out)
                orow = slice(c * PEER_OUT_ROWS, (c + 1) * PEER_OUT_ROWS)
                acc_ref[orow, :] += _dot(evt_ref[orow, :], p_old[...])
            if piece >= n_piece // 2 and piece % (n_piece // 2 // n_act) == 0:
                c = (piece - n_piece // 2) // (n_piece // 2 // n_act)
                erow = slice(c * PEER_ACT_ROWS, (c + 1) * PEER_ACT_ROWS)
                act = _dot(eu_ref[erow, :], xnt_ref[...])
                for t in range(tb // PEER_TC):
                    act_new[t, erow, :] = act[:, t * PEER_TC:(t + 1) * PEER_TC]
            sub, tc = divmod(piece, tb // PEER_TC)
            rows = slice(sub * NK, (sub + 1) * NK)
            lanes = slice(tc * PEER_TC, (tc + 1) * PEER_TC)
            w = jnp.zeros((NK, PEER_TC), F32)
            for h in range(PEER_HEADS):
                sel = s2_ref[h, tc] >= th_ref[h, tc, sub:sub + 1, :]
                w = w + jnp.where(sel, e1_ref[h, tc, sub:sub + 1, :] * e2_ref[h, tc], 0.0)
            a = act_old[tc, rows, :]
            wa = w * a
            p_new[rows, lanes] = (wa + wa * jnp.tanh(a * (GELU_C1 + GELU_C2 * (a * a)))).astype(BF16)

    @pl.when(s % 2 == 0)
    def _():
        gate_and_project(p_even, p_odd, act_even, act_odd)

    @pl.when(s % 2 == 1)
    def _():
        gate_and_project(p_odd, p_even, act_odd, act_even)

    @pl.when(proj_last)
    def _():
        y = x_ref[...] + acc_ref[...].T
        out_ref[...] = _rms(y, gf_ref[...]) if final_norm else y


def _experts(x, xn, th, s2, e1, e2, e_u, e_v, g_final, final_norm):
    R = x.shape[0]
    tb = 512
    H, NK = PEER_HEADS, PEER_KEYS
    ne = e_u.shape[0] // PEER_EB
    e_vt = e_v.reshape(ne, PEER_EB, D_MODEL).transpose(0, 2, 1)
    total = (R // tb) * ne
    pair = lambda s, lag: jnp.clip(s - lag, 0, total - 1)
    nt = tb // PEER_TC
    tmap = pl.BlockSpec((H, nt, NK, PEER_TC), lambda s: (0, pair(s, 1) // ne, 0, 0))
    kmap = pl.BlockSpec((H, nt, PEER_SUB, PEER_TC), lambda s: (0, pair(s, 1) // ne, pair(s, 1) % ne, 0))
    kern = functools.partial(_expert_kernel, tb=tb, ne=ne, final_norm=final_norm)
    return pl.pallas_call(
        kern,
        grid=(total + 2,),
        in_specs=[pl.BlockSpec((tb, D_MODEL), lambda s: (pair(s, 2) // ne, 0)),
                  pl.BlockSpec((D_MODEL, tb), lambda s: (0, pair(s, 0) // ne)),
                  kmap, tmap, kmap, tmap,
                  pl.BlockSpec((PEER_EB, D_MODEL), lambda s: (pair(s, 0) % ne, 0)),
                  pl.BlockSpec((None, D_MODEL, PEER_EB), lambda s: (pair(s, 2) % ne, 0, 0)),
                  pl.BlockSpec((1, D_MODEL), lambda s: (0, 0))],
        out_specs=pl.BlockSpec((tb, D_MODEL), lambda s: (pair(s, 2) // ne, 0)),
        out_shape=jax.ShapeDtypeStruct((R, D_MODEL), F32),
        scratch_shapes=[pltpu.VMEM((D_MODEL, tb), F32),
                        pltpu.VMEM((nt, PEER_EB, PEER_TC), F32), pltpu.VMEM((nt, PEER_EB, PEER_TC), F32),
                        pltpu.VMEM((PEER_EB, tb), BF16), pltpu.VMEM((PEER_EB, tb), BF16)],
        compiler_params=_cparams("arbitrary"),
        name="peer_experts",
    )(x, xn, th, s2, e1, e2, e_u, e_vt, g_final)


def _s5_params(a_re, a_im, b_re, b_im, c_re, c_im, log_dt):
    dt = jnp.exp(log_dt)[:, None]
    mag = jnp.exp(dt * a_re)
    abar_re = mag * jnp.cos(dt * a_im)
    abar_im = mag * jnp.sin(dt * a_im)
    nr, ni = abar_re - 1.0, abar_im
    den = a_re * a_re + a_im * a_im
    coef_re = (nr * a_re + ni * a_im) / den
    coef_im = (ni * a_re - nr * a_im) / den
    bbar_re = coef_re[..., None] * b_re - coef_im[..., None] * b_im
    bbar_im = coef_re[..., None] * b_im + coef_im[..., None] * b_re
    eye = jnp.eye(SSM_GROUPS, dtype=F32)
    expand_b = lambda b: jnp.einsum('gpc,gh->gchp', b, eye).reshape(SSM_WIDTH, SSM_CH)
    expand_c = lambda c: jnp.einsum('gcp,gh->gphc', c, eye).reshape(SSM_CH, SSM_WIDTH)
    bmat = jnp.concatenate([expand_b(bbar_re), expand_b(bbar_im)], axis=1).astype(BF16)
    cmat = jnp.stack([expand_c(c_re), -expand_c(c_im)]).astype(BF16)
    abar = jnp.stack([abar_re.reshape(SSM_CH), abar_im.reshape(SSM_CH)])
    return bmat, abar, cmat


def _swap_halves(w):
    half = ROPE_DIM // 2
    return jnp.concatenate([w[..., half:], w[..., :half]], axis=-1)


def _rope_table(pos):
    half = ROPE_DIM // 2
    inv = 1.0 / (ROPE_THETA ** (jnp.arange(half, dtype=F32) / half))
    ang = pos[:, None] * inv[None, :]
    cos, sin = jnp.cos(ang), jnp.sin(ang)
    return jnp.concatenate([cos, cos, -sin, sin], axis=1)


def kernel(x_prompt, x_sample, cache_ckv, cache_kpe, state_conv, state_ssm_re, state_ssm_im, meta_tokens, norm_mix, norm_ffn, w_in, b_gate, conv_w, conv_b, w_a_out, q_norm, w_uq, kv_norm, w_ukv, w_b_out, ssm_a_re, ssm_a_im, ssm_b_re, ssm_b_im, ssm_c_re, ssm_c_im, ssm_log_dt, ssm_d, w_glu, w_o, peer_wq, peer_k1, peer_k2, peer_u, peer_v, norm_final):
    B, seq, _ = x_prompt.shape
    assert B == 1
    nb, dec_seq, _ = x_sample.shape
    ns = nb * dec_seq
    past = cache_ckv.shape[2]
    depth = w_in.shape[0]
    p_end = FRONT + seq
    R = -(-(p_end + ns) // ROW_ALIGN) * ROW_ALIGN
    s0 = R - ns
    lay = dict(ns=ns, dec_seq=dec_seq, p_end=p_end, s0=s0)

    x = jnp.concatenate([
        jnp.zeros((FRONT - N_META, D_MODEL), F32), meta_tokens, x_prompt[0],
        jnp.zeros((s0 - p_end, D_MODEL), F32), x_sample.reshape(ns, D_MODEL)], axis=0)
    pos = jnp.concatenate([
        jnp.arange(s0, dtype=F32) - (FRONT - N_META),
        jnp.tile(past + jnp.arange(dec_seq, dtype=F32), nb)])
    cs = _rope_table(pos)

    outs = [[] for _ in range(10)]
    for l in range(depth):
        wl = w_in[l]
        kp = wl[:, 2560:2624]
        w_small = jnp.concatenate([wl[:, 0:2560], wl[:, 2624:3136], kp, _swap_halves(kp)], axis=1).astype(BF16)
        w_gate = wl[:, 3136:].astype(BF16)
        wq3 = w_uq[l].reshape(Q_LORA, MLA_HEADS, QK_DIM)
        wq_aug = jnp.concatenate([wq3, _swap_halves(wq3[..., NOPE_DIM:])], axis=-1).reshape(Q_LORA, MLA_HEADS * 256)
        wq_aug = wq_aug.astype(BF16)
        wkv = w_ukv[l].astype(BF16)
        wkv3 = wkv.reshape(KV_LORA, MLA_HEADS, NOPE_DIM + V_DIM)
        w_k = wkv3[..., :NOPE_DIM].reshape(KV_LORA, MLA_HEADS * NOPE_DIM)
        w_vt = wkv3[..., NOPE_DIM:].transpose(1, 2, 0)
        bmat, abar, cmat = _s5_params(ssm_a_re[l], ssm_a_im[l], ssm_b_re[l], ssm_b_im[l], ssm_c_re[l],
                                      ssm_c_im[l], ssm_log_dt[l])
        h0 = jnp.stack([state_ssm_re[l].reshape(nb, SSM_CH), state_ssm_im[l].reshape(nb, SSM_CH)])

        proj = _in_proj(x, norm_mix[l][None], w_small)
        za, p_conv, s_conv = _conv(proj, conv_w[l], conv_b[l][None], state_conv[l].reshape(2 * nb, A_WIDTH), lay)
        q, k, vt, lat, kpe = _qkv(proj, cs, q_norm[l][None], wq_aug, kv_norm[l][None], w_k, w_vt)
        o = _flash(q, k, vt)
        lat_all = jnp.concatenate([cache_ckv[l], lat[s0:].reshape(nb, dec_seq, KV_LORA)], axis=1).astype(BF16)
        kpe_all = jnp.concatenate([cache_kpe[l], kpe[s0:].reshape(nb, dec_seq, ROPE_DIM)], axis=1).astype(BF16)
        o = _cached_attn(q, lat_all, kpe_all, wkv, o, lay)
        yc, p_state, s_state = _s5(proj, bmat, abar, cmat, ssm_d[l][None], h0, lay)
        x = _merge(x, norm_mix[l][None], za, o, yc, w_gate, b_gate[l][None], w_a_out[l].astype(BF16),
                   w_b_out[l].astype(BF16), w_glu[l].astype(BF16), w_o[l].astype(BF16))
        xn2, th, s2, e1, e2 = _route(x, norm_ffn[l][None], peer_wq[l].astype(BF16),
                                     peer_k1[l].astype(BF16), peer_k2[l].astype(BF16))
        x = _experts(x, xn2, th, s2, e1, e2, peer_u[l].astype(BF16), peer_v[l].astype(BF16),
                     norm_final[None], final_norm=(l == depth - 1))

        lo = FRONT - N_META
        outs[0].append(lat[lo:p_end][None])
        outs[1].append(kpe[lo:p_end][None])
        outs[2].append(p_conv[None])
        outs[3].append(p_state[0].reshape(1, SSM_GROUPS, SSM_STATE))
        outs[4].append(p_state[1].reshape(1, SSM_GROUPS, SSM_STATE))
        outs[5].append(lat[s0:].reshape(nb, dec_seq, KV_LORA))
        outs[6].append(kpe[s0:].reshape(nb, dec_seq, ROPE_DIM))
        outs[7].append(s_conv.reshape(nb, CONV_WIDTH - 1, A_WIDTH))
        outs[8].append(s_state[0].reshape(nb, SSM_GROUPS, SSM_STATE))
        outs[9].append(s_state[1].reshape(nb, SSM_GROUPS, SSM_STATE))

    y_prompt = x[FRONT:p_end][None]
    y_sample = x[s0:].reshape(nb, dec_seq, D_MODEL)
    return (y_prompt, y_sample) + tuple(jnp.stack(o) for o in outs)
```

```python
import functools
import math

import jax
import jax.numpy as jnp
from jax import lax
from jax.experimental import pallas as pl
from jax.experimental.pallas import tpu as pltpu

F32 = jnp.float32
BF16 = jnp.bfloat16

D_MODEL = 2048
CHUNK = 64
N_META = 16
EPS = 1e-6
NEG_INF = -1e30
A_WIDTH = 512
CONV_WIDTH = 3
MLA_HEADS = 8
Q_LORA = 512
KV_LORA = 512
NOPE_DIM = 128
ROPE_DIM = 64
V_DIM = 128
QK_DIM = NOPE_DIM + ROPE_DIM
ROPE_THETA = 10000.0
ATTN_SCALE = 1.0 / math.sqrt(NOPE_DIM + ROPE_DIM)
SSM_GROUP = 16
SSM_GROUPS = 32
SSM_WIDTH = SSM_GROUP * SSM_GROUPS
SSM_STATE = 64
SSM_CH = SSM_GROUPS * SSM_STATE
PEER_HEADS = 8
PEER_KEYS = 128
PEER_QDIM = 256
PEER_TOPK = 16

FRONT = CHUNK
ROW_ALIGN = 512
VMEM_LIMIT = 60 * 1024 * 1024

C_AB, C_AC, C_AH, C_Q, C_KV, C_U, C_KPE, N_SMALL = 0, 512, 1024, 1536, 2048, 2560, 3072, 3200


def _cparams(*sem):
    return pltpu.CompilerParams(dimension_semantics=sem, vmem_limit_bytes=VMEM_LIMIT)


def _rms(x, g):
    return x * lax.rsqrt(jnp.mean(x * x, axis=-1, keepdims=True) + EPS) * g


def _sigmoid(x):
    return 1.0 / (1.0 + jnp.exp(-x))


def _gelu(x):
    return 0.5 * x * (1.0 + jnp.tanh(math.sqrt(2.0 / math.pi) * (x + 0.044715 * (x * x * x))))


def _dot(a, b):
    return jnp.dot(a, b, preferred_element_type=F32)


def _dot_nt(a, b):
    return lax.dot_general(a, b, (((1,), (1,)), ((), ())), preferred_element_type=F32)


def _in_proj_kernel(x_ref, g_ref, w_ref, o_ref):
    xn = _rms(x_ref[...], g_ref[...]).astype(BF16)
    o_ref[...] = _dot(xn, w_ref[...])


def _in_proj(x, g, w_small):
    R = x.shape[0]
    tb = 256
    return pl.pallas_call(
        _in_proj_kernel,
        grid=(R // tb,),
        in_specs=[pl.BlockSpec((tb, D_MODEL), lambda i: (i, 0)),
                  pl.BlockSpec((1, D_MODEL), lambda i: (0, 0)),
                  pl.BlockSpec((D_MODEL, N_SMALL), lambda i: (0, 0))],
        out_specs=pl.BlockSpec((tb, N_SMALL), lambda i: (i, 0)),
        out_shape=jax.ShapeDtypeStruct((R, N_SMALL), F32),
        compiler_params=_cparams("parallel"),
        name="in_proj",
    )(x, g, w_small)


def _conv_kernel(p_ref, w_ref, b_ref, st_ref, za_ref, pc_ref, sc_ref, zbuf, *, p_blk, p_off, n_blk, tb, dec_seq):
    i = pl.program_id(0)
    ab = p_ref[:, C_AB:C_AB + A_WIDTH]
    z = p_ref[:, C_AC:C_AC + A_WIDTH] * p_ref[:, C_AH:C_AH + A_WIDTH]
    w0, w1, w2 = w_ref[0:1, :], w_ref[1:2, :], w_ref[2:3, :]
    b = b_ref[...]

    @pl.when(i == 0)
    def _():
        zbuf[0:8, :] = jnp.zeros((8, A_WIDTH), F32)

    @pl.when(i < n_blk - 1)
    def _():
        zbuf[8:8 + tb, :] = z
        y = b + w0 * zbuf[6:6 + tb, :] + w1 * zbuf[7:7 + tb, :] + w2 * zbuf[8:8 + tb, :]
        za_ref[...] = (ab * y).astype(BF16)
        zbuf[0:8, :] = zbuf[tb:tb + 8, :]

    @pl.when(i == p_blk)
    def _():
        pc_ref[...] = z[p_off - 1:p_off + 1, :]

    @pl.when(i == n_blk - 1)
    def _():
        for s in range(tb // dec_seq):
            r0 = s * dec_seq
            zbuf[6:8, :] = st_ref[2 * s:2 * s + 2, :]
            zbuf[8:8 + dec_seq, :] = z[r0:r0 + dec_seq, :]
            y = (b + w0 * zbuf[6:6 + dec_seq, :] + w1 * zbuf[7:7 + dec_seq, :]
                 + w2 * zbuf[8:8 + dec_seq, :])
            za_ref[r0:r0 + dec_seq, :] = (ab[r0:r0 + dec_seq, :] * y).astype(BF16)
            sc_ref[2 * s:2 * s + 2, :] = z[r0 + dec_seq - 2:r0 + dec_seq, :]


def _conv(proj, conv_w, conv_b, state, lay):
    R = proj.shape[0]
    tb = lay["ns"]
    n_blk = R // tb
    last = lay["p_end"] - 1
    kern = functools.partial(_conv_kernel, p_blk=last // tb, p_off=last % tb, n_blk=n_blk, tb=tb,
                             dec_seq=lay["dec_seq"])
    assert last % tb >= 1
    nseq = tb // lay["dec_seq"]
    return pl.pallas_call(
        kern,
        grid=(n_blk,),
        in_specs=[pl.BlockSpec((tb, 3 * A_WIDTH), lambda i: (i, 0)),
                  pl.BlockSpec((CONV_WIDTH, A_WIDTH), lambda i: (0, 0)),
                  pl.BlockSpec((1, A_WIDTH), lambda i: (0, 0)),
                  pl.BlockSpec((2 * nseq, A_WIDTH), lambda i: (0, 0))],
        out_specs=[pl.BlockSpec((tb, A_WIDTH), lambda i: (i, 0)),
                   pl.BlockSpec((2, A_WIDTH), lambda i: (0, 0)),
                   pl.BlockSpec((2 * nseq, A_WIDTH), lambda i: (0, 0))],
        out_shape=[jax.ShapeDtypeStruct((R, A_WIDTH), BF16),
                   jax.ShapeDtypeStruct((2, A_WIDTH), F32),
                   jax.ShapeDtypeStruct((2 * nseq, A_WIDTH), F32)],
        scratch_shapes=[pltpu.VMEM((tb + 8, A_WIDTH), F32)],
        compiler_params=_cparams("arbitrary"),
        name="short_conv",
    )(proj, conv_w, conv_b, state)


QK_PAD = 256
Q_SCALE = ATTN_SCALE * math.log2(math.e)


V_ROWS = V_DIM + 16


def _qkv_kernel(cq_ref, ckv_ref, kp_ref, cs_ref, qn_ref, wq_ref, kn_ref, wk_ref, wvt_ref,
                q_ref, k_ref, vt_ref, lat_ref, kpe_ref, *, tb):
    cs = cs_ref[...]
    qc = _rms(cq_ref[...], qn_ref[...]).astype(BF16)
    q = _dot(qc, wq_ref[...])
    lat = _rms(ckv_ref[...], kn_ref[...])
    lat_ref[...] = lat
    lat_b = lat.astype(BF16)
    kn = _dot(lat_b, wk_ref[...])
    ones_row = jnp.where(lax.broadcasted_iota(jnp.int32, (V_ROWS - V_DIM, tb), 0) == 0, 1.0, 0.0).astype(BF16)
    lane = lax.broadcasted_iota(jnp.int32, (tb, 2 * ROPE_DIM), 1)
    row = pl.program_id(0) * tb + lax.broadcasted_iota(jnp.int32, (tb, 2 * ROPE_DIM), 0)
    q_tail = jnp.where(lane == ROPE_DIM, 1.0, 0.0)
    k_tail = jnp.where((lane == ROPE_DIM) & (row < FRONT - N_META), NEG_INF, 0.0)
    t = kp_ref[...] * cs
    kpe = t + pltpu.roll(t, ROPE_DIM, 1)
    kpe_ref[...] = kpe[:, 0:ROPE_DIM]
    k_hi = jnp.where(lane < ROPE_DIM, kpe, k_tail).astype(BF16)
    for h in range(MLA_HEADS):
        c = h * 256
        q_ref[h, :, 0:NOPE_DIM] = (q[:, c:c + NOPE_DIM] * Q_SCALE).astype(BF16)
        t = q[:, c + NOPE_DIM:c + 256] * cs
        qpe = (t + pltpu.roll(t, ROPE_DIM, 1)) * Q_SCALE
        q_ref[h, :, NOPE_DIM:QK_PAD] = jnp.where(lane < ROPE_DIM, qpe, q_tail).astype(BF16)
        k_ref[h, :, 0:NOPE_DIM] = kn[:, h * NOPE_DIM:(h + 1) * NOPE_DIM].astype(BF16)
        k_ref[h, :, NOPE_DIM:QK_PAD] = k_hi
        vt_ref[h, 0:V_DIM, :] = _dot_nt(wvt_ref[h], lat_b).astype(BF16)
        vt_ref[h, V_DIM:V_ROWS, :] = ones_row


def _qkv(proj, cs, q_norm, wq_aug, kv_norm, w_k, w_vt):
    R = proj.shape[0]
    tb = 256
    H = MLA_HEADS
    return pl.pallas_call(
        functools.partial(_qkv_kernel, tb=tb),
        grid=(R // tb,),
        in_specs=[pl.BlockSpec((tb, Q_LORA), lambda i: (i, C_Q // Q_LORA)),
                  pl.BlockSpec((tb, KV_LORA), lambda i: (i, C_KV // KV_LORA)),
                  pl.BlockSpec((tb, 2 * ROPE_DIM), lambda i: (i, C_KPE // (2 * ROPE_DIM))),
                  pl.BlockSpec((tb, 2 * ROPE_DIM), lambda i: (i, 0)),
                  pl.BlockSpec((1, Q_LORA), lambda i: (0, 0)),
                  pl.BlockSpec((Q_LORA, H * 256), lambda i: (0, 0)),
                  pl.BlockSpec((1, KV_LORA), lambda i: (0, 0)),
                  pl.BlockSpec((KV_LORA, H * NOPE_DIM), lambda i: (0, 0)),
                  pl.BlockSpec((H, V_DIM, KV_LORA), lambda i: (0, 0, 0))],
        out_specs=[pl.BlockSpec((H, tb, QK_PAD), lambda i: (0, i, 0)),
                   pl.BlockSpec((H, tb, QK_PAD), lambda i: (0, i, 0)),
                   pl.BlockSpec((H, V_ROWS, tb), lambda i: (0, 0, i)),
                   pl.BlockSpec((tb, KV_LORA), lambda i: (i, 0)),
                   pl.BlockSpec((tb, ROPE_DIM), lambda i: (i, 0))],
        out_shape=[jax.ShapeDtypeStruct((H, R, QK_PAD), BF16),
                   jax.ShapeDtypeStruct((H, R, QK_PAD), BF16),
                   jax.ShapeDtypeStruct((H, V_ROWS, R), BF16),
                   jax.ShapeDtypeStruct((R, KV_LORA), F32),
                   jax.ShapeDtypeStruct((R, ROPE_DIM), F32)],
        compiler_params=_cparams("parallel"),
        name="qkv_rope",
    )(proj, proj, proj, cs, q_norm, wq_aug, kv_norm, w_k, w_vt)


FLASH_TQ = 512
FLASH_TK = 512


def _flash_kernel(it_ref, jt_ref, q_ref, k_ref, vt_ref, o_ref, m_ref, acc_ref, *, bq):
    tq, tk = FLASH_TQ, FLASH_TK
    t = pl.program_id(1)
    i = it_ref[t]
    j = jt_ref[t]

    @pl.when(j == 0)
    def _():
        m_ref[...] = jnp.full(m_ref.shape, NEG_INF, F32)
        acc_ref[...] = jnp.zeros(acc_ref.shape, F32)

    def run(diag):
        tiles = [(a, b) for a in range(bq // tq) for b in range(bq // tk)
                 if not (diag and (b * tk) // CHUNK > (a * tq + tq - 1) // CHUNK)]

        def scores(a, b):
            return _dot_nt(k_ref[b * tk:(b + 1) * tk, :], q_ref[a * tq:(a + 1) * tq, :])

        tiles.sort(key=lambda ab: (ab[1], ab[0]))
        cols = {a: slice(a * tq, (a + 1) * tq) for a, _ in tiles}
        m = {a: m_ref[:, c] for a, c in cols.items()}
        acc = {a: acc_ref[:, c] for a, c in cols.items()}
        def accumulate(a, b, alpha, p):
            acc[a] = alpha * acc[a] + _dot(vt_ref[:, b * tk:(b + 1) * tk], p)

        s_next = scores(*tiles[0])
        pending = None
        for n, (a, b) in enumerate(tiles):
            s = s_next
            if n + 1 < len(tiles):
                s_next = scores(*tiles[n + 1])
            if pending is not None:
                accumulate(*pending)
            if diag and (b * tk + tk - 1) // CHUNK > (a * tq) // CHUNK:
                kc = (b * tk + lax.broadcasted_iota(jnp.int32, (tk, tq), 0)) // CHUNK
                qc = (a * tq + lax.broadcasted_iota(jnp.int32, (tk, tq), 1)) // CHUNK
                s = jnp.where(kc <= qc, s, NEG_INF)
            m_new = jnp.maximum(m[a], jnp.max(s, axis=0, keepdims=True))
            alpha = jnp.exp2(m[a] - m_new)
            pending = (a, b, alpha, jnp.exp2(s - m_new).astype(BF16))
            m[a] = m_new
        accumulate(*pending)
        for a, c in cols.items():
            if diag:
                row = i * bq + a * tq + lax.broadcasted_iota(jnp.int32, (V_DIM, tq), 1)
                o = jnp.where(row >= FRONT - N_META, acc[a][0:V_DIM, :] / acc[a][V_DIM:V_DIM + 1, :], 0.0)
                o_ref[c, :] = o.T.astype(o_ref.dtype)
            else:
                m_ref[:, c], acc_ref[:, c] = m[a], acc[a]

    @pl.when(j < i)
    def _():
        run(False)

    @pl.when(j == i)
    def _():
        run(True)


def _flash(q, k, vt):
    H, R, _ = q.shape
    bq = next(b for b in (1536, 1024, 512) if R % b == 0)
    n = R // bq
    pairs = [(i, j) for i in range(n) for j in range(i + 1)]
    it = jnp.array([p[0] for p in pairs], jnp.int32)
    jt = jnp.array([p[1] for p in pairs], jnp.int32)
    grid_spec = pltpu.PrefetchScalarGridSpec(
        num_scalar_prefetch=2,
        grid=(H, len(pairs)),
        in_specs=[pl.BlockSpec((None, bq, QK_PAD), lambda h, t, it, jt: (h, it[t], 0)),
                  pl.BlockSpec((None, bq, QK_PAD), lambda h, t, it, jt: (h, jt[t], 0)),
                  pl.BlockSpec((None, V_ROWS, bq), lambda h, t, it, jt: (h, 0, jt[t]))],
        out_specs=pl.BlockSpec((bq, V_DIM), lambda h, t, it, jt: (it[t], h)),
        scratch_shapes=[pltpu.VMEM((1, bq), F32), pltpu.VMEM((V_ROWS, bq), F32)],
    )
    return pl.pallas_call(
        functools.partial(_flash_kernel, bq=bq),
        grid_spec=grid_spec,
        out_shape=jax.ShapeDtypeStruct((R, H * V_DIM), BF16),
        compiler_params=_cparams("parallel", "arbitrary"),
        name="prompt_attention",
    )(it, jt, q, k, vt)


def _cached_attn_kernel(q_ref, lat_ref, kpe_ref, w_ref, o_all_ref, o_ref):
    del o_all_ref
    kv = _dot(lat_ref[...], w_ref[...]).astype(BF16)
    kpe = kpe_ref[...]
    for h in range(MLA_HEADS):
        c = h * 256
        qh = q_ref[h]
        s = _dot_nt(qh[:, 0:NOPE_DIM], kv[:, c:c + NOPE_DIM]) + _dot_nt(qh[:, NOPE_DIM:QK_DIM], kpe)
        m = jnp.max(s, axis=1, keepdims=True)
        p = jnp.exp2(s - m)
        p = p / jnp.sum(p, axis=1, keepdims=True)
        o_ref[:, h * V_DIM:(h + 1) * V_DIM] = _dot(p.astype(BF16), kv[:, c + NOPE_DIM:c + 256]).astype(o_ref.dtype)


def _cached_attn(q, lat_all, kpe_all, w_ukv, o_all, lay):
    H = MLA_HEADS
    B, Lk, _ = lat_all.shape
    S = lay["dec_seq"]
    blk0 = lay["s0"] // S
    return pl.pallas_call(
        _cached_attn_kernel,
        grid=(B,),
        in_specs=[pl.BlockSpec((H, S, QK_PAD), lambda b: (0, blk0 + b, 0)),
                  pl.BlockSpec((None, Lk, KV_LORA), lambda b: (b, 0, 0)),
                  pl.BlockSpec((None, Lk, ROPE_DIM), lambda b: (b, 0, 0)),
                  pl.BlockSpec((KV_LORA, H * 256), lambda b: (0, 0)),
                  pl.BlockSpec(memory_space=pl.ANY)],
        out_specs=pl.BlockSpec((S, H * V_DIM), lambda b: (blk0 + b, 0)),
        out_shape=jax.ShapeDtypeStruct(o_all.shape, o_all.dtype),
        input_output_aliases={4: 0},
        compiler_params=_cparams("parallel"),
        name="sample_attention",
    )(q, lat_all, kpe_all, w_ukv, o_all)


S5_BLOCKS = 4


def _s5_kernel(u_ref, bm_ref, a_ref, cm_ref, d_ref, h0_ref, y_ref, ps_ref, ss_ref, hre, him, carry,
               *, p_blk, p_off, n_blk, tb, dec_seq):
    i = pl.program_id(0)
    u = u_ref[...]
    u_b = u.astype(BF16)
    cin, cst = SSM_WIDTH // S5_BLOCKS, SSM_CH // S5_BLOCKS
    for k in range(S5_BLOCKS):
        hb = _dot(u_b[:, k * cin:(k + 1) * cin], bm_ref[k])
        hre[:, k * cst:(k + 1) * cst] = hb[:, 0:cst]
        him[:, k * cst:(k + 1) * cst] = hb[:, cst:2 * cst]
    ar = a_ref[0:1, :]
    ai = a_ref[1:2, :]

    def scan(start, n, hr, hi):
        def body(t, c):
            hr, hi = c
            r = start + t
            nr = ar * hr - ai * hi + hre[pl.ds(r, 1), :]
            ni = ar * hi + ai * hr + him[pl.ds(r, 1), :]
            hre[pl.ds(r, 1), :] = nr
            him[pl.ds(r, 1), :] = ni
            return nr, ni
        return lax.fori_loop(0, n, body, (hr, hi), unroll=2)

    @pl.when(i == 0)
    def _():
        carry[...] = jnp.zeros(carry.shape, F32)

    @pl.when(i < n_blk - 1)
    def _():
        hr, hi = scan(0, tb, carry[0:1, :], carry[1:2, :])
        carry[0:1, :] = hr
        carry[1:2, :] = hi

    @pl.when(i == p_blk)
    def _():
        ps_ref[0:1, :] = hre[p_off:p_off + 1, :]
        ps_ref[1:2, :] = him[p_off:p_off + 1, :]

    @pl.when(i == n_blk - 1)
    def _():
        for s in range(tb // dec_seq):
            hr, hi = scan(s * dec_seq, dec_seq, h0_ref[0, s:s + 1, :], h0_ref[1, s:s + 1, :])
            ss_ref[0, s:s + 1, :] = hr
            ss_ref[1, s:s + 1, :] = hi

    y = jnp.concatenate(
        [_dot(hre[:, k * cst:(k + 1) * cst].astype(BF16), cm_ref[0, k])
         + _dot(him[:, k * cst:(k + 1) * cst].astype(BF16), cm_ref[1, k]) for k in range(S5_BLOCKS)], axis=1)
    y_ref[...] = _gelu(y + d_ref[...] * u).astype(BF16)


def _s5(proj, bmat, abar, cmat, d, h0, lay):
    R = proj.shape[0]
    tb = lay["ns"]
    n_blk = R // tb
    last = lay["p_end"] - 1
    nseq = tb // lay["dec_seq"]
    kern = functools.partial(_s5_kernel, p_blk=last // tb, p_off=last % tb, n_blk=n_blk, tb=tb,
                             dec_seq=lay["dec_seq"])
    return pl.pallas_call(
        kern,
        grid=(n_blk,),
        in_specs=[pl.BlockSpec((tb, SSM_WIDTH), lambda i: (i, C_U // SSM_WIDTH)),
                  pl.BlockSpec(bmat.shape, lambda i: (0, 0, 0)),
                  pl.BlockSpec((2, SSM_CH), lambda i: (0, 0)),
                  pl.BlockSpec(cmat.shape, lambda i: (0, 0, 0, 0)),
                  pl.BlockSpec((1, SSM_WIDTH), lambda i: (0, 0)),
                  pl.BlockSpec((2, nseq, SSM_CH), lambda i: (0, 0, 0))],
        out_specs=[pl.BlockSpec((tb, SSM_WIDTH), lambda i: (i, 0)),
                   pl.BlockSpec((2, SSM_CH), lambda i: (0, 0)),
                   pl.BlockSpec((2, nseq, SSM_CH), lambda i: (0, 0, 0))],
        out_shape=[jax.ShapeDtypeStruct((R, SSM_WIDTH), BF16),
                   jax.ShapeDtypeStruct((2, SSM_CH), F32),
                   jax.ShapeDtypeStruct((2, nseq, SSM_CH), F32)],
        scratch_shapes=[pltpu.VMEM((tb, SSM_CH), F32), pltpu.VMEM((tb, SSM_CH), F32),
                        pltpu.VMEM((2, SSM_CH), F32)],
        compiler_params=_cparams("arbitrary"),
        name="s5_scan",
    )(proj, bmat, abar, cmat, d, h0)


MERGE_TN = 256
MERGE_NC = D_MODEL // MERGE_TN


def _merge_kernel(x_ref, g_ref, za_ref, o_ref, yc_ref, wg0_ref, wg1_ref, wg2_ref, b0_ref, b1_ref, b2_ref,
                  wa_ref, wb_ref, wga_ref, wgb_ref, wo_ref, out_ref, xn_s, mg_s):
    c = pl.program_id(1)

    @pl.when(c == 0)
    def _():
        xn_s[...] = _rms(x_ref[...], g_ref[...]).astype(BF16)

    xn = xn_s[...]
    g0 = _sigmoid(_dot(xn, wg0_ref[...]) + b0_ref[...])
    g1 = _sigmoid(_dot(xn, wg1_ref[...]) + b1_ref[...])
    g2 = _sigmoid(_dot(xn, wg2_ref[...]) + b2_ref[...])
    yc = yc_ref[...]
    y_a = _dot(za_ref[...], wa_ref[...])
    y_b = _dot(o_ref[...], wb_ref[...])
    y_c = _dot(yc, wga_ref[...]) * _sigmoid(_dot(yc, wgb_ref[...]))
    mg_s[c] = (g0 * y_a + g1 * y_b + g2 * y_c).astype(BF16)

    @pl.when(c == MERGE_NC - 1)
    def _():
        acc = x_ref[...]
        for cc in range(MERGE_NC):
            acc = acc + _dot(mg_s[cc], wo_ref[cc * MERGE_TN:(cc + 1) * MERGE_TN, :])
        out_ref[...] = acc


def _merge(x, g, za, o, yc, w_gate, b_gate, w_a, w_b, w_glu, w_o):
    R = x.shape[0]
    tb = 512
    tn, nc = MERGE_TN, MERGE_NC
    row = lambda w: pl.BlockSpec((tb, w), lambda i, c: (i, 0))
    col = lambda k, off: pl.BlockSpec((k, tn), lambda i, c: (0, off + c))
    return pl.pallas_call(
        _merge_kernel,
        grid=(R // tb, nc),
        in_specs=[row(D_MODEL), pl.BlockSpec((1, D_MODEL), lambda i, c: (0, 0)),
                  row(A_WIDTH), row(MLA_HEADS * V_DIM), row(SSM_WIDTH),
                  col(D_MODEL, 0), col(D_MODEL, nc), col(D_MODEL, 2 * nc),
                  col(1, 0), col(1, nc), col(1, 2 * nc),
                  col(A_WIDTH, 0), col(MLA_HEADS * V_DIM, 0), col(SSM_WIDTH, 0), col(SSM_WIDTH, nc),
                  pl.BlockSpec((D_MODEL, D_MODEL), lambda i, c: (0, 0))],
        out_specs=row(D_MODEL),
        out_shape=jax.ShapeDtypeStruct((R, D_MODEL), F32),
        scratch_shapes=[pltpu.VMEM((tb, D_MODEL), BF16), pltpu.VMEM((nc, tb, tn), BF16)],
        compiler_params=_cparams("parallel", "arbitrary"),
        name="branch_merge",
    )(x, g, za, o, yc, w_gate, w_gate, w_gate, b_gate, b_gate, b_gate, w_a, w_b, w_glu, w_glu, w_o)


SUBLANES = 8


def _sort_desc(v):
    n = len(v)
    v = list(v)
    k = 2
    while k <= n:
        j = k // 2
        while j >= 1:
            for i in range(n):
                l = i ^ j
                if l > i:
                    hi, lo = jnp.maximum(v[i], v[l]), jnp.minimum(v[i], v[l])
                    v[i], v[l] = (hi, lo) if (i & k) == 0 else (lo, hi)
            j //= 2
        k *= 2
    return v


def _merge_top(a, b):
    n = len(a)
    v = [jnp.maximum(a[i], b[n - 1 - i]) for i in range(n)]
    j = n // 2
    while j >= 1:
        for i in range(n):
            l = i ^ j
            if l > i:
                v[i], v[l] = jnp.maximum(v[i], v[l]), jnp.minimum(v[i], v[l])
        j //= 2
    return v


def _top_desc(tiles):
    v = _sort_desc(tiles)
    shift = SUBLANES // 2
    while shift >= 1:
        v = _merge_top(v, [pltpu.roll(x, shift, 0) for x in v])
        shift //= 2
    return v


def _pack_sublanes(rows, sub):
    out = rows[0]
    for r in range(1, len(rows)):
        out = jnp.where(sub == r, rows[r], out)
    return out


def _route_kernel(x_ref, g_ref, wq_ref, k1_ref, k2_ref, xnt_ref, th_ref, s2_ref, e1_ref, e2_ref, *, tb):
    K, NK, S = PEER_TOPK, PEER_KEYS, SUBLANES
    xn32 = _rms(x_ref[...], g_ref[...])
    xnt_ref[...] = xn32.T.astype(BF16)
    xn = xn32.astype(BF16)
    q = _dot(xn, wq_ref[...]).astype(BF16)
    half = PEER_QDIM // 2
    sub = lax.broadcasted_iota(jnp.int32, (S, tb), 0)
    neg = jnp.full((S, tb), -jnp.inf, F32)
    for h in range(PEER_HEADS):
        c = h * PEER_QDIM
        s1 = _dot_nt(k1_ref[...], q[:, c:c + half])
        s2 = _dot_nt(k2_ref[...], q[:, c + half:c + PEER_QDIM])
        t1 = [s1[S * r:S * (r + 1), :] for r in range(NK // S)]
        t2 = [s2[S * r:S * (r + 1), :] for r in range(NK // S)]
        v1 = _top_desc(t1)
        v2 = _top_desc(t2)
        v2_lo, v2_hi = _pack_sublanes(v2[0:S], sub), _pack_sublanes(v2[S:K], sub)
        v1_hi = _pack_sublanes(v1[S:K], sub)
        cand = [v1[0] + v2_lo, v1[0] + v2_hi] + [v1[a] + v2_lo for a in range(1, S)] + [v1_hi + v2[0]]
        vals = _top_desc(cand + [neg] * (K - len(cand)))
        tau = vals[K - 1]
        z = 1.0 + jnp.exp(vals[1] - vals[0])
        for r in range(2, K):
            z = z + jnp.exp(vals[r] - vals[0])
        scale = 0.5 / z
        for r in range(NK // S):
            rows = slice(S * r, S * (r + 1))
            th = jnp.full((S, tb), jnp.inf, F32)
            for b in range(K):
                th = jnp.where(t1[r] + v2[b] >= tau, v2[b], th)
            th_ref[h, rows, :] = th
            e1_ref[h, rows, :] = jnp.exp(t1[r] - v1[0]) * scale
            e2_ref[h, rows, :] = jnp.exp(t2[r] - v2[0])
        s2_ref[h] = s2


def _route(x, g, wq, k1, k2):
    R = x.shape[0]
    tb = 128
    H, NK = PEER_HEADS, PEER_KEYS
    tmap = pl.BlockSpec((H, None, NK, tb), lambda i: (0, i, 0, 0))
    tshape = jax.ShapeDtypeStruct((H, R // tb, NK, tb), F32)
    return pl.pallas_call(
        functools.partial(_route_kernel, tb=tb),
        grid=(R // tb,),
        in_specs=[pl.BlockSpec((tb, D_MODEL), lambda i: (i, 0)),
                  pl.BlockSpec((1, D_MODEL), lambda i: (0, 0)),
                  pl.BlockSpec((D_MODEL, H * PEER_QDIM), lambda i: (0, 0)),
                  pl.BlockSpec((NK, PEER_QDIM // 2), lambda i: (0, 0)),
                  pl.BlockSpec((NK, PEER_QDIM // 2), lambda i: (0, 0))],
        out_specs=[pl.BlockSpec((D_MODEL, tb), lambda i: (0, i)), tmap, tmap, tmap, tmap],
        out_shape=[jax.ShapeDtypeStruct((D_MODEL, R), BF16), tshape, tshape, tshape, tshape],
        compiler_params=_cparams("parallel"),
        name="peer_route",
    )(x, g, wq, k1, k2)


PEER_SUB = 8
PEER_EB = PEER_SUB * PEER_KEYS
PEER_TC = 128
PEER_OUT_ROWS = 256
PEER_ACT_ROWS = 128
GELU_C1 = math.sqrt(2.0 / math.pi)
GELU_C2 = GELU_C1 * 0.044715


def _expert_kernel(x_ref, xnt_ref, th_ref, s2_ref, e1_ref, e2_ref, eu_ref, evt_ref, gf_ref, out_ref, acc_ref,
                   act_even, act_odd, p_even, p_odd, *, tb, ne, final_norm):
    s = pl.program_id(0)
    NK = PEER_KEYS
    proj_first = (s < 2) | ((s - 2) % ne == 0)
    proj_last = (s >= 2) & ((s - 2) % ne == ne - 1)

    @pl.when(s == 0)
    def _():
        act_odd[...] = jnp.zeros(act_odd.shape, F32)
        p_odd[...] = jnp.zeros(p_odd.shape, BF16)

    @pl.when(proj_first)
    def _():
        acc_ref[...] = jnp.zeros(acc_ref.shape, F32)

    n_piece = PEER_SUB * (tb // PEER_TC)
    n_out = D_MODEL // PEER_OUT_ROWS

    n_act = PEER_EB // PEER_ACT_ROWS

    def gate_and_project(p_new, p_old, act_new, act_old):
        for piece in range(n_piece):
            if piece < n_piece // 2 and piece % (n_piece // 2 // n_out) == 0:
                c = piece // (n_piece // 2 // n_out)
                orow = slice(c * PEER_OUT_ROWS, (c + 1) * PEER_OUT_ROWS)
                acc_ref[orow, :] += _dot(evt_ref[orow, :], p_old[...])
            if piece >= n_piece // 2 and piece % (n_piece // 2 // n_act) == 0:
                c = (piece - n_piece // 2) // (n_piece // 2 // n_act)
                erow = slice(c * PEER_ACT_ROWS, (c + 1) * PEER_ACT_ROWS)
                act = _dot(eu_ref[erow, :], xnt_ref[...])
                for t in range(tb // PEER_TC):
                    act_new[t, erow, :] = act[:, t * PEER_TC:(t + 1) * PEER_TC]
            sub, tc = divmod(piece, tb // PEER_TC)
            rows = slice(sub * NK, (sub + 1) * NK)
            lanes = slice(tc * PEER_TC, (tc + 1) * PEER_TC)
            w = jnp.zeros((NK, PEER_TC), F32)
            for h in range(PEER_HEADS):
                sel = s2_ref[h, tc] >= th_ref[h, tc, sub:sub + 1, :]
                w = w + jnp.where(sel, e1_ref[h, tc, sub:sub + 1, :] * e2_ref[h, tc], 0.0)
            a = act_old[tc, rows, :]
            wa = w * a
            p_new[rows, lanes] = (wa + wa * jnp.tanh(a * (GELU_C1 + GELU_C2 * (a * a)))).astype(BF16)

    @pl.when(s % 2 == 0)
    def _():
        gate_and_project(p_even, p_odd, act_even, act_odd)

    @pl.when(s % 2 == 1)
    def _():
        gate_and_project(p_odd, p_even, act_odd, act_even)

    @pl.when(proj_last)
    def _():
        y = x_ref[...] + acc_ref[...].T
        out_ref[...] = _rms(y, gf_ref[...]) if final_norm else y


def _experts(x, xn, th, s2, e1, e2, e_u, e_v, g_final, final_norm):
    R = x.shape[0]
    tb = 512
    H, NK = PEER_HEADS, PEER_KEYS
    ne = e_u.shape[0] // PEER_EB
    e_vt = e_v.reshape(ne, PEER_EB, D_MODEL).transpose(0, 2, 1)
    total = (R // tb) * ne
    pair = lambda s, lag: jnp.clip(s - lag, 0, total - 1)
    nt = tb // PEER_TC
    tmap = pl.BlockSpec((H, nt, NK, PEER_TC), lambda s: (0, pair(s, 1) // ne, 0, 0))
    kmap = pl.BlockSpec((H, nt, PEER_SUB, PEER_TC), lambda s: (0, pair(s, 1) // ne, pair(s, 1) % ne, 0))
    kern = functools.partial(_expert_kernel, tb=tb, ne=ne, final_norm=final_norm)
    return pl.pallas_call(
        kern,
        grid=(total + 2,),
        in_specs=[pl.BlockSpec((tb, D_MODEL), lambda s: (pair(s, 2) // ne, 0)),
                  pl.BlockSpec((D_MODEL, tb), lambda s: (0, pair(s, 0) // ne)),
                  kmap, tmap, kmap, tmap,
                  pl.BlockSpec((PEER_EB, D_MODEL), lambda s: (pair(s, 0) % ne, 0)),
                  pl.BlockSpec((None, D_MODEL, PEER_EB), lambda s: (pair(s, 2) % ne, 0, 0)),
                  pl.BlockSpec((1, D_MODEL), lambda s: (0, 0))],
        out_specs=pl.BlockSpec((tb, D_MODEL), lambda s: (pair(s, 2) // ne, 0)),
        out_shape=jax.ShapeDtypeStruct((R, D_MODEL), F32),
        scratch_shapes=[pltpu.VMEM((D_MODEL, tb), F32),
                        pltpu.VMEM((nt, PEER_EB, PEER_TC), F32), pltpu.VMEM((nt, PEER_EB, PEER_TC), F32),
                        pltpu.VMEM((PEER_EB, tb), BF16), pltpu.VMEM((PEER_EB, tb), BF16)],
        compiler_params=_cparams("arbitrary"),
        name="peer_experts",
    )(x, xn, th, s2, e1, e2, e_u, e_vt, g_final)


def _s5_params(a_re, a_im, b_re, b_im, c_re, c_im, log_dt):
    dt = jnp.exp(log_dt)[:, None]
    mag = jnp.exp(dt * a_re)
    abar_re = mag * jnp.cos(dt * a_im)
    abar_im = mag * jnp.sin(dt * a_im)
    nr, ni = abar_re - 1.0, abar_im
    den = a_re * a_re + a_im * a_im
    coef_re = (nr * a_re + ni * a_im) / den
    coef_im = (ni * a_re - nr * a_im) / den
    bbar_re = coef_re[..., None] * b_re - coef_im[..., None] * b_im
    bbar_im = coef_re[..., None] * b_im + coef_im[..., None] * b_re
    gb = SSM_GROUPS // S5_BLOCKS
    eye = jnp.eye(gb, dtype=F32)

    def expand_b(b):
        b = b.reshape(S5_BLOCKS, gb, SSM_STATE, SSM_GROUP)
        return jnp.einsum('kgpc,gh->kgchp', b, eye).reshape(S5_BLOCKS, gb * SSM_GROUP, gb * SSM_STATE)

    def expand_c(c):
        c = c.reshape(S5_BLOCKS, gb, SSM_GROUP, SSM_STATE)
        return jnp.einsum('kgcp,gh->kgphc', c, eye).reshape(S5_BLOCKS, gb * SSM_STATE, gb * SSM_GROUP)

    bmat = jnp.concatenate([expand_b(bbar_re), expand_b(bbar_im)], axis=2).astype(BF16)
    cmat = jnp.stack([expand_c(c_re), -expand_c(c_im)]).astype(BF16)
    abar = jnp.stack([abar_re.reshape(SSM_CH), abar_im.reshape(SSM_CH)])
    return bmat, abar, cmat


def _swap_halves(w):
    half = ROPE_DIM // 2
    return jnp.concatenate([w[..., half:], w[..., :half]], axis=-1)


def _rope_table(pos):
    half = ROPE_DIM // 2
    inv = 1.0 / (ROPE_THETA ** (jnp.arange(half, dtype=F32) / half))
    ang = pos[:, None] * inv[None, :]
    cos, sin = jnp.cos(ang), jnp.sin(ang)
    return jnp.concatenate([cos, cos, -sin, sin], axis=1)


def kernel(x_prompt, x_sample, cache_ckv, cache_kpe, state_conv, state_ssm_re, state_ssm_im, meta_tokens, norm_mix, norm_ffn, w_in, b_gate, conv_w, conv_b, w_a_out, q_norm, w_uq, kv_norm, w_ukv, w_b_out, ssm_a_re, ssm_a_im, ssm_b_re, ssm_b_im, ssm_c_re, ssm_c_im, ssm_log_dt, ssm_d, w_glu, w_o, peer_wq, peer_k1, peer_k2, peer_u, peer_v, norm_final):
    B, seq, _ = x_prompt.shape
    assert B == 1
    nb, dec_seq, _ = x_sample.shape
    ns = nb * dec_seq
    past = cache_ckv.shape[2]
    depth = w_in.shape[0]
    p_end = FRONT + seq
    R = -(-(p_end + ns) // ROW_ALIGN) * ROW_ALIGN
    s0 = R - ns
    lay = dict(ns=ns, dec_seq=dec_seq, p_end=p_end, s0=s0)

    x = jnp.concatenate([
        jnp.zeros((FRONT - N_META, D_MODEL), F32), meta_tokens, x_prompt[0],
        jnp.zeros((s0 - p_end, D_MODEL), F32), x_sample.reshape(ns, D_MODEL)], axis=0)
    pos = jnp.concatenate([
        jnp.arange(s0, dtype=F32) - (FRONT - N_META),
        jnp.tile(past + jnp.arange(dec_seq, dtype=F32), nb)])
    cs = _rope_table(pos)

    outs = [[] for _ in range(10)]
    for l in range(depth):
        wl = w_in[l]
        kp = wl[:, 2560:2624]
        w_small = jnp.concatenate([wl[:, 0:2560], wl[:, 2624:3136], kp, _swap_halves(kp)], axis=1).astype(BF16)
        w_gate = wl[:, 3136:].astype(BF16)
        wq3 = w_uq[l].reshape(Q_LORA, MLA_HEADS, QK_DIM)
        wq_aug = jnp.concatenate([wq3, _swap_halves(wq3[..., NOPE_DIM:])], axis=-1).reshape(Q_LORA, MLA_HEADS * 256)
        wq_aug = wq_aug.astype(BF16)
        wkv = w_ukv[l].astype(BF16)
        wkv3 = wkv.reshape(KV_LORA, MLA_HEADS, NOPE_DIM + V_DIM)
        w_k = wkv3[..., :NOPE_DIM].reshape(KV_LORA, MLA_HEADS * NOPE_DIM)
        w_vt = wkv3[..., NOPE_DIM:].transpose(1, 2, 0)
        bmat, abar, cmat = _s5_params(ssm_a_re[l], ssm_a_im[l], ssm_b_re[l], ssm_b_im[l], ssm_c_re[l],
                                      ssm_c_im[l], ssm_log_dt[l])
        h0 = jnp.stack([state_ssm_re[l].reshape(nb, SSM_CH), state_ssm_im[l].reshape(nb, SSM_CH)])

        proj = _in_proj(x, norm_mix[l][None], w_small)
        za, p_conv, s_conv = _conv(proj, conv_w[l], conv_b[l][None], state_conv[l].reshape(2 * nb, A_WIDTH), lay)
        q, k, vt, lat, kpe = _qkv(proj, cs, q_norm[l][None], wq_aug, kv_norm[l][None], w_k, w_vt)
        o = _flash(q, k, vt)
        lat_all = jnp.concatenate([cache_ckv[l], lat[s0:].reshape(nb, dec_seq, KV_LORA)], axis=1).astype(BF16)
        kpe_all = jnp.concatenate([cache_kpe[l], kpe[s0:].reshape(nb, dec_seq, ROPE_DIM)], axis=1).astype(BF16)
        o = _cached_attn(q, lat_all, kpe_all, wkv, o, lay)
        yc, p_state, s_state = _s5(proj, bmat, abar, cmat, ssm_d[l][None], h0, lay)
        x = _merge(x, norm_mix[l][None], za, o, yc, w_gate, b_gate[l][None], w_a_out[l].astype(BF16),
                   w_b_out[l].astype(BF16), w_glu[l].astype(BF16), w_o[l].astype(BF16))
        xn2, th, s2, e1, e2 = _route(x, norm_ffn[l][None], peer_wq[l].astype(BF16),
                                     peer_k1[l].astype(BF16), peer_k2[l].astype(BF16))
        x = _experts(x, xn2, th, s2, e1, e2, peer_u[l].astype(BF16), peer_v[l].astype(BF16),
                     norm_final[None], final_norm=(l == depth - 1))

        lo = FRONT - N_META
        outs[0].append(lat[lo:p_end][None])
        outs[1].append(kpe[lo:p_end][None])
        outs[2].append(p_conv[None])
        outs[3].append(p_state[0].reshape(1, SSM_GROUPS, SSM_STATE))
        outs[4].append(p_state[1].reshape(1, SSM_GROUPS, SSM_STATE))
        outs[5].append(lat[s0:].reshape(nb, dec_seq, KV_LORA))
        outs[6].append(kpe[s0:].reshape(nb, dec_seq, ROPE_DIM))
        outs[7].append(s_conv.reshape(nb, CONV_WIDTH - 1, A_WIDTH))
        outs[8].append(s_state[0].reshape(nb, SSM_GROUPS, SSM_STATE))
        outs[9].append(s_state[1].reshape(nb, SSM_GROUPS, SSM_STATE))

    y_prompt = x[FRONT:p_end][None]
    y_sample = x[s0:].reshape(nb, dec_seq, D_MODEL)
    return (y_prompt, y_sample) + tuple(jnp.stack(o) for o in outs)
```
